```python
import math
import jax, jax.numpy as jnp
from jax import lax
import numpy as np

D_MODEL = 2048
BATCH = 2
SEQ = 8192
DEPTH = 2

N_HEADS = 16
N_KV_HEADS = 4
HEAD_DIM = D_MODEL // N_HEADS
GROUP = N_HEADS // N_KV_HEADS
QKV_DIM = (N_HEADS + 2 * N_KV_HEADS) * HEAD_DIM
WINDOW = 128
BLOCK = 128
KEY_SPAN = BLOCK + 2 * WINDOW
NUM_BUCKETS = 32
MAX_DISTANCE = 128
POOL_WINDOWS = (2, 4, 8, 16)
N_POOL_GROUPS = len(POOL_WINDOWS)
POOL_GROUP_DIM = D_MODEL // N_POOL_GROUPS
D_FF = 5632
N_EXPERTS = 8
TOP_K = 2
D_FF_EXPERT = 7168
EPS = 1e-6
NEG_INF = -1e30

kernel_name = "hybrid_window_gqa_pool_moe_encoder"


def rms_norm(x, g):
    xf = x.astype(jnp.float32)
    y = xf * lax.rsqrt(jnp.mean(xf * xf, axis=-1, keepdims=True) + EPS) * g.astype(jnp.float32)
    return y.astype(x.dtype)


def t5_bucket(rel):
    half = NUM_BUCKETS // 2
    ret = jnp.where(rel > 0, half, 0)
    n = jnp.abs(rel)
    max_exact = half // 2
    nf = jnp.maximum(n, 1).astype(jnp.float32)
    large = max_exact + (jnp.log(nf / max_exact) / math.log(MAX_DISTANCE / max_exact)
                         * (half - max_exact)).astype(jnp.int32)
    large = jnp.minimum(large, half - 1)
    return ret + jnp.where(n < max_exact, n, large)


def window_attention(h, w_qkv, q_gain, k_gain, sink, w_o, rel_bias):
    B, S, _ = h.shape
    nb = S // BLOCK
    qkv = h @ w_qkv
    q = qkv[..., :N_HEADS * HEAD_DIM].reshape(B, S, N_KV_HEADS, GROUP, HEAD_DIM)
    k = qkv[..., N_HEADS * HEAD_DIM:(N_HEADS + N_KV_HEADS) * HEAD_DIM].reshape(B, S, N_KV_HEADS, HEAD_DIM)
    v = qkv[..., (N_HEADS + N_KV_HEADS) * HEAD_DIM:].reshape(B, S, N_KV_HEADS, HEAD_DIM)
    q = rms_norm(q, q_gain)
    k = rms_norm(k, k_gain)

    def band(t):
        tp = jnp.pad(t, ((0, 0), (WINDOW, WINDOW), (0, 0), (0, 0)))
        tb = tp.reshape(B, nb + 2, BLOCK, N_KV_HEADS, HEAD_DIM)
        return jnp.concatenate([tb[:, :-2], tb[:, 1:-1], tb[:, 2:]], axis=2)

    kw, vw = band(k), band(v)
    qb = q.reshape(B, nb, BLOCK, N_KV_HEADS, GROUP, HEAD_DIM)
    logits = jnp.einsum('bnqhgd,bnkhd->bnhgqk', qb, kw).astype(jnp.float32) * (HEAD_DIM ** -0.5)

    qi = jnp.arange(BLOCK)[:, None]
    kj = jnp.arange(KEY_SPAN)[None, :]
    rel = kj - WINDOW - qi
    bias = rel_bias.astype(jnp.float32)[t5_bucket(rel)]
    bias = jnp.transpose(bias, (2, 0, 1)).reshape(N_KV_HEADS, GROUP, BLOCK, KEY_SPAN)
    key_pos = jnp.arange(nb)[:, None] * BLOCK - WINDOW + jnp.arange(KEY_SPAN)[None, :]
    valid = (jnp.abs(rel) <= WINDOW)[None] & ((key_pos >= 0) & (key_pos < S))[:, None, :]

    logits = jnp.where(valid[None, :, None, None], logits + bias[None, None], NEG_INF)
    sink_b = sink.astype(jnp.float32).reshape(N_KV_HEADS, GROUP)[None, None, :, :, None, None]
    m = jnp.maximum(jnp.max(logits, axis=-1, keepdims=True), sink_b)
    p = jnp.exp(logits - m)
    w = p / (jnp.sum(p, axis=-1, keepdims=True) + jnp.exp(sink_b - m))
    out = jnp.einsum('bnhgqk,bnkhd->bnqhgd', w.astype(vw.dtype), vw)
    return out.reshape(B, S, N_HEADS * HEAD_DIM) @ w_o


def multiscale_pool(h, w_in, w_group, scale, w_out):
    B, S, _ = h.shape
    u = (h @ w_in).astype(jnp.float32)
    c = jnp.pad(jnp.cumsum(u, axis=1), ((0, 0), (1, 0), (0, 0)))
    t = jnp.arange(S)
    diffs = []
    for g, win in enumerate(POOL_WINDOWS):
        sl = slice(g * POOL_GROUP_DIM, (g + 1) * POOL_GROUP_DIM)
        lo = jnp.clip(t - win // 2, 0, S)
        hi = jnp.clip(t + win // 2, 0, S)
        cg = c[:, :, sl]
        mean = (cg[:, hi] - cg[:, lo]) / (hi - lo).astype(jnp.float32)[:, None]
        diffs.append(mean - u[:, :, sl])
    d = jnp.stack(diffs, axis=2).astype(h.dtype)
    y = jnp.einsum('bsgc,gce->bsge', d, w_group).reshape(B, S, D_MODEL) * scale
    return y @ w_out


def swiglu(t, wg, wu, wd):
    return (jax.nn.silu(t @ wg) * (t @ wu)) @ wd


def moe_swiglu(h, router_w, router_b, w_gate, w_up, w_down):
    B, S, D = h.shape
    t = h.reshape(B * S, D)
    logits = (t @ router_w).astype(jnp.float32) + router_b.astype(jnp.float32)
    top_v, top_i = lax.top_k(logits, TOP_K)
    top_w = jax.nn.softmax(top_v, axis=-1)
    gates = jnp.sum(jax.nn.one_hot(top_i, N_EXPERTS, dtype=jnp.float32) * top_w[..., None], axis=1)
    y = jnp.zeros_like(t)
    for e in range(N_EXPERTS):
        y = y + gates[:, e:e + 1].astype(t.dtype) * swiglu(t, w_gate[e], w_up[e], w_down[e])
    return y.reshape(B, S, D)


def setup_inputs(seed: int = 0) -> dict:
    key = jax.random.key(seed)
    ks = jax.random.split(key, 24)
    n_a = (DEPTH + 1) // 2
    n_b = DEPTH // 2
    f32 = jnp.float32

    def dense(k, shape, fan_in):
        return jax.random.normal(k, shape, f32) * fan_in ** -0.5

    def gain(k, shape, s=0.02):
        return 1.0 + s * jax.random.normal(k, shape, f32)

    return {
        "x": jax.random.normal(ks[0], (BATCH, SEQ, D_MODEL), f32),
        "mix_norm": gain(ks[1], (DEPTH, D_MODEL)),
        "ffn_norm": gain(ks[2], (DEPTH, D_MODEL)),
        "rel_bias": 0.5 * jax.random.normal(ks[3], (NUM_BUCKETS, N_HEADS), f32),
        "attn_w_qkv": dense(ks[4], (n_a, D_MODEL, QKV_DIM), D_MODEL),
        "attn_q_gain": gain(ks[5], (n_a, HEAD_DIM)),
        "attn_k_gain": gain(ks[6], (n_a, HEAD_DIM)),
        "attn_sink": jax.random.normal(ks[7], (n_a, N_HEADS), f32),
        "attn_w_o": dense(ks[8], (n_a, N_HEADS * HEAD_DIM, D_MODEL), N_HEADS * HEAD_DIM),
        "ffn_w_gate": dense(ks[9], (n_a, D_MODEL, D_FF), D_MODEL),
        "ffn_w_up": dense(ks[10], (n_a, D_MODEL, D_FF), D_MODEL),
        "ffn_w_down": dense(ks[11], (n_a, D_FF, D_MODEL), D_FF),
        "pool_w_in": dense(ks[12], (n_b, D_MODEL, D_MODEL), D_MODEL),
        "pool_w_group": dense(ks[13], (n_b, N_POOL_GROUPS, POOL_GROUP_DIM, POOL_GROUP_DIM), POOL_GROUP_DIM),
        "pool_scale": gain(ks[14], (n_b, D_MODEL), 0.1),
        "pool_w_out": dense(ks[15], (n_b, D_MODEL, D_MODEL), D_MODEL),
        "moe_router_w": dense(ks[16], (n_b, D_MODEL, N_EXPERTS), D_MODEL),
        "moe_router_b": 0.01 * jax.random.normal(ks[17], (n_b, N_EXPERTS), f32),
        "moe_w_gate": dense(ks[18], (n_b, N_EXPERTS, D_MODEL, D_FF_EXPERT), D_MODEL),
        "moe_w_up": dense(ks[19], (n_b, N_EXPERTS, D_MODEL, D_FF_EXPERT), D_MODEL),
        "moe_w_down": dense(ks[20], (n_b, N_EXPERTS, D_FF_EXPERT, D_MODEL), D_FF_EXPERT),
    }


def reference(x, mix_norm, ffn_norm, rel_bias, attn_w_qkv, attn_q_gain, attn_k_gain, attn_sink,
              attn_w_o, ffn_w_gate, ffn_w_up, ffn_w_down, pool_w_in, pool_w_group, pool_scale,
              pool_w_out, moe_router_w, moe_router_b, moe_w_gate, moe_w_up, moe_w_down):
    for i in range(DEPTH):
        j = i // 2
        h = rms_norm(x, mix_norm[i])
        if i % 2 == 0:
            x = x + window_attention(h, attn_w_qkv[j], attn_q_gain[j], attn_k_gain[j],
                                     attn_sink[j], attn_w_o[j], rel_bias)
            h = rms_norm(x, ffn_norm[i])
            x = x + swiglu(h, ffn_w_gate[j], ffn_w_up[j], ffn_w_down[j])
        else:
            x = x + multiscale_pool(h, pool_w_in[j], pool_w_group[j], pool_scale[j], pool_w_out[j])
            h = rms_norm(x, ffn_norm[i])
            x = x + moe_swiglu(h, moe_router_w[j], moe_router_b[j], moe_w_gate[j],
                               moe_w_up[j], moe_w_down[j])
    return x
```

```python
import functools
import math

import jax
import jax.numpy as jnp
from jax import lax
from jax.experimental import pallas as pl
from jax.experimental.pallas import tpu as pltpu

F32 = jnp.float32
BF16 = jnp.bfloat16
U32 = jnp.uint32
I32 = jnp.int32

EPS = 1e-6
NEG_INF = -1e30

LANES = 128
SUBLANES = 8
VMEM_LIMIT_BYTES = 56 * 1024 * 1024

HEAD_DIM = 128
GROUP = 4
WINDOW = 128
BLOCK = 128
KEY_SPAN = BLOCK + 2 * WINDOW
NUM_BUCKETS = 32
MAX_DISTANCE = 128
POOL_WINDOWS = (2, 4, 8, 16)
POOL_HALO = 8
TOP_K = 2


def _cparams(semantics):
    return pltpu.CompilerParams(dimension_semantics=semantics,
                                vmem_limit_bytes=VMEM_LIMIT_BYTES)


def _rms(x, gain):
    ms = jnp.mean(x * x, axis=-1, keepdims=True)
    return x * lax.rsqrt(ms + EPS) * gain


def _norm_matmul_kernel(x_ref, g_ref, w_ref, hg_ref, o_ref, h_scr, *, normed_heads):
    j = pl.program_id(1)
    tn = o_ref.shape[1]
    heads_per_tile = tn // HEAD_DIM

    @pl.when(j == 0)
    def _():
        h_scr[...] = _rms(x_ref[...], g_ref[...]).astype(BF16)

    acc = jnp.dot(h_scr[...], w_ref[...], preferred_element_type=F32)

    if normed_heads == 0:
        o_ref[...] = acc.astype(o_ref.dtype)
        return

    def store(first_raw_head):
        for c in range(heads_per_tile):
            sl = slice(c * HEAD_DIM, (c + 1) * HEAD_DIM)
            chunk = acc[:, sl]
            if c < first_raw_head:
                chunk = _rms(chunk, hg_ref[:, sl])
            o_ref[:, sl] = chunk.astype(o_ref.dtype)

    full_tiles, rem = divmod(normed_heads, heads_per_tile)

    @pl.when(j < full_tiles)
    def _():
        store(heads_per_tile)

    @pl.when(j == full_tiles)
    def _():
        store(rem)

    @pl.when(j > full_tiles)
    def _():
        store(0)


def _norm_matmul(x, gain, w, head_gain, *, normed_heads, out_dtype, tm, tn):
    t, d = x.shape
    n = w.shape[1]
    return pl.pallas_call(
        functools.partial(_norm_matmul_kernel, normed_heads=normed_heads),
        out_shape=jax.ShapeDtypeStruct((t, n), out_dtype),
        grid=(t // tm, n // tn),
        in_specs=[
            pl.BlockSpec((tm, d), lambda i, j: (i, 0)),
            pl.BlockSpec((1, d), lambda i, j: (0, 0)),
            pl.BlockSpec((d, tn), lambda i, j: (0, j)),
            pl.BlockSpec((1, tn), lambda i, j: (0, j)),
        ],
        out_specs=pl.BlockSpec((tm, tn), lambda i, j: (i, j)),
        scratch_shapes=[pltpu.VMEM((tm, d), BF16)],
        compiler_params=_cparams(("parallel", "arbitrary")),
        name="norm_matmul",
    )(x, gain.reshape(1, d), w, head_gain)


def _attn_kernel(sink_ref, q_ref, kv_ref, kvp_ref, kvn_ref, bias_ref, o_ref, *,
                 n_kv_heads, steps_per_seq):
    i = pl.program_id(0) % steps_per_seq
    tq = q_ref.shape[0]
    nqb = tq // BLOCK
    kw = n_kv_heads * HEAD_DIM
    prev_penalty = jnp.where(i > 0, 0.0, NEG_INF)
    next_penalty = jnp.where(i < steps_per_seq - 1, 0.0, NEG_INF)
    col =lax.broadcasted_iota(I32, (GROUP * BLOCK, KEY_SPAN), 1)

    for h in range(n_kv_heads):
        ksl = slice(h * HEAD_DIM, (h + 1) * HEAD_DIM)
        vsl = slice(kw + h * HEAD_DIM, kw + (h + 1) * HEAD_DIM)
        bias = bias_ref[h * GROUP:(h + 1) * GROUP].reshape(GROUP * BLOCK, KEY_SPAN)
        for qb in range(nqb):
            rows = slice(qb * BLOCK, (qb + 1) * BLOCK)
            prev_rows = slice((qb - 1) * BLOCK, qb * BLOCK)
            next_rows = slice((qb + 1) * BLOCK, (qb + 2) * BLOCK)
            if qb == 0:
                k_prev, v_prev = kvp_ref[:, ksl], kvp_ref[:, vsl]
            else:
                k_prev, v_prev = kv_ref[prev_rows, ksl], kv_ref[prev_rows, vsl]
            if qb == nqb - 1:
                k_next, v_next = kvn_ref[:, ksl], kvn_ref[:, vsl]
            else:
                k_next, v_next = kv_ref[next_rows, ksl], kv_ref[next_rows, vsl]
            k3 = jnp.concatenate([k_prev, kv_ref[rows, ksl], k_next], axis=0)
            v3 = jnp.concatenate([v_prev, kv_ref[rows, vsl], v_next], axis=0)
            q4 = jnp.concatenate(
                [q_ref[rows, (h * GROUP + g) * HEAD_DIM:(h * GROUP + g + 1) * HEAD_DIM]
                 for g in range(GROUP)], axis=0)
            s = lax.dot_general(q4, k3, (((1,), (1,)), ((), ())),
                                preferred_element_type=F32) + bias
            if qb == 0:
                s = s + jnp.where(col < WINDOW, prev_penalty, 0.0)
            if qb == nqb - 1:
                s = s + jnp.where(col >= WINDOW + BLOCK, next_penalty, 0.0)
            outs = []
            for g in range(GROUP):
                sg = s[g * BLOCK:(g + 1) * BLOCK]
                sink = sink_ref[h * GROUP + g]
                m = jnp.maximum(jnp.max(sg, axis=-1, keepdims=True), sink)
                p = jnp.exp(sg - m)
                denom = jnp.sum(p, axis=-1, keepdims=True) + jnp.exp(sink - m)
                pv = jnp.dot(p.astype(BF16), v3, preferred_element_type=F32)
                outs.append(pv / denom)
            for g in range(GROUP):
                o_ref[rows, (h * GROUP + g) * HEAD_DIM:(h * GROUP + g + 1) * HEAD_DIM] = (
                    outs[g].astype(o_ref.dtype))


def _attention(qkv, bias, sink, *, seq, n_heads, n_kv_heads, tq):
    t = qkv.shape[0]
    dq = n_heads * HEAD_DIM
    dkv = 2 * n_kv_heads * HEAD_DIM
    assert dq % dkv == 0
    kv_col = dq // dkv
    steps_per_seq = seq // tq
    blocks_per_step = tq // BLOCK
    last_block = t // BLOCK - 1
    return pl.pallas_call(
        functools.partial(_attn_kernel, n_kv_heads=n_kv_heads, steps_per_seq=steps_per_seq),
        out_shape=jax.ShapeDtypeStruct((t, dq), BF16),
        grid=(t // tq,),
        in_specs=[
            pl.BlockSpec(memory_space=pltpu.SMEM),
            pl.BlockSpec((tq, dq), lambda r: (r, 0)),
            pl.BlockSpec((tq, dkv), lambda r: (r, kv_col)),
            pl.BlockSpec((BLOCK, dkv),
                         lambda r: (jnp.maximum(r * blocks_per_step - 1, 0), kv_col)),
            pl.BlockSpec((BLOCK, dkv),
                         lambda r: (jnp.minimum((r + 1) * blocks_per_step, last_block), kv_col)),
            pl.BlockSpec((n_heads, BLOCK, KEY_SPAN), lambda r: (0, 0, 0)),
        ],
        out_specs=pl.BlockSpec((tq, dq), lambda r: (r, 0)),
        compiler_params=_cparams(("parallel",)),
        name="window_attention",
    )(sink, qkv, qkv, qkv, qkv, bias)


def _t5_bucket(rel):
    half = NUM_BUCKETS // 2
    ret = jnp.where(rel > 0, half, 0)
    n = jnp.abs(rel)
    max_exact = half // 2
    nf = jnp.maximum(n, 1).astype(F32)
    large = max_exact + (jnp.log(nf / max_exact) / math.log(MAX_DISTANCE / max_exact)
                         * (half - max_exact)).astype(I32)
    large = jnp.minimum(large, half - 1)
    return ret + jnp.where(n < max_exact, n, large)


def _band_bias(rel_bias):
    qi = jnp.arange(BLOCK)[:, None]
    kj = jnp.arange(KEY_SPAN)[None, :]
    rel = kj - WINDOW - qi
    bias = rel_bias.astype(F32)[_t5_bucket(rel)]
    bias = jnp.where((jnp.abs(rel) <= WINDOW)[:, :, None], bias, NEG_INF)
    return jnp.transpose(bias, (2, 0, 1))


def _proj_res_norm_kernel(a_ref, w_ref, x_ref, g_ref, xo_ref, ho_ref):
    xo = x_ref[...] + jnp.dot(a_ref[...], w_ref[...], preferred_element_type=F32)
    xo_ref[...] = xo
    ho_ref[...] = _rms(xo, g_ref[...]).astype(BF16)


def _proj_res_norm(a, w, x, gain, *, tm):
    t, d = x.shape
    k = a.shape[1]
    return pl.pallas_call(
        _proj_res_norm_kernel,
        out_shape=(jax.ShapeDtypeStruct((t, d), F32), jax.ShapeDtypeStruct((t, d), BF16)),
        grid=(t // tm,),
        in_specs=[
            pl.BlockSpec((tm, k), lambda i: (i, 0)),
            pl.BlockSpec((k, d), lambda i: (0, 0)),
            pl.BlockSpec((tm, d), lambda i: (i, 0)),
            pl.BlockSpec((1, d), lambda i: (0, 0)),
        ],
        out_specs=(pl.BlockSpec((tm, d), lambda i: (i, 0)),
                   pl.BlockSpec((tm, d), lambda i: (i, 0))),
        compiler_params=_cparams(("parallel",)),
        name="proj_res_norm",
    )(a, w, x, gain.reshape(1, d))


def _swiglu_act(h, wg, wu):
    g = jnp.dot(h, wg, preferred_element_type=F32)
    u = jnp.dot(h, wu, preferred_element_type=F32)
    return (g * jax.nn.sigmoid(g) * u).astype(BF16)


def _ffn_kernel(h_ref, x_ref, wg_ref, wu_ref, wd_ref, o_ref):
    f = pl.program_id(1)

    @pl.when(f == 0)
    def _():
        o_ref[...] = x_ref[...]

    a = _swiglu_act(h_ref[...], wg_ref[...], wu_ref[...])
    o_ref[...] += jnp.dot(a, wd_ref[...], preferred_element_type=F32)


def _ffn(h, x, wg, wu, wd, *, tm, tf):
    t, d = x.shape
    dff = wg.shape[1]
    return pl.pallas_call(
        _ffn_kernel,
        out_shape=jax.ShapeDtypeStruct((t, d), F32),
        grid=(t // tm, dff // tf),
        in_specs=[
            pl.BlockSpec((tm, d), lambda i, f: (i, 0)),
            pl.BlockSpec((tm, d), lambda i, f: (i, 0)),
            pl.BlockSpec((d, tf), lambda i, f: (0, f)),
            pl.BlockSpec((d, tf), lambda i, f: (0, f)),
            pl.BlockSpec((tf, d), lambda i, f: (f, 0)),
        ],
        out_specs=pl.BlockSpec((tm, d), lambda i, f: (i, 0)),
        compiler_params=_cparams(("parallel", "arbitrary")),
        name="dense_swiglu",
    )(h, x, wg, wu, wd)


def _pack_bf16_pairs(lo, hi):
    lo_bits = lax.bitcast_convert_type(lo.astype(BF16).astype(F32), U32)
    hi_bits = lax.bitcast_convert_type(hi.astype(BF16).astype(F32), U32)
    return (lo_bits >> 16) | hi_bits


def _unpack_bf16_pairs(words):
    lo = lax.bitcast_convert_type(words << 16, F32).astype(BF16)
    hi = lax.bitcast_convert_type(words & jnp.uint32(0xFFFF0000), F32).astype(BF16)
    return lo, hi


def _pool_tail_kernel(u_ref, up_ref, un_ref, wgrp_ref, scale_ref, wout_ref, x_ref, g_ref,
                      rw_ref, rb_ref, xo_ref, hp_ref, route_ref, *, seq, n_experts):
    tm, d = x_ref.shape
    gd = d // len(POOL_WINDOWS)
    ext = tm + 2 * POOL_HALO
    start = (pl.program_id(0) * tm) % seq
    pos = start + lax.broadcasted_iota(I32, (tm, 1), 0)

    ys = []
    for gi, win in enumerate(POOL_WINDOWS):
        cs = slice(gi * gd, (gi + 1) * gd)
        u = u_ref[:, cs]
        up = jnp.where(start > 0, up_ref[:, cs], 0.0)
        un = jnp.where(start + tm < seq, un_ref[:, cs], 0.0)
        a = jnp.concatenate([up, u, un], axis=0)
        a = a + pltpu.roll(a, 1, axis=0)
        w = 2
        while w < win:
            a = pltpu.roll(a, w // 2, axis=0) + pltpu.roll(a, ext - w // 2, axis=0)
            w *= 2
        half = win // 2
        count = (jnp.minimum(pos + half, seq) - jnp.maximum(pos - half, 0)).astype(F32)
        diff = a[POOL_HALO:POOL_HALO + tm] / count - u
        y = jnp.dot(diff.astype(BF16), wgrp_ref[gi], preferred_element_type=F32)
        ys.append((y * scale_ref[:, cs]).astype(BF16))
    y = jnp.concatenate(ys, axis=1)
    xo = x_ref[...] + jnp.dot(y, wout_ref[...], preferred_element_type=F32)
    xo_ref[...] = xo

    h = _rms(xo, g_ref[...])
    half_d = d // 2
    words = _pack_bf16_pairs(h[:, :half_d], h[:, half_d:])
    n_chunks = half_d // LANES
    for c in range(n_chunks):
        hp_ref[pl.ds(c, tm, stride=n_chunks), :] = words[:, c * LANES:(c + 1) * LANES]

    logits = jnp.dot(h, rw_ref[...], preferred_element_type=F32,
                     precision=lax.Precision.HIGHEST) + rb_ref[...]
    lane = lax.broadcasted_iota(I32, logits.shape, 1)
    logits = jnp.where(lane < n_experts, logits, -jnp.inf)
    lane_f = lane.astype(F32)
    m1 = jnp.max(logits, axis=-1, keepdims=True)
    i1 = jnp.min(jnp.where(logits == m1, lane_f, float(LANES)), axis=-1, keepdims=True)
    rest = jnp.where(lane_f == i1, -jnp.inf, logits)
    m2 = jnp.max(rest, axis=-1, keepdims=True)
    i2 = jnp.min(jnp.where(rest == m2, lane_f, float(LANES)), axis=-1, keepdims=True)
    e2 = jnp.exp(m2 - m1)
    w1 = 1.0 / (1.0 + e2)
    w2 = e2 / (1.0 + e2)
    route = jnp.where(lane == 0, i1,
                      jnp.where(lane == 1, i2,
                                jnp.where(lane == 2, w1, jnp.where(lane == 3, w2, 0.0))))
    route_ref[...] = route


def _pool_tail(u, wgrp, scale, wout, x, gain, rw, rb, *, seq, n_experts, tm):
    t, d = x.shape
    ng, gd, _ = wgrp.shape
    n_chunks = d // 2 // LANES
    halo_blocks = tm // POOL_HALO
    last_halo = t // POOL_HALO - 1
    return pl.pallas_call(
        functools.partial(_pool_tail_kernel, seq=seq, n_experts=n_experts),
        out_shape=(jax.ShapeDtypeStruct((t, d), F32),
                   jax.ShapeDtypeStruct((t * n_chunks, LANES), U32),
                   jax.ShapeDtypeStruct((t, LANES), F32)),
        grid=(t // tm,),
        in_specs=[
            pl.BlockSpec((tm, d), lambda i: (i, 0)),
            pl.BlockSpec((POOL_HALO, d), lambda i: (jnp.maximum(i * halo_blocks - 1, 0), 0)),
            pl.BlockSpec((POOL_HALO, d),
                         lambda i: (jnp.minimum((i + 1) * halo_blocks, last_halo), 0)),
            pl.BlockSpec((ng, gd, gd), lambda i: (0, 0, 0)),
            pl.BlockSpec((1, d), lambda i: (0, 0)),
            pl.BlockSpec((d, d), lambda i: (0, 0)),
            pl.BlockSpec((tm, d), lambda i: (i, 0)),
            pl.BlockSpec((1, d), lambda i: (0, 0)),
            pl.BlockSpec((d, LANES), lambda i: (0, 0)),
            pl.BlockSpec((1, LANES), lambda i: (0, 0)),
        ],
        out_specs=(pl.BlockSpec((tm, d), lambda i: (i, 0)),
                   pl.BlockSpec((tm * n_chunks, LANES), lambda i: (i, 0)),
                   pl.BlockSpec((tm, LANES), lambda i: (i, 0))),
        compiler_params=_cparams(("parallel",)),
        name="pool_tail",
    )(u, u, u, wgrp, scale.reshape(1, d), wout, x, gain.reshape(1, d), rw, rb)


def _routing_plan(e_flat, *, n_experts, tm, n_tiles):
    onehot = (e_flat[:, None] == jnp.arange(n_experts, dtype=I32)[None, :]).astype(I32)
    csum = jnp.cumsum(onehot, axis=0)
    rank = jnp.sum(onehot * csum, axis=1) - 1
    counts = csum[-1]
    tiles_per = (counts + tm - 1) // tm
    padded = tiles_per * tm
    offsets = jnp.cumsum(padded) - padded
    pos = jnp.sum(onehot * offsets[None, :], axis=1) + rank
    tile_end = jnp.cumsum(tiles_per)
    n_used = tile_end[-1]
    tile_ids = jnp.arange(n_tiles, dtype=I32)
    tile_expert = jnp.sum((tile_ids[:, None] >= tile_end[None, :]).astype(I32), axis=1)
    last_expert = jnp.sum((n_used - 1 >= tile_end).astype(I32))
    tile_expert = jnp.minimum(tile_expert, last_expert).astype(I32)
    return pos.astype(I32), counts.astype(I32), offsets.astype(I32), tile_expert, n_used.astype(I32)


def _dispatch_kernel(pos_ref, cnt_ref, off_ref, src_ref, dst_ref, sem, *, tc, rpt, tm,
                     n_experts):
    i = pl.program_id(0)
    base = i * tc

    def row_copy(src_row, dst_row):
        return pltpu.make_async_copy(
            src_ref.at[pl.ds(pl.multiple_of(src_row * rpt, rpt), rpt)],
            dst_ref.at[pl.ds(pl.multiple_of(dst_row * rpt, rpt), rpt)], sem)

    def issue(j, carry):
        for k in range(TOP_K):
            row_copy(base + j, pos_ref[0, 0, TOP_K * j + k]).start()
        return carry

    lax.fori_loop(0, tc, issue, 0)

    def drain(j, carry):
        for k in range(TOP_K):
            row_copy(0, 0).wait()
        return carry

    lax.fori_loop(0, tc, drain, 0)

    @pl.when(i == pl.num_programs(0) - 1)
    def _():
        for e in range(n_experts):
            cnt = cnt_ref[e]
            n_pad = (-cnt) % tm
            first = off_ref[e] + cnt

            def fill(j, carry):
                row_copy(0, first + j).start()
                return carry

            def fill_wait(j, carry):
                row_copy(0, 0).wait()
                return carry

            lax.fori_loop(0, n_pad, fill, 0)
            lax.fori_loop(0, n_pad, fill_wait, 0)

        n_used = sum((cnt_ref[e] + tm - 1) // tm for e in range(n_experts))
        tile_rows = tm * rpt

        def fill_tile(j, carry):
            cp = pltpu.make_async_copy(
                dst_ref.at[pl.ds(0, tile_rows)],
                dst_ref.at[pl.ds(pl.multiple_of(j * tile_rows, tile_rows), tile_rows)], sem)
            cp.start()
            cp.wait()
            return carry

        lax.fori_loop(n_used, dst_ref.shape[0] // tile_rows, fill_tile, 0)


def _dispatch(pos, counts, offsets, packed, *, rows_per_token, n_rows, tc, tm):
    t = packed.shape[0] // rows_per_token
    n_steps = t // tc
    return pl.pallas_call(
        functools.partial(_dispatch_kernel, tc=tc, rpt=rows_per_token, tm=tm,
                          n_experts=counts.shape[0]),
        out_shape=jax.ShapeDtypeStruct((n_rows * rows_per_token, LANES), packed.dtype),
        grid=(n_steps,),
        in_specs=[
            pl.BlockSpec((1, 1, TOP_K * tc), lambda i: (i, 0, 0), memory_space=pltpu.SMEM),
            pl.BlockSpec(memory_space=pltpu.SMEM),
            pl.BlockSpec(memory_space=pltpu.SMEM),
            pl.BlockSpec(memory_space=pl.ANY),
        ],
        out_specs=pl.BlockSpec(memory_space=pl.ANY),
        scratch_shapes=[pltpu.SemaphoreType.DMA(())],
        compiler_params=_cparams(("arbitrary",)),
        name="moe_dispatch",
    )(pos.reshape(n_steps, 1, TOP_K * tc), counts, offsets, packed)


def _moe_kernel(te_ref, nu_ref, xs_ref, wg_ref, wu_ref, wd_ref, o_ref, x_scr, acc_scr, *,
                in_chunks, out_chunks):
    del te_ref
    i = pl.program_id(0)
    f = pl.program_id(1)
    tm, d = x_scr.shape
    used = i < nu_ref[0]

    @pl.when(used & (f == 0))
    def _():
        for c in range(in_chunks):
            lo, hi = _unpack_bf16_pairs(xs_ref[pl.ds(c, tm, stride=in_chunks), :])
            x_scr[:, c * LANES:(c + 1) * LANES] = lo
            x_scr[:, d // 2 + c * LANES:d // 2 + (c + 1) * LANES] = hi

    @pl.when(used)
    def _():
        a = _swiglu_act(x_scr[...], wg_ref[...], wu_ref[...])
        y = jnp.dot(a, wd_ref[...], preferred_element_type=F32)

        @pl.when(f == 0)
        def _():
            acc_scr[...] = y

        @pl.when(f > 0)
        def _():
            acc_scr[...] += y

    @pl.when(f == pl.num_programs(1) - 1)
    def _():
        for c in range(out_chunks):
            val = jnp.where(used, acc_scr[:, c * LANES:(c + 1) * LANES], 0.0)
            o_ref[pl.ds(c, tm, stride=out_chunks), :] = val


def _moe(tile_expert, n_used, xs, wg, wu, wd, *, tm, tf, n_tiles):
    n_experts, d, dff = wg.shape
    in_chunks = d // 2 // LANES
    out_chunks = d // LANES
    nf = dff // tf

    def row_map(i, f, te, nu):
        return (jnp.minimum(i, nu[0] - 1), 0)

    def f_of(i, f, nu):
        return jnp.where(i < nu[0], f, nf - 1)

    grid_spec = pltpu.PrefetchScalarGridSpec(
        num_scalar_prefetch=2,
        grid=(n_tiles, nf),
        in_specs=[
            pl.BlockSpec((tm * in_chunks, LANES), row_map),
            pl.BlockSpec((None, d, tf), lambda i, f, te, nu: (te[i], 0, f_of(i, f, nu))),
            pl.BlockSpec((None, d, tf), lambda i, f, te, nu: (te[i], 0, f_of(i, f, nu))),
            pl.BlockSpec((None, tf, d), lambda i, f, te, nu: (te[i], f_of(i, f, nu), 0)),
        ],
        out_specs=pl.BlockSpec((tm * out_chunks, LANES), lambda i, f, te, nu: (i, 0)),
        scratch_shapes=[pltpu.VMEM((tm, d), BF16), pltpu.VMEM((tm, d), F32)],
    )
    return pl.pallas_call(
        functools.partial(_moe_kernel, in_chunks=in_chunks, out_chunks=out_chunks),
        out_shape=jax.ShapeDtypeStruct((n_tiles * tm * out_chunks, LANES), F32),
        grid_spec=grid_spec,
        compiler_params=_cparams(("arbitrary", "arbitrary")),
        name="moe_grouped_swiglu",
    )(tile_expert, n_used.reshape(1), xs, wg, wu, wd)


def _combine_kernel(pos_ref, ys_ref, x_ref, route_ref, o_ref, buf_a, buf_b, sem, *, rpt):
    tc = x_ref.shape[0]

    def row_copy(src_row, dst_row, buf):
        return pltpu.make_async_copy(
            ys_ref.at[pl.ds(pl.multiple_of(src_row * rpt, rpt), rpt)],
            buf.at[pl.ds(pl.multiple_of(dst_row * rpt, rpt), rpt)], sem)

    def issue(j, carry):
        row_copy(pos_ref[0, 0, TOP_K * j], j, buf_a).start()
        row_copy(pos_ref[0, 0, TOP_K * j + 1], j, buf_b).start()
        return carry

    lax.fori_loop(0, tc, issue, 0)

    def drain(j, carry):
        row_copy(0, 0, buf_a).wait()
        row_copy(0, 0, buf_b).wait()
        return carry

    lax.fori_loop(0, tc, drain, 0)

    w1 = route_ref[:, 2:3]
    w2 = route_ref[:, 3:4]
    for c in range(rpt):
        sl = slice(c * LANES, (c + 1) * LANES)
        o_ref[:, sl] = (x_ref[:, sl] + w1 * buf_a[pl.ds(c, tc, stride=rpt), :]
                        + w2 * buf_b[pl.ds(c, tc, stride=rpt), :])


def _combine(pos, ys, x, route, *, tc):
    t, d = x.shape
    rpt = d // LANES
    n_steps = t // tc
    return pl.pallas_call(
        functools.partial(_combine_kernel, rpt=rpt),
        out_shape=jax.ShapeDtypeStruct((t, d), F32),
        grid=(n_steps,),
        in_specs=[
            pl.BlockSpec((1, 1, TOP_K * tc), lambda i: (i, 0, 0), memory_space=pltpu.SMEM),
            pl.BlockSpec(memory_space=pl.ANY),
            pl.BlockSpec((tc, d), lambda i: (i, 0)),
            pl.BlockSpec((tc, LANES), lambda i: (i, 0)),
        ],
        out_specs=pl.BlockSpec((tc, d), lambda i: (i, 0)),
        scratch_shapes=[pltpu.VMEM((tc * rpt, LANES), F32), pltpu.VMEM((tc * rpt, LANES), F32),
                        pltpu.SemaphoreType.DMA(())],
        compiler_params=_cparams(("arbitrary",)),
        name="moe_combine",
    )(pos.reshape(n_steps, 1, TOP_K * tc), ys, x, route)


def _tile(n, want):
    want = min(want, n)
    while n % want:
        want //= 2
    return want


def _attention_layer(x, seq, mix_gain, ffn_gain, rel_bias, w_qkv, q_gain, k_gain, sink, w_o,
                     w_gate, w_up, w_down):
    t, d = x.shape
    n_heads = sink.shape[0]
    n_kv_heads = (w_qkv.shape[1] // HEAD_DIM - n_heads) // 2
    head_gain = jnp.concatenate([
        jnp.tile(q_gain * HEAD_DIM ** -0.5, n_heads),
        jnp.tile(k_gain, n_kv_heads),
        jnp.ones((n_kv_heads * HEAD_DIM,), F32)]).reshape(1, -1)
    qkv = _norm_matmul(x, mix_gain, w_qkv.astype(BF16), head_gain,
                       normed_heads=n_heads + n_kv_heads, out_dtype=BF16,
                       tm=_tile(t, 1024), tn=_tile(w_qkv.shape[1], 1024))
    attn = _attention(qkv, _band_bias(rel_bias), sink, seq=seq, n_heads=n_heads,
                      n_kv_heads=n_kv_heads, tq=_tile(seq, 512))
    x, h = _proj_res_norm(attn, w_o.astype(BF16), x, ffn_gain, tm=_tile(t, 512))
    return _ffn(h, x, w_gate.astype(BF16), w_up.astype(BF16), w_down.astype(BF16),
                tm=_tile(t, 512), tf=_tile(w_gate.shape[1], 512))


def _pool_moe_layer(x, seq, mix_gain, ffn_gain, w_in, w_group, scale, w_out, router_w, router_b,
                    w_gate, w_up, w_down):
    t, d = x.shape
    n_experts = router_w.shape[1]
    u = _norm_matmul(x, mix_gain, w_in.astype(BF16), jnp.ones((1, d), F32), normed_heads=0,
                     out_dtype=F32, tm=_tile(t, 1024), tn=_tile(d, 1024))
    rw = jnp.zeros((d, LANES), F32).at[:, :n_experts].set(router_w)
    rb = jnp.zeros((1, LANES), F32).at[0, :n_experts].set(router_b)
    x, packed, route = _pool_tail(u, w_group.astype(BF16), scale, w_out.astype(BF16), x, ffn_gain,
                                  rw, rb, seq=seq, n_experts=n_experts, tm=_tile(seq, 256))
    tm = _tile(t, 1024)
    n_tiles = TOP_K * t // tm + n_experts
    e_flat = route[:, :TOP_K].astype(I32).reshape(-1)
    pos, counts, offsets, tile_expert, n_used = _routing_plan(
        e_flat, n_experts=n_experts, tm=tm, n_tiles=n_tiles)
    xs = _dispatch(pos, counts, offsets, packed, rows_per_token=d // 2 // LANES,
                   n_rows=n_tiles * tm, tc=_tile(t, 512), tm=tm)
    ys = _moe(tile_expert, n_used, xs, w_gate.astype(BF16), w_up.astype(BF16),
              w_down.astype(BF16), tm=tm, tf=_tile(w_gate.shape[2], 512), n_tiles=n_tiles)
    return _combine(pos, ys, x, route, tc=_tile(t, 256))


def kernel(x, mix_norm, ffn_norm, rel_bias, attn_w_qkv, attn_q_gain, attn_k_gain, attn_sink,
           attn_w_o, ffn_w_gate, ffn_w_up, ffn_w_down, pool_w_in, pool_w_group, pool_scale,
           pool_w_out, moe_router_w, moe_router_b, moe_w_gate, moe_w_up, moe_w_down):
    b, s, d = x.shape
    y = x.reshape(b * s, d)
    for i in range(mix_norm.shape[0]):
        j = i // 2
        if i % 2 == 0:
            y = _attention_layer(y, s, mix_norm[i], ffn_norm[i], rel_bias, attn_w_qkv[j],
                                 attn_q_gain[j], attn_k_gain[j], attn_sink[j], attn_w_o[j],
                                 ffn_w_gate[j], ffn_w_up[j], ffn_w_down[j])
        else:
            y = _pool_moe_layer(y, s, mix_norm[i], ffn_norm[i], pool_w_in[j], pool_w_group[j],
                                pool_scale[j], pool_w_out[j], moe_router_w[j], moe_router_b[j],
                                moe_w_gate[j], moe_w_up[j], moe_w_down[j])
    return y.reshape(b, s, d)
```

```python
import functools
import math

import jax
import jax.numpy as jnp
from jax import lax
from jax.experimental import pallas as pl
from jax.experimental.pallas import tpu as pltpu

F32 = jnp.float32
BF16 = jnp.bfloat16
U32 = jnp.uint32
I32 = jnp.int32

EPS = 1e-6
NEG_INF = -1e30

LANES = 128
SUBLANES = 8
VMEM_LIMIT_BYTES = 56 * 1024 * 1024

HEAD_DIM = 128
GROUP = 4
WINDOW = 128
BLOCK = 128
KEY_SPAN = BLOCK + 2 * WINDOW
NUM_BUCKETS = 32
MAX_DISTANCE = 128
POOL_WINDOWS = (2, 4, 8, 16)
POOL_HALO = 8
TOP_K = 2


def _cparams(semantics):
    return pltpu.CompilerParams(dimension_semantics=semantics,
                                vmem_limit_bytes=VMEM_LIMIT_BYTES)


def _rms(x, gain):
    ms = jnp.mean(x * x, axis=-1, keepdims=True)
    return x * lax.rsqrt(ms + EPS) * gain


def _norm_matmul_kernel(x_ref, g_ref, w_ref, hg_ref, o_ref, h_scr, *, normed_heads):
    j = pl.program_id(1)
    tn = o_ref.shape[1]
    heads_per_tile = tn // HEAD_DIM

    @pl.when(j == 0)
    def _():
        h_scr[...] = _rms(x_ref[...], g_ref[...]).astype(BF16)

    acc = jnp.dot(h_scr[...], w_ref[...], preferred_element_type=F32)

    if normed_heads == 0:
        o_ref[...] = acc.astype(o_ref.dtype)
        return

    def store(first_raw_head):
        for c in range(heads_per_tile):
            sl = slice(c * HEAD_DIM, (c + 1) * HEAD_DIM)
            chunk = acc[:, sl]
            if c < first_raw_head:
                chunk = _rms(chunk, hg_ref[:, sl])
            o_ref[:, sl] = chunk.astype(o_ref.dtype)

    full_tiles, rem = divmod(normed_heads, heads_per_tile)

    @pl.when(j < full_tiles)
    def _():
        store(heads_per_tile)

    @pl.when(j == full_tiles)
    def _():
        store(rem)

    @pl.when(j > full_tiles)
    def _():
        store(0)


def _norm_matmul(x, gain, w, head_gain, *, normed_heads, out_dtype, tm, tn):
    t, d = x.shape
    n = w.shape[1]
    return pl.pallas_call(
        functools.partial(_norm_matmul_kernel, normed_heads=normed_heads),
        out_shape=jax.ShapeDtypeStruct((t, n), out_dtype),
        grid=(t // tm, n // tn),
        in_specs=[
            pl.BlockSpec((tm, d), lambda i, j: (i, 0)),
            pl.BlockSpec((1, d), lambda i, j: (0, 0)),
            pl.BlockSpec((d, tn), lambda i, j: (0, j)),
            pl.BlockSpec((1, tn), lambda i, j: (0, j)),
        ],
        out_specs=pl.BlockSpec((tm, tn), lambda i, j: (i, j)),
        scratch_shapes=[pltpu.VMEM((tm, d), BF16)],
        compiler_params=_cparams(("parallel", "arbitrary")),
        name="norm_matmul",
    )(x, gain.reshape(1, d), w, head_gain)


def _attn_kernel(sink_ref, q_ref, kv_ref, kvp_ref, kvn_ref, bias_ref, o_ref, *,
                 n_kv_heads, steps_per_seq):
    i = pl.program_id(0) % steps_per_seq
    tq = q_ref.shape[0]
    nqb = tq // BLOCK
    kw = n_kv_heads * HEAD_DIM
    prev_penalty = jnp.where(i > 0, 0.0, NEG_INF)
    next_penalty = jnp.where(i < steps_per_seq - 1, 0.0, NEG_INF)
    col =lax.broadcasted_iota(I32, (GROUP * BLOCK, KEY_SPAN), 1)

    for h in range(n_kv_heads):
        ksl = slice(h * HEAD_DIM, (h + 1) * HEAD_DIM)
        vsl = slice(kw + h * HEAD_DIM, kw + (h + 1) * HEAD_DIM)
        bias = bias_ref[h * GROUP:(h + 1) * GROUP].reshape(GROUP * BLOCK, KEY_SPAN)
        for qb in range(nqb):
            rows = slice(qb * BLOCK, (qb + 1) * BLOCK)
            prev_rows = slice((qb - 1) * BLOCK, qb * BLOCK)
            next_rows = slice((qb + 1) * BLOCK, (qb + 2) * BLOCK)
            if qb == 0:
                k_prev, v_prev = kvp_ref[:, ksl], kvp_ref[:, vsl]
            else:
                k_prev, v_prev = kv_ref[prev_rows, ksl], kv_ref[prev_rows, vsl]
            if qb == nqb - 1:
                k_next, v_next = kvn_ref[:, ksl], kvn_ref[:, vsl]
            else:
                k_next, v_next = kv_ref[next_rows, ksl], kv_ref[next_rows, vsl]
            k3 = jnp.concatenate([k_prev, kv_ref[rows, ksl], k_next], axis=0)
            v3 = jnp.concatenate([v_prev, kv_ref[rows, vsl], v_next], axis=0)
            q4 = jnp.concatenate(
                [q_ref[rows, (h * GROUP + g) * HEAD_DIM:(h * GROUP + g + 1) * HEAD_DIM]
                 for g in range(GROUP)], axis=0)
            s = lax.dot_general(q4, k3, (((1,), (1,)), ((), ())),
                                preferred_element_type=F32) + bias
            if qb == 0:
                s = s + jnp.where(col < WINDOW, prev_penalty, 0.0)
            if qb == nqb - 1:
                s = s + jnp.where(col >= WINDOW + BLOCK, next_penalty, 0.0)
            outs = []
            for g in range(GROUP):
                sg = s[g * BLOCK:(g + 1) * BLOCK]
                sink = sink_ref[h * GROUP + g]
                m = jnp.maximum(jnp.max(sg, axis=-1, keepdims=True), sink)
                p = jnp.exp(sg - m)
                denom = jnp.sum(p, axis=-1, keepdims=True) + jnp.exp(sink - m)
                pv = jnp.dot(p.astype(BF16), v3, preferred_element_type=F32)
                outs.append(pv / denom)
            for g in range(GROUP):
                o_ref[rows, (h * GROUP + g) * HEAD_DIM:(h * GROUP + g + 1) * HEAD_DIM] = (
                    outs[g].astype(o_ref.dtype))


def _attention(qkv, bias, sink, *, seq, n_heads, n_kv_heads, tq):
    t = qkv.shape[0]
    dq = n_heads * HEAD_DIM
    dkv = 2 * n_kv_heads * HEAD_DIM
    assert dq % dkv == 0
    kv_col = dq // dkv
    steps_per_seq = seq // tq
    blocks_per_step = tq // BLOCK
    last_block = t // BLOCK - 1
    return pl.pallas_call(
        functools.partial(_attn_kernel, n_kv_heads=n_kv_heads, steps_per_seq=steps_per_seq),
        out_shape=jax.ShapeDtypeStruct((t, dq), BF16),
        grid=(t // tq,),
        in_specs=[
            pl.BlockSpec(memory_space=pltpu.SMEM),
            pl.BlockSpec((tq, dq), lambda r: (r, 0)),
            pl.BlockSpec((tq, dkv), lambda r: (r, kv_col)),
            pl.BlockSpec((BLOCK, dkv),
                         lambda r: (jnp.maximum(r * blocks_per_step - 1, 0), kv_col)),
            pl.BlockSpec((BLOCK, dkv),
                         lambda r: (jnp.minimum((r + 1) * blocks_per_step, last_block), kv_col)),
            pl.BlockSpec((n_heads, BLOCK, KEY_SPAN), lambda r: (0, 0, 0)),
        ],
        out_specs=pl.BlockSpec((tq, dq), lambda r: (r, 0)),
        compiler_params=_cparams(("parallel",)),
        name="window_attention",
    )(sink, qkv, qkv, qkv, qkv, bias)


def _t5_bucket(rel):
    half = NUM_BUCKETS // 2
    ret = jnp.where(rel > 0, half, 0)
    n = jnp.abs(rel)
    max_exact = half // 2
    nf = jnp.maximum(n, 1).astype(F32)
    large = max_exact + (jnp.log(nf / max_exact) / math.log(MAX_DISTANCE / max_exact)
                         * (half - max_exact)).astype(I32)
    large = jnp.minimum(large, half - 1)
    return ret + jnp.where(n < max_exact, n, large)


def _band_bias(rel_bias):
    qi = jnp.arange(BLOCK)[:, None]
    kj = jnp.arange(KEY_SPAN)[None, :]
    rel = kj - WINDOW - qi
    onehot = (_t5_bucket(rel)[:, :, None] == jnp.arange(NUM_BUCKETS)[None, None, :]).astype(F32)
    bias = jnp.einsum("qkb,bh->hqk", onehot, rel_bias.astype(F32),
                      precision=lax.Precision.HIGHEST)
    return jnp.where((jnp.abs(rel) <= WINDOW)[None], bias, NEG_INF)


def _proj_res_norm_kernel(a_ref, w_ref, x_ref, g_ref, xo_ref, ho_ref):
    xo = x_ref[...] + jnp.dot(a_ref[...], w_ref[...], preferred_element_type=F32)
    xo_ref[...] = xo
    ho_ref[...] = _rms(xo, g_ref[...]).astype(BF16)


def _proj_res_norm(a, w, x, gain, *, tm):
    t, d = x.shape
    k = a.shape[1]
    return pl.pallas_call(
        _proj_res_norm_kernel,
        out_shape=(jax.ShapeDtypeStruct((t, d), F32), jax.ShapeDtypeStruct((t, d), BF16)),
        grid=(t // tm,),
        in_specs=[
            pl.BlockSpec((tm, k), lambda i: (i, 0)),
            pl.BlockSpec((k, d), lambda i: (0, 0)),
            pl.BlockSpec((tm, d), lambda i: (i, 0)),
            pl.BlockSpec((1, d), lambda i: (0, 0)),
        ],
        out_specs=(pl.BlockSpec((tm, d), lambda i: (i, 0)),
                   pl.BlockSpec((tm, d), lambda i: (i, 0))),
        compiler_params=_cparams(("parallel",)),
        name="proj_res_norm",
    )(a, w, x, gain.reshape(1, d))


def _swiglu_act(h, wg, wu):
    g = jnp.dot(h, wg, preferred_element_type=F32)
    u = jnp.dot(h, wu, preferred_element_type=F32)
    return (g * jax.nn.sigmoid(g) * u).astype(BF16)


def _ffn_kernel(h_ref, x_ref, wg_ref, wu_ref, wd_ref, o_ref):
    f = pl.program_id(1)

    @pl.when(f == 0)
    def _():
        o_ref[...] = x_ref[...]

    a = _swiglu_act(h_ref[...], wg_ref[...], wu_ref[...])
    o_ref[...] += jnp.dot(a, wd_ref[...], preferred_element_type=F32)


def _ffn(h, x, wg, wu, wd, *, tm, tf):
    t, d = x.shape
    dff = wg.shape[1]
    return pl.pallas_call(
        _ffn_kernel,
        out_shape=jax.ShapeDtypeStruct((t, d), F32),
        grid=(t // tm, dff // tf),
        in_specs=[
            pl.BlockSpec((tm, d), lambda i, f: (i, 0)),
            pl.BlockSpec((tm, d), lambda i, f: (i, 0)),
            pl.BlockSpec((d, tf), lambda i, f: (0, f)),
            pl.BlockSpec((d, tf), lambda i, f: (0, f)),
            pl.BlockSpec((tf, d), lambda i, f: (f, 0)),
        ],
        out_specs=pl.BlockSpec((tm, d), lambda i, f: (i, 0)),
        compiler_params=_cparams(("parallel", "arbitrary")),
        name="dense_swiglu",
    )(h, x, wg, wu, wd)


def _pack_bf16_pairs(lo, hi):
    lo_bits = lax.bitcast_convert_type(lo.astype(BF16).astype(F32), U32)
    hi_bits = lax.bitcast_convert_type(hi.astype(BF16).astype(F32), U32)
    return (lo_bits >> 16) | hi_bits


def _unpack_bf16_pairs(words):
    lo = lax.bitcast_convert_type(words << 16, F32).astype(BF16)
    hi = lax.bitcast_convert_type(words & jnp.uint32(0xFFFF0000), F32).astype(BF16)
    return lo, hi


def _pool_tail_kernel(u_ref, up_ref, un_ref, wgrp_ref, scale_ref, wout_ref, x_ref, g_ref,
                      rw_ref, rb_ref, xo_ref, hp_ref, route_ref, *, seq, n_experts):
    tm, d = x_ref.shape
    gd = d // len(POOL_WINDOWS)
    ext = tm + 2 * POOL_HALO
    start = (pl.program_id(0) * tm) % seq
    pos = start + lax.broadcasted_iota(I32, (tm, 1), 0)

    ys = []
    for gi, win in enumerate(POOL_WINDOWS):
        cs = slice(gi * gd, (gi + 1) * gd)
        u = u_ref[:, cs]
        up = jnp.where(start > 0, up_ref[:, cs], 0.0)
        un = jnp.where(start + tm < seq, un_ref[:, cs], 0.0)
        a = jnp.concatenate([up, u, un], axis=0)
        a = a + pltpu.roll(a, 1, axis=0)
        w = 2
        while w < win:
            a = pltpu.roll(a, w // 2, axis=0) + pltpu.roll(a, ext - w // 2, axis=0)
            w *= 2
        half = win // 2
        count = (jnp.minimum(pos + half, seq) - jnp.maximum(pos - half, 0)).astype(F32)
        diff = a[POOL_HALO:POOL_HALO + tm] / count - u
        y = jnp.dot(diff.astype(BF16), wgrp_ref[gi], preferred_element_type=F32)
        ys.append((y * scale_ref[:, cs]).astype(BF16))
    y = jnp.concatenate(ys, axis=1)
    xo = x_ref[...] + jnp.dot(y, wout_ref[...], preferred_element_type=F32)
    xo_ref[...] = xo

    h = _rms(xo, g_ref[...])
    half_d = d // 2
    words = _pack_bf16_pairs(h[:, :half_d], h[:, half_d:])
    n_chunks = half_d // LANES
    for c in range(n_chunks):
        hp_ref[pl.ds(c, tm, stride=n_chunks), :] = words[:, c * LANES:(c + 1) * LANES]

    logits = jnp.dot(h, rw_ref[...], preferred_element_type=F32,
                     precision=lax.Precision.HIGHEST) + rb_ref[...]
    lane = lax.broadcasted_iota(I32, logits.shape, 1)
    logits = jnp.where(lane < n_experts, logits, -jnp.inf)
    lane_f = lane.astype(F32)
    m1 = jnp.max(logits, axis=-1, keepdims=True)
    i1 = jnp.min(jnp.where(logits == m1, lane_f, float(LANES)), axis=-1, keepdims=True)
    rest = jnp.where(lane_f == i1, -jnp.inf, logits)
    m2 = jnp.max(rest, axis=-1, keepdims=True)
    i2 = jnp.min(jnp.where(rest == m2, lane_f, float(LANES)), axis=-1, keepdims=True)
    e2 = jnp.exp(m2 - m1)
    w1 = 1.0 / (1.0 + e2)
    w2 = e2 / (1.0 + e2)
    route = jnp.where(lane == 0, i1,
                      jnp.where(lane == 1, i2,
                                jnp.where(lane == 2, w1, jnp.where(lane == 3, w2, 0.0))))
    route_ref[...] = route


def _pool_tail(u, wgrp, scale, wout, x, gain, rw, rb, *, seq, n_experts, tm):
    t, d = x.shape
    ng, gd, _ = wgrp.shape
    n_chunks = d // 2 // LANES
    halo_blocks = tm // POOL_HALO
    last_halo = t // POOL_HALO - 1
    return pl.pallas_call(
        functools.partial(_pool_tail_kernel, seq=seq, n_experts=n_experts),
        out_shape=(jax.ShapeDtypeStruct((t, d), F32),
                   jax.ShapeDtypeStruct((t * n_chunks, LANES), U32),
                   jax.ShapeDtypeStruct((t, LANES), F32)),
        grid=(t // tm,),
        in_specs=[
            pl.BlockSpec((tm, d), lambda i: (i, 0)),
            pl.BlockSpec((POOL_HALO, d), lambda i: (jnp.maximum(i * halo_blocks - 1, 0), 0)),
            pl.BlockSpec((POOL_HALO, d),
                         lambda i: (jnp.minimum((i + 1) * halo_blocks, last_halo), 0)),
            pl.BlockSpec((ng, gd, gd), lambda i: (0, 0, 0)),
            pl.BlockSpec((1, d), lambda i: (0, 0)),
            pl.BlockSpec((d, d), lambda i: (0, 0)),
            pl.BlockSpec((tm, d), lambda i: (i, 0)),
            pl.BlockSpec((1, d), lambda i: (0, 0)),
            pl.BlockSpec((d, LANES), lambda i: (0, 0)),
            pl.BlockSpec((1, LANES), lambda i: (0, 0)),
        ],
        out_specs=(pl.BlockSpec((tm, d), lambda i: (i, 0)),
                   pl.BlockSpec((tm * n_chunks, LANES), lambda i: (i, 0)),
                   pl.BlockSpec((tm, LANES), lambda i: (i, 0))),
        compiler_params=_cparams(("parallel",)),
        name="pool_tail",
    )(u, u, u, wgrp, scale.reshape(1, d), wout, x, gain.reshape(1, d), rw, rb)


def _routing_plan(e_flat, *, n_experts, tm, n_tiles):
    onehot = (e_flat[:, None] == jnp.arange(n_experts, dtype=I32)[None, :]).astype(I32)
    csum = jnp.cumsum(onehot, axis=0)
    rank = jnp.sum(onehot * csum, axis=1) - 1
    counts = csum[-1]
    tiles_per = (counts + tm - 1) // tm
    padded = tiles_per * tm
    offsets = jnp.cumsum(padded) - padded
    pos = jnp.sum(onehot * offsets[None, :], axis=1) + rank
    tile_end = jnp.cumsum(tiles_per)
    n_used = tile_end[-1]
    tile_ids = jnp.arange(n_tiles, dtype=I32)
    tile_expert = jnp.sum((tile_ids[:, None] >= tile_end[None, :]).astype(I32), axis=1)
    last_expert = jnp.sum((n_used - 1 >= tile_end).astype(I32))
    tile_expert = jnp.minimum(tile_expert, last_expert).astype(I32)
    return pos.astype(I32), counts.astype(I32), offsets.astype(I32), tile_expert, n_used.astype(I32)


def _dispatch_kernel(pos_ref, cnt_ref, off_ref, src_ref, dst_ref, sem, *, rpt, tm, n_experts):
    i = pl.program_id(0)
    tc = src_ref.shape[0] // rpt
    unroll = 8

    def row_copy(src_row, dst_row):
        return pltpu.make_async_copy(
            src_ref.at[pl.ds(pl.multiple_of(src_row * rpt, rpt), rpt)],
            dst_ref.at[pl.ds(pl.multiple_of(dst_row * rpt, rpt), rpt)], sem)

    def issue(jo, carry):
        for ji in range(unroll):
            j = jo * unroll + ji
            for k in range(TOP_K):
                row_copy(j, pos_ref[0, 0, TOP_K * j + k]).start()
        return carry

    lax.fori_loop(0, tc // unroll, issue, 0)
    for k in range(TOP_K):
        pltpu.make_async_copy(src_ref, dst_ref.at[pl.ds(0, tc * rpt)], sem).wait()

    @pl.when(i == pl.num_programs(0) - 1)
    def _():
        for e in range(n_experts):
            cnt = cnt_ref[e]
            n_pad = (-cnt) % tm
            first = off_ref[e] + cnt

            def fill(j, carry):
                row_copy(0, first + j).start()
                return carry

            def fill_wait(j, carry):
                row_copy(0, 0).wait()
                return carry

            lax.fori_loop(0, n_pad, fill, 0)
            lax.fori_loop(0, n_pad, fill_wait, 0)

        n_used = sum((cnt_ref[e] + tm - 1) // tm for e in range(n_experts))
        block_rows = tc * rpt

        def fill_block(j, carry):
            cp = pltpu.make_async_copy(
                src_ref,
                dst_ref.at[pl.ds(pl.multiple_of(j * block_rows, block_rows), block_rows)], sem)
            cp.start()
            cp.wait()
            return carry

        lax.fori_loop(n_used * (tm // tc), dst_ref.shape[0] // block_rows, fill_block, 0)


def _dispatch(pos, counts, offsets, packed, *, rows_per_token, n_rows, tc, tm):
    t = packed.shape[0] // rows_per_token
    n_steps = t // tc
    assert tm % tc == 0
    return pl.pallas_call(
        functools.partial(_dispatch_kernel, rpt=rows_per_token, tm=tm,
                          n_experts=counts.shape[0]),
        out_shape=jax.ShapeDtypeStruct((n_rows * rows_per_token, LANES), packed.dtype),
        grid=(n_steps,),
        in_specs=[
            pl.BlockSpec((1, 1, TOP_K * tc), lambda i: (i, 0, 0), memory_space=pltpu.SMEM),
            pl.BlockSpec(memory_space=pltpu.SMEM),
            pl.BlockSpec(memory_space=pltpu.SMEM),
            pl.BlockSpec((tc * rows_per_token, LANES), lambda i: (i, 0)),
        ],
        out_specs=pl.BlockSpec(memory_space=pl.ANY),
        scratch_shapes=[pltpu.SemaphoreType.DMA(())],
        compiler_params=_cparams(("arbitrary",)),
        name="moe_dispatch",
    )(pos.reshape(n_steps, 1, TOP_K * tc), counts, offsets, packed)


def _moe_kernel(te_ref, nu_ref, xs_ref, wg_ref, wu_ref, wd_ref, o_ref, x_scr, acc_scr, *,
                in_chunks, out_chunks):
    del te_ref
    i = pl.program_id(0)
    f = pl.program_id(1)
    tm, d = x_scr.shape
    used = i < nu_ref[0]

    @pl.when(used & (f == 0))
    def _():
        for c in range(in_chunks):
            lo, hi = _unpack_bf16_pairs(xs_ref[pl.ds(c, tm, stride=in_chunks), :])
            x_scr[:, c * LANES:(c + 1) * LANES] = lo
            x_scr[:, d // 2 + c * LANES:d // 2 + (c + 1) * LANES] = hi

    @pl.when(used)
    def _():
        a = _swiglu_act(x_scr[...], wg_ref[...], wu_ref[...])
        y = jnp.dot(a, wd_ref[...], preferred_element_type=F32)

        @pl.when(f == 0)
        def _():
            acc_scr[...] = y

        @pl.when(f > 0)
        def _():
            acc_scr[...] += y

    @pl.when(f == pl.num_programs(1) - 1)
    def _():
        for c in range(out_chunks):
            val = jnp.where(used, acc_scr[:, c * LANES:(c + 1) * LANES], 0.0)
            o_ref[pl.ds(c, tm, stride=out_chunks), :] = val


def _moe(tile_expert, n_used, xs, wg, wu, wd, *, tm, tf, n_tiles):
    n_experts, d, dff = wg.shape
    in_chunks = d // 2 // LANES
    out_chunks = d // LANES
    nf = dff // tf

    def row_map(i, f, te, nu):
        return (jnp.minimum(i, nu[0] - 1), 0)

    def f_of(i, f, nu):
        return jnp.where(i < nu[0], f, nf - 1)

    grid_spec = pltpu.PrefetchScalarGridSpec(
        num_scalar_prefetch=2,
        grid=(n_tiles, nf),
        in_specs=[
            pl.BlockSpec((tm * in_chunks, LANES), row_map),
            pl.BlockSpec((None, d, tf), lambda i, f, te, nu: (te[i], 0, f_of(i, f, nu))),
            pl.BlockSpec((None, d, tf), lambda i, f, te, nu: (te[i], 0, f_of(i, f, nu))),
            pl.BlockSpec((None, tf, d), lambda i, f, te, nu: (te[i], f_of(i, f, nu), 0)),
        ],
        out_specs=pl.BlockSpec((tm * out_chunks, LANES), lambda i, f, te, nu: (i, 0)),
        scratch_shapes=[pltpu.VMEM((tm, d), BF16), pltpu.VMEM((tm, d), F32)],
    )
    return pl.pallas_call(
        functools.partial(_moe_kernel, in_chunks=in_chunks, out_chunks=out_chunks),
        out_shape=jax.ShapeDtypeStruct((n_tiles * tm * out_chunks, LANES), F32),
        grid_spec=grid_spec,
        compiler_params=_cparams(("arbitrary", "arbitrary")),
        name="moe_grouped_swiglu",
    )(tile_expert, n_used.reshape(1), xs, wg, wu, wd)


def _combine_kernel(pos_ref, ys_ref, x_ref, route_ref, o_ref, buf_a, buf_b, sem, *, rpt):
    tc = x_ref.shape[0]

    def row_copy(src_row, dst_row, buf):
        return pltpu.make_async_copy(
            ys_ref.at[pl.ds(pl.multiple_of(src_row * rpt, rpt), rpt)],
            buf.at[pl.ds(pl.multiple_of(dst_row * rpt, rpt), rpt)], sem)

    unroll = 8

    def issue(jo, carry):
        for ji in range(unroll):
            j = jo * unroll + ji
            row_copy(pos_ref[0, 0, TOP_K * j], j, buf_a).start()
            row_copy(pos_ref[0, 0, TOP_K * j + 1], j, buf_b).start()
        return carry

    lax.fori_loop(0, tc // unroll, issue, 0)
    for buf in (buf_a, buf_b):
        pltpu.make_async_copy(ys_ref.at[pl.ds(0, tc * rpt)], buf, sem).wait()

    w1 = route_ref[:, 2:3]
    w2 = route_ref[:, 3:4]
    for c in range(rpt):
        sl = slice(c * LANES, (c + 1) * LANES)
        o_ref[:, sl] = (x_ref[:, sl] + w1 * buf_a[pl.ds(c, tc, stride=rpt), :]
                        + w2 * buf_b[pl.ds(c, tc, stride=rpt), :])


def _combine(pos, ys, x, route, *, tc):
    t, d = x.shape
    rpt = d // LANES
    n_steps = t // tc
    return pl.pallas_call(
        functools.partial(_combine_kernel, rpt=rpt),
        out_shape=jax.ShapeDtypeStruct((t, d), F32),
        grid=(n_steps,),
        in_specs=[
            pl.BlockSpec((1, 1, TOP_K * tc), lambda i: (i, 0, 0), memory_space=pltpu.SMEM),
            pl.BlockSpec(memory_space=pl.ANY),
            pl.BlockSpec((tc, d), lambda i: (i, 0)),
            pl.BlockSpec((tc, LANES), lambda i: (i, 0)),
        ],
        out_specs=pl.BlockSpec((tc, d), lambda i: (i, 0)),
        scratch_shapes=[pltpu.VMEM((tc * rpt, LANES), F32), pltpu.VMEM((tc * rpt, LANES), F32),
                        pltpu.SemaphoreType.DMA(())],
        compiler_params=_cparams(("arbitrary",)),
        name="moe_combine",
    )(pos.reshape(n_steps, 1, TOP_K * tc), ys, x, route)


def _tile(n, want):
    want = min(want, n)
    while n % want:
        want //= 2
    return want


def _attention_layer(x, seq, mix_gain, ffn_gain, rel_bias, w_qkv, q_gain, k_gain, sink, w_o,
                     w_gate, w_up, w_down):
    t, d = x.shape
    n_heads = sink.shape[0]
    n_kv_heads = (w_qkv.shape[1] // HEAD_DIM - n_heads) // 2
    head_gain = jnp.concatenate([
        jnp.tile(q_gain * HEAD_DIM ** -0.5, n_heads),
        jnp.tile(k_gain, n_kv_heads),
        jnp.ones((n_kv_heads * HEAD_DIM,), F32)]).reshape(1, -1)
    qkv = _norm_matmul(x, mix_gain, w_qkv.astype(BF16), head_gain,
                       normed_heads=n_heads + n_kv_heads, out_dtype=BF16,
                       tm=_tile(t, 1024), tn=_tile(w_qkv.shape[1], 1024))
    attn = _attention(qkv, _band_bias(rel_bias), sink, seq=seq, n_heads=n_heads,
                      n_kv_heads=n_kv_heads, tq=_tile(seq, 512))
    x, h = _proj_res_norm(attn, w_o.astype(BF16), x, ffn_gain, tm=_tile(t, 512))
    return _ffn(h, x, w_gate.astype(BF16), w_up.astype(BF16), w_down.astype(BF16),
                tm=_tile(t, 512), tf=_tile(w_gate.shape[1], 512))


def _pool_moe_layer(x, seq, mix_gain, ffn_gain, w_in, w_group, scale, w_out, router_w, router_b,
                    w_gate, w_up, w_down):
    t, d = x.shape
    n_experts = router_w.shape[1]
    u = _norm_matmul(x, mix_gain, w_in.astype(BF16), jnp.ones((1, d), F32), normed_heads=0,
                     out_dtype=F32, tm=_tile(t, 1024), tn=_tile(d, 1024))
    rw = jnp.zeros((d, LANES), F32).at[:, :n_experts].set(router_w)
    rb = jnp.zeros((1, LANES), F32).at[0, :n_experts].set(router_b)
    x, packed, route = _pool_tail(u, w_group.astype(BF16), scale, w_out.astype(BF16), x, ffn_gain,
                                  rw, rb, seq=seq, n_experts=n_experts, tm=_tile(seq, 256))
    tm = _tile(t, 1024)
    n_tiles = TOP_K * t // tm + n_experts
    e_flat = route[:, :TOP_K].astype(I32).reshape(-1)
    pos, counts, offsets, tile_expert, n_used = _routing_plan(
        e_flat, n_experts=n_experts, tm=tm, n_tiles=n_tiles)
    xs = _dispatch(pos, counts, offsets, packed, rows_per_token=d // 2 // LANES,
                   n_rows=n_tiles * tm, tc=_tile(t, 512), tm=tm)
    ys = _moe(tile_expert, n_used, xs, w_gate.astype(BF16), w_up.astype(BF16),
              w_down.astype(BF16), tm=tm, tf=_tile(w_gate.shape[2], 512), n_tiles=n_tiles)
    return _combine(pos, ys, x, route, tc=_tile(t, 256))


def kernel(x, mix_norm, ffn_norm, rel_bias, attn_w_qkv, attn_q_gain, attn_k_gain, attn_sink,
           attn_w_o, ffn_w_gate, ffn_w_up, ffn_w_down, pool_w_in, pool_w_group, pool_scale,
           pool_w_out, moe_router_w, moe_router_b, moe_w_gate, moe_w_up, moe_w_down):
    b, s, d = x.shape
    y = x.reshape(b * s, d)
    for i in range(mix_norm.shape[0]):
        j = i // 2
        if i % 2 == 0:
            y = _attention_layer(y, s, mix_norm[i], ffn_norm[i], rel_bias, attn_w_qkv[j],
                                 attn_q_gain[j], attn_k_gain[j], attn_sink[j], attn_w_o[j],
                                 ffn_w_gate[j], ffn_w_up[j], ffn_w_down[j])
        else:
            y = _pool_moe_layer(y, s, mix_norm[i], ffn_norm[i], pool_w_in[j], pool_w_group[j],
                                pool_scale[j], pool_w_out[j], moe_router_w[j], moe_router_b[j],
                                moe_w_gate[j], moe_w_up[j], moe_w_down[j])
    return y.reshape(b, s, d)
```

```python
import functools
import math

import jax
import jax.numpy as jnp
from jax import lax
from jax.experimental import pallas as pl
from jax.experimental.pallas import tpu as pltpu

F32 = jnp.float32
BF16 = jnp.bfloat16
U32 = jnp.uint32
I32 = jnp.int32

EPS = 1e-6
NEG_INF = -1e30

LANES = 128
SUBLANES = 8
VMEM_LIMIT_BYTES = 56 * 1024 * 1024

HEAD_DIM = 128
GROUP = 4
WINDOW = 128
BLOCK = 128
KEY_SPAN = BLOCK + 2 * WINDOW
NUM_BUCKETS = 32
MAX_DISTANCE = 128
POOL_WINDOWS = (2, 4, 8, 16)
POOL_HALO = 8
TOP_K = 2


def _cparams(semantics):
    return pltpu.CompilerParams(dimension_semantics=semantics,
                                vmem_limit_bytes=VMEM_LIMIT_BYTES)


def _rms(x, gain):
    ms = jnp.mean(x * x, axis=-1, keepdims=True)
    return x * lax.rsqrt(ms + EPS) * gain


def _norm_matmul_kernel(x_ref, g_ref, w_ref, hg_ref, o_ref, h_scr, *, normed_heads):
    j = pl.program_id(1)
    tn = o_ref.shape[1]
    heads_per_tile = tn // HEAD_DIM

    @pl.when(j == 0)
    def _():
        h_scr[...] = _rms(x_ref[...], g_ref[...]).astype(BF16)

    acc = jnp.dot(h_scr[...], w_ref[...], preferred_element_type=F32)

    if normed_heads == 0:
        o_ref[...] = acc.astype(o_ref.dtype)
        return

    def store(first_raw_head):
        for c in range(heads_per_tile):
            sl = slice(c * HEAD_DIM, (c + 1) * HEAD_DIM)
            chunk = acc[:, sl]
            if c < first_raw_head:
                chunk = _rms(chunk, hg_ref[:, sl])
            o_ref[:, sl] = chunk.astype(o_ref.dtype)

    full_tiles, rem = divmod(normed_heads, heads_per_tile)

    @pl.when(j < full_tiles)
    def _():
        store(heads_per_tile)

    @pl.when(j == full_tiles)
    def _():
        store(rem)

    @pl.when(j > full_tiles)
    def _():
        store(0)


def _norm_matmul(x, gain, w, head_gain, *, normed_heads, out_dtype, tm, tn):
    t, d = x.shape
    n = w.shape[1]
    return pl.pallas_call(
        functools.partial(_norm_matmul_kernel, normed_heads=normed_heads),
        out_shape=jax.ShapeDtypeStruct((t, n), out_dtype),
        grid=(t // tm, n // tn),
        in_specs=[
            pl.BlockSpec((tm, d), lambda i, j: (i, 0)),
            pl.BlockSpec((1, d), lambda i, j: (0, 0)),
            pl.BlockSpec((d, tn), lambda i, j: (0, j)),
            pl.BlockSpec((1, tn), lambda i, j: (0, j)),
        ],
        out_specs=pl.BlockSpec((tm, tn), lambda i, j: (i, j)),
        scratch_shapes=[pltpu.VMEM((tm, d), BF16)],
        compiler_params=_cparams(("parallel", "arbitrary")),
        name="norm_matmul",
    )(x, gain.reshape(1, d), w, head_gain)


def _attn_kernel(sink_ref, q_ref, kv_ref, kvp_ref, kvn_ref, bias_ref, o_ref, *,
                 n_kv_heads, steps_per_seq):
    i = pl.program_id(0) % steps_per_seq
    tq = q_ref.shape[0]
    nqb = tq // BLOCK
    kw = n_kv_heads * HEAD_DIM
    prev_penalty = jnp.where(i > 0, 0.0, NEG_INF)
    next_penalty = jnp.where(i < steps_per_seq - 1, 0.0, NEG_INF)
    col =lax.broadcasted_iota(I32, (GROUP * BLOCK, KEY_SPAN), 1)

    for h in range(n_kv_heads):
        ksl = slice(h * HEAD_DIM, (h + 1) * HEAD_DIM)
        vsl = slice(kw + h * HEAD_DIM, kw + (h + 1) * HEAD_DIM)
        bias = bias_ref[h * GROUP:(h + 1) * GROUP].reshape(GROUP * BLOCK, KEY_SPAN)
        for qb in range(nqb):
            rows = slice(qb * BLOCK, (qb + 1) * BLOCK)
            prev_rows = slice((qb - 1) * BLOCK, qb * BLOCK)
            next_rows = slice((qb + 1) * BLOCK, (qb + 2) * BLOCK)
            if qb == 0:
                k_prev, v_prev = kvp_ref[:, ksl], kvp_ref[:, vsl]
            else:
                k_prev, v_prev = kv_ref[prev_rows, ksl], kv_ref[prev_rows, vsl]
            if qb == nqb - 1:
                k_next, v_next = kvn_ref[:, ksl], kvn_ref[:, vsl]
            else:
                k_next, v_next = kv_ref[next_rows, ksl], kv_ref[next_rows, vsl]
            k3 = jnp.concatenate([k_prev, kv_ref[rows, ksl], k_next], axis=0)
            v3 = jnp.concatenate([v_prev, kv_ref[rows, vsl], v_next], axis=0)
            q4 = jnp.concatenate(
                [q_ref[rows, (h * GROUP + g) * HEAD_DIM:(h * GROUP + g + 1) * HEAD_DIM]
                 for g in range(GROUP)], axis=0)
            s = lax.dot_general(q4, k3, (((1,), (1,)), ((), ())),
                                preferred_element_type=F32) + bias
            if qb == 0:
                s = s + jnp.where(col < WINDOW, prev_penalty, 0.0)
            if qb == nqb - 1:
                s = s + jnp.where(col >= WINDOW + BLOCK, next_penalty, 0.0)
            outs = []
            for g in range(GROUP):
                sg = s[g * BLOCK:(g + 1) * BLOCK]
                sink = sink_ref[h * GROUP + g]
                m = jnp.maximum(jnp.max(sg, axis=-1, keepdims=True), sink)
                p = jnp.exp(sg - m)
                denom = jnp.sum(p, axis=-1, keepdims=True) + jnp.exp(sink - m)
                pv = jnp.dot(p.astype(BF16), v3, preferred_element_type=F32)
                outs.append(pv / denom)
            for g in range(GROUP):
                o_ref[rows, (h * GROUP + g) * HEAD_DIM:(h * GROUP + g + 1) * HEAD_DIM] = (
                    outs[g].astype(o_ref.dtype))


def _attention(qkv, bias, sink, *, seq, n_heads, n_kv_heads, tq):
    t = qkv.shape[0]
    dq = n_heads * HEAD_DIM
    dkv = 2 * n_kv_heads * HEAD_DIM
    assert dq % dkv == 0
    kv_col = dq // dkv
    steps_per_seq = seq // tq
    blocks_per_step = tq // BLOCK
    last_block = t // BLOCK - 1
    return pl.pallas_call(
        functools.partial(_attn_kernel, n_kv_heads=n_kv_heads, steps_per_seq=steps_per_seq),
        out_shape=jax.ShapeDtypeStruct((t, dq), BF16),
        grid=(t // tq,),
        in_specs=[
            pl.BlockSpec(memory_space=pltpu.SMEM),
            pl.BlockSpec((tq, dq), lambda r: (r, 0)),
            pl.BlockSpec((tq, dkv), lambda r: (r, kv_col)),
            pl.BlockSpec((BLOCK, dkv),
                         lambda r: (jnp.maximum(r * blocks_per_step - 1, 0), kv_col)),
            pl.BlockSpec((BLOCK, dkv),
                         lambda r: (jnp.minimum((r + 1) * blocks_per_step, last_block), kv_col)),
            pl.BlockSpec((n_heads, BLOCK, KEY_SPAN), lambda r: (0, 0, 0)),
        ],
        out_specs=pl.BlockSpec((tq, dq), lambda r: (r, 0)),
        compiler_params=_cparams(("parallel",)),
        name="window_attention",
    )(sink, qkv, qkv, qkv, qkv, bias)


def _t5_bucket(rel):
    half = NUM_BUCKETS // 2
    ret = jnp.where(rel > 0, half, 0)
    n = jnp.abs(rel)
    max_exact = half // 2
    nf = jnp.maximum(n, 1).astype(F32)
    large = max_exact + (jnp.log(nf / max_exact) / math.log(MAX_DISTANCE / max_exact)
                         * (half - max_exact)).astype(I32)
    large = jnp.minimum(large, half - 1)
    return ret + jnp.where(n < max_exact, n, large)


def _band_bias(rel_bias):
    qi = jnp.arange(BLOCK)[:, None]
    kj = jnp.arange(KEY_SPAN)[None, :]
    rel = kj - WINDOW - qi
    onehot = (_t5_bucket(rel)[:, :, None] == jnp.arange(NUM_BUCKETS)[None, None, :]).astype(F32)
    bias = jnp.einsum("qkb,bh->hqk", onehot, rel_bias.astype(F32),
                      precision=lax.Precision.HIGHEST)
    return jnp.where((jnp.abs(rel) <= WINDOW)[None], bias, NEG_INF)


def _proj_res_norm_kernel(a_ref, w_ref, x_ref, g_ref, xo_ref, ho_ref):
    xo = x_ref[...] + jnp.dot(a_ref[...], w_ref[...], preferred_element_type=F32)
    xo_ref[...] = xo
    ho_ref[...] = _rms(xo, g_ref[...]).astype(BF16)


def _proj_res_norm(a, w, x, gain, *, tm):
    t, d = x.shape
    k = a.shape[1]
    return pl.pallas_call(
        _proj_res_norm_kernel,
        out_shape=(jax.ShapeDtypeStruct((t, d), F32), jax.ShapeDtypeStruct((t, d), BF16)),
        grid=(t // tm,),
        in_specs=[
            pl.BlockSpec((tm, k), lambda i: (i, 0)),
            pl.BlockSpec((k, d), lambda i: (0, 0)),
            pl.BlockSpec((tm, d), lambda i: (i, 0)),
            pl.BlockSpec((1, d), lambda i: (0, 0)),
        ],
        out_specs=(pl.BlockSpec((tm, d), lambda i: (i, 0)),
                   pl.BlockSpec((tm, d), lambda i: (i, 0))),
        compiler_params=_cparams(("parallel",)),
        name="proj_res_norm",
    )(a, w, x, gain.reshape(1, d))


def _swiglu_act(h, wg, wu):
    g = jnp.dot(h, wg, preferred_element_type=F32)
    u = jnp.dot(h, wu, preferred_element_type=F32)
    return (g * jax.nn.sigmoid(g) * u).astype(BF16)


def _down_proj(a_scr, wd):
    a = jnp.concatenate([a_scr[k] for k in range(a_scr.shape[0])], axis=1)
    return jnp.dot(a, wd, preferred_element_type=F32)


def _ffn_kernel(h_ref, x_ref, wg_ref, wu_ref, wd_ref, o_ref, a_scr):
    s = pl.program_id(1)
    nf = a_scr.shape[0]

    @pl.when(s < nf)
    def _():
        a_scr[s] = _swiglu_act(h_ref[...], wg_ref[...], wu_ref[...])

    @pl.when(s >= nf)
    def _():
        o_ref[...] = x_ref[...] + _down_proj(a_scr, wd_ref[...])


def _ffn(h, x, wg, wu, wd, *, tm, tf, tn):
    t, d = x.shape
    dff = wg.shape[1]
    nf = dff // tf
    nn = d // tn
    up_map = lambda i, s: (0, jnp.minimum(s, nf - 1))
    out_map = lambda i, s: (i, jnp.maximum(s - nf, 0))
    return pl.pallas_call(
        _ffn_kernel,
        out_shape=jax.ShapeDtypeStruct((t, d), F32),
        grid=(t // tm, nf + nn),
        in_specs=[
            pl.BlockSpec((tm, d), lambda i, s: (i, 0)),
            pl.BlockSpec((tm, tn), out_map),
            pl.BlockSpec((d, tf), up_map),
            pl.BlockSpec((d, tf), up_map),
            pl.BlockSpec((dff, tn), lambda i, s: (0, jnp.maximum(s - nf, 0))),
        ],
        out_specs=pl.BlockSpec((tm, tn), out_map),
        scratch_shapes=[pltpu.VMEM((nf, tm, tf), BF16)],
        compiler_params=_cparams(("parallel", "arbitrary")),
        name="dense_swiglu",
    )(h, x, wg, wu, wd)


def _pack_bf16_pairs(lo, hi):
    lo_bits = lax.bitcast_convert_type(lo.astype(BF16).astype(F32), U32)
    hi_bits = lax.bitcast_convert_type(hi.astype(BF16).astype(F32), U32)
    return (lo_bits >> 16) | hi_bits


def _unpack_bf16_pairs(words):
    lo = lax.bitcast_convert_type(words << 16, F32).astype(BF16)
    hi = lax.bitcast_convert_type(words & jnp.uint32(0xFFFF0000), F32).astype(BF16)
    return lo, hi


def _pool_tail_kernel(u_ref, up_ref, un_ref, wgrp_ref, scale_ref, wout_ref, x_ref, g_ref,
                      rw_ref, rb_ref, xo_ref, hp_ref, route_ref, *, seq, n_experts):
    tm, d = x_ref.shape
    gd = d // len(POOL_WINDOWS)
    ext = tm + 2 * POOL_HALO
    start = (pl.program_id(0) * tm) % seq
    pos = start + lax.broadcasted_iota(I32, (tm, 1), 0)

    ys = []
    for gi, win in enumerate(POOL_WINDOWS):
        cs = slice(gi * gd, (gi + 1) * gd)
        u = u_ref[:, cs]
        up = jnp.where(start > 0, up_ref[:, cs], 0.0)
        un = jnp.where(start + tm < seq, un_ref[:, cs], 0.0)
        a = jnp.concatenate([up, u, un], axis=0)
        a = a + pltpu.roll(a, 1, axis=0)
        w = 2
        while w < win:
            a = pltpu.roll(a, w // 2, axis=0) + pltpu.roll(a, ext - w // 2, axis=0)
            w *= 2
        half = win // 2
        count = (jnp.minimum(pos + half, seq) - jnp.maximum(pos - half, 0)).astype(F32)
        diff = a[POOL_HALO:POOL_HALO + tm] / count - u
        y = jnp.dot(diff.astype(BF16), wgrp_ref[gi], preferred_element_type=F32)
        ys.append((y * scale_ref[:, cs]).astype(BF16))
    y = jnp.concatenate(ys, axis=1)
    xo = x_ref[...] + jnp.dot(y, wout_ref[...], preferred_element_type=F32)
    xo_ref[...] = xo

    h = _rms(xo, g_ref[...])
    half_d = d // 2
    words = _pack_bf16_pairs(h[:, :half_d], h[:, half_d:])
    n_chunks = half_d // LANES
    for c in range(n_chunks):
        hp_ref[pl.ds(c, tm, stride=n_chunks), :] = words[:, c * LANES:(c + 1) * LANES]

    h_hi = h.astype(BF16)
    h_lo = (h - h_hi.astype(F32)).astype(BF16)
    prod = jnp.dot(jnp.concatenate([h_hi, h_lo], axis=0), rw_ref[...],
                   preferred_element_type=F32)
    part = prod[:tm] + prod[tm:]
    logits = part + pltpu.roll(part, LANES - n_experts, axis=1) + rb_ref[...]
    lane = lax.broadcasted_iota(I32, logits.shape, 1)
    logits = jnp.where(lane < n_experts, logits, -jnp.inf)
    lane_f = lane.astype(F32)
    m1 = jnp.max(logits, axis=-1, keepdims=True)
    i1 = jnp.min(jnp.where(logits == m1, lane_f, float(LANES)), axis=-1, keepdims=True)
    rest = jnp.where(lane_f == i1, -jnp.inf, logits)
    m2 = jnp.max(rest, axis=-1, keepdims=True)
    i2 = jnp.min(jnp.where(rest == m2, lane_f, float(LANES)), axis=-1, keepdims=True)
    e2 = jnp.exp(m2 - m1)
    w1 = 1.0 / (1.0 + e2)
    w2 = e2 / (1.0 + e2)
    route = jnp.where(lane == 0, i1,
                      jnp.where(lane == 1, i2,
                                jnp.where(lane == 2, w1, jnp.where(lane == 3, w2, 0.0))))
    route_ref[...] = route


def _pool_tail(u, wgrp, scale, wout, x, gain, rw, rb, *, seq, n_experts, tm):
    t, d = x.shape
    ng, gd, _ = wgrp.shape
    n_chunks = d // 2 // LANES
    halo_blocks = tm // POOL_HALO
    last_halo = t // POOL_HALO - 1
    return pl.pallas_call(
        functools.partial(_pool_tail_kernel, seq=seq, n_experts=n_experts),
        out_shape=(jax.ShapeDtypeStruct((t, d), F32),
                   jax.ShapeDtypeStruct((t * n_chunks, LANES), U32),
                   jax.ShapeDtypeStruct((t, LANES), F32)),
        grid=(t // tm,),
        in_specs=[
            pl.BlockSpec((tm, d), lambda i: (i, 0)),
            pl.BlockSpec((POOL_HALO, d), lambda i: (jnp.maximum(i * halo_blocks - 1, 0), 0)),
            pl.BlockSpec((POOL_HALO, d),
                         lambda i: (jnp.minimum((i + 1) * halo_blocks, last_halo), 0)),
            pl.BlockSpec((ng, gd, gd), lambda i: (0, 0, 0)),
            pl.BlockSpec((1, d), lambda i: (0, 0)),
            pl.BlockSpec((d, d), lambda i: (0, 0)),
            pl.BlockSpec((tm, d), lambda i: (i, 0)),
            pl.BlockSpec((1, d), lambda i: (0, 0)),
            pl.BlockSpec((d, LANES), lambda i: (0, 0)),
            pl.BlockSpec((1, LANES), lambda i: (0, 0)),
        ],
        out_specs=(pl.BlockSpec((tm, d), lambda i: (i, 0)),
                   pl.BlockSpec((tm * n_chunks, LANES), lambda i: (i, 0)),
                   pl.BlockSpec((tm, LANES), lambda i: (i, 0))),
        compiler_params=_cparams(("parallel",)),
        name="pool_tail",
    )(u, u, u, wgrp, scale.reshape(1, d), wout, x, gain.reshape(1, d), rw, rb)


def _routing_plan(e_flat, *, n_experts, tm, n_tiles):
    onehot = (e_flat[:, None] == jnp.arange(n_experts, dtype=I32)[None, :]).astype(I32)
    csum = jnp.cumsum(onehot, axis=0)
    rank = jnp.sum(onehot * csum, axis=1) - 1
    counts = csum[-1]
    tiles_per = (counts + tm - 1) // tm
    padded = tiles_per * tm
    offsets = jnp.cumsum(padded) - padded
    pos = jnp.sum(onehot * offsets[None, :], axis=1) + rank
    tile_end = jnp.cumsum(tiles_per)
    n_used = tile_end[-1]
    tile_ids = jnp.arange(n_tiles, dtype=I32)
    tile_expert = jnp.sum((tile_ids[:, None] >= tile_end[None, :]).astype(I32), axis=1)
    last_expert = jnp.sum((n_used - 1 >= tile_end).astype(I32))
    tile_expert = jnp.minimum(tile_expert, last_expert).astype(I32)
    return pos.astype(I32), counts.astype(I32), offsets.astype(I32), tile_expert, n_used.astype(I32)


def _dispatch_kernel(pos_ref, cnt_ref, off_ref, src_ref, dst_ref, sem, *, rpt, tm, n_experts):
    i = pl.program_id(0)
    tc = src_ref.shape[0] // rpt
    unroll = 8

    def row_copy(src_row, dst_row):
        return pltpu.make_async_copy(
            src_ref.at[pl.ds(pl.multiple_of(src_row * rpt, rpt), rpt)],
            dst_ref.at[pl.ds(pl.multiple_of(dst_row * rpt, rpt), rpt)], sem)

    def issue(jo, carry):
        for ji in range(unroll):
            j = jo * unroll + ji
            for k in range(TOP_K):
                row_copy(j, pos_ref[0, 0, TOP_K * j + k]).start()
        return carry

    lax.fori_loop(0, tc // unroll, issue, 0)
    for k in range(TOP_K):
        pltpu.make_async_copy(src_ref, dst_ref.at[pl.ds(0, tc * rpt)], sem).wait()

    @pl.when(i == pl.num_programs(0) - 1)
    def _():
        for e in range(n_experts):
            cnt = cnt_ref[e]
            n_pad = (-cnt) % tm
            first = off_ref[e] + cnt

            def fill(j, carry):
                row_copy(0, first + j).start()
                return carry

            def fill_wait(j, carry):
                row_copy(0, 0).wait()
                return carry

            lax.fori_loop(0, n_pad, fill, 0)
            lax.fori_loop(0, n_pad, fill_wait, 0)

        n_used = sum((cnt_ref[e] + tm - 1) // tm for e in range(n_experts))
        block_rows = tc * rpt

        def fill_block(j, carry):
            cp = pltpu.make_async_copy(
                src_ref,
                dst_ref.at[pl.ds(pl.multiple_of(j * block_rows, block_rows), block_rows)], sem)
            cp.start()
            cp.wait()
            return carry

        lax.fori_loop(n_used * (tm // tc), dst_ref.shape[0] // block_rows, fill_block, 0)


def _dispatch(pos, counts, offsets, packed, *, rows_per_token, n_rows, tc, tm):
    t = packed.shape[0] // rows_per_token
    n_steps = t // tc
    assert tm % tc == 0
    return pl.pallas_call(
        functools.partial(_dispatch_kernel, rpt=rows_per_token, tm=tm,
                          n_experts=counts.shape[0]),
        out_shape=jax.ShapeDtypeStruct((n_rows * rows_per_token, LANES), packed.dtype),
        grid=(n_steps,),
        in_specs=[
            pl.BlockSpec((1, 1, TOP_K * tc), lambda i: (i, 0, 0), memory_space=pltpu.SMEM),
            pl.BlockSpec(memory_space=pltpu.SMEM),
            pl.BlockSpec(memory_space=pltpu.SMEM),
            pl.BlockSpec((tc * rows_per_token, LANES), lambda i: (i, 0)),
        ],
        out_specs=pl.BlockSpec(memory_space=pl.ANY),
        scratch_shapes=[pltpu.SemaphoreType.DMA(())],
        compiler_params=_cparams(("arbitrary",)),
        name="moe_dispatch",
    )(pos.reshape(n_steps, 1, TOP_K * tc), counts, offsets, packed)


def _moe_kernel(te_ref, nu_ref, xs_ref, wg_ref, wu_ref, wd_ref, o_ref, x_scr, a_scr, *,
                in_chunks, out_chunks):
    del te_ref
    i = pl.program_id(0)
    s = pl.program_id(1)
    tm, d = x_scr.shape
    nf = a_scr.shape[0]
    tn = wd_ref.shape[1]
    nn = d // tn
    used = i < nu_ref[0]

    @pl.when(used & (s == 0))
    def _():
        for c in range(in_chunks):
            lo, hi = _unpack_bf16_pairs(xs_ref[pl.ds(c, tm, stride=in_chunks), :])
            x_scr[:, c * LANES:(c + 1) * LANES] = lo
            x_scr[:, d // 2 + c * LANES:d // 2 + (c + 1) * LANES] = hi

    @pl.when(used & (s < nf))
    def _():
        a_scr[s] = _swiglu_act(x_scr[...], wg_ref[...], wu_ref[...])

    def store_columns(n, y):
        for c in range(tn // LANES):
            chunk = n * (tn // LANES) + c
            o_ref[pl.ds(chunk, tm, stride=out_chunks), :] = y[:, c * LANES:(c + 1) * LANES]

    for n in range(nn):
        @pl.when(used & (s == nf + n))
        def _():
            store_columns(n, _down_proj(a_scr, wd_ref[...]))

        @pl.when(jnp.logical_not(used) & (s == nf + n))
        def _():
            store_columns(n, jnp.zeros((tm, tn), F32))


def _moe(tile_expert, n_used, xs, wg, wu, wd, *, tm, tf, tn, n_tiles):
    n_experts, d, dff = wg.shape
    in_chunks = d // 2 // LANES
    out_chunks = d // LANES
    nf = dff // tf
    nn = d // tn

    def row_map(i, s, te, nu):
        return (jnp.minimum(i, nu[0] - 1), 0)

    def up_map(i, s, te, nu):
        return (te[i], 0, jnp.where(i < nu[0], jnp.minimum(s, nf - 1), nf - 1))

    def down_map(i, s, te, nu):
        return (te[i], 0, jnp.where(i < nu[0], jnp.maximum(s - nf, 0), nn - 1))

    grid_spec = pltpu.PrefetchScalarGridSpec(
        num_scalar_prefetch=2,
        grid=(n_tiles, nf + nn),
        in_specs=[
            pl.BlockSpec((tm * in_chunks, LANES), row_map),
            pl.BlockSpec((None, d, tf), up_map),
            pl.BlockSpec((None, d, tf), up_map),
            pl.BlockSpec((None, dff, tn), down_map),
        ],
        out_specs=pl.BlockSpec((tm * out_chunks, LANES), lambda i, s, te, nu: (i, 0)),
        scratch_shapes=[pltpu.VMEM((tm, d), BF16), pltpu.VMEM((nf, tm, tf), BF16)],
    )
    return pl.pallas_call(
        functools.partial(_moe_kernel, in_chunks=in_chunks, out_chunks=out_chunks),
        out_shape=jax.ShapeDtypeStruct((n_tiles * tm * out_chunks, LANES), F32),
        grid_spec=grid_spec,
        compiler_params=_cparams(("arbitrary", "arbitrary")),
        name="moe_grouped_swiglu",
    )(tile_expert, n_used.reshape(1), xs, wg, wu, wd)


def _combine_kernel(pos_ref, ys_ref, x_ref, route_ref, o_ref, buf_a, buf_b, sem, *, rpt):
    tc = x_ref.shape[0]

    def row_copy(src_row, dst_row, buf):
        return pltpu.make_async_copy(
            ys_ref.at[pl.ds(pl.multiple_of(src_row * rpt, rpt), rpt)],
            buf.at[pl.ds(pl.multiple_of(dst_row * rpt, rpt), rpt)], sem)

    unroll = 8

    def issue(jo, carry):
        for ji in range(unroll):
            j = jo * unroll + ji
            row_copy(pos_ref[0, 0, TOP_K * j], j, buf_a).start()
            row_copy(pos_ref[0, 0, TOP_K * j + 1], j, buf_b).start()
        return carry

    lax.fori_loop(0, tc // unroll, issue, 0)
    for buf in (buf_a, buf_b):
        pltpu.make_async_copy(ys_ref.at[pl.ds(0, tc * rpt)], buf, sem).wait()

    w1 = route_ref[:, 2:3]
    w2 = route_ref[:, 3:4]
    for c in range(rpt):
        sl = slice(c * LANES, (c + 1) * LANES)
        o_ref[:, sl] = (x_ref[:, sl] + w1 * buf_a[pl.ds(c, tc, stride=rpt), :]
                        + w2 * buf_b[pl.ds(c, tc, stride=rpt), :])


def _combine(pos, ys, x, route, *, tc):
    t, d = x.shape
    rpt = d // LANES
    n_steps = t // tc
    return pl.pallas_call(
        functools.partial(_combine_kernel, rpt=rpt),
        out_shape=jax.ShapeDtypeStruct((t, d), F32),
        grid=(n_steps,),
        in_specs=[
            pl.BlockSpec((1, 1, TOP_K * tc), lambda i: (i, 0, 0), memory_space=pltpu.SMEM),
            pl.BlockSpec(memory_space=pl.ANY),
            pl.BlockSpec((tc, d), lambda i: (i, 0)),
            pl.BlockSpec((tc, LANES), lambda i: (i, 0)),
        ],
        out_specs=pl.BlockSpec((tc, d), lambda i: (i, 0)),
        scratch_shapes=[pltpu.VMEM((tc * rpt, LANES), F32), pltpu.VMEM((tc * rpt, LANES), F32),
                        pltpu.SemaphoreType.DMA(())],
        compiler_params=_cparams(("arbitrary",)),
        name="moe_combine",
    )(pos.reshape(n_steps, 1, TOP_K * tc), ys, x, route)


def _tile(n, want):
    want = min(want, n)
    while n % want:
        want //= 2
    return want


def _attention_layer(x, seq, mix_gain, ffn_gain, rel_bias, w_qkv, q_gain, k_gain, sink, w_o,
                     w_gate, w_up, w_down):
    t, d = x.shape
    n_heads = sink.shape[0]
    n_kv_heads = (w_qkv.shape[1] // HEAD_DIM - n_heads) // 2
    head_gain = jnp.concatenate([
        jnp.tile(q_gain * HEAD_DIM ** -0.5, n_heads),
        jnp.tile(k_gain, n_kv_heads),
        jnp.ones((n_kv_heads * HEAD_DIM,), F32)]).reshape(1, -1)
    qkv = _norm_matmul(x, mix_gain, w_qkv.astype(BF16), head_gain,
                       normed_heads=n_heads + n_kv_heads, out_dtype=BF16,
                       tm=_tile(t, 1024), tn=_tile(w_qkv.shape[1], 1024))
    attn = _attention(qkv, _band_bias(rel_bias), sink, seq=seq, n_heads=n_heads,
                      n_kv_heads=n_kv_heads, tq=_tile(seq, 512))
    x, h = _proj_res_norm(attn, w_o.astype(BF16), x, ffn_gain, tm=_tile(t, 512))
    return _ffn(h, x, w_gate.astype(BF16), w_up.astype(BF16), w_down.astype(BF16),
                tm=_tile(t, 1024), tf=_tile(w_gate.shape[1], 512), tn=_tile(d, 512))


def _pool_moe_layer(x, seq, mix_gain, ffn_gain, w_in, w_group, scale, w_out, router_w, router_b,
                    w_gate, w_up, w_down):
    t, d = x.shape
    n_experts = router_w.shape[1]
    u = _norm_matmul(x, mix_gain, w_in.astype(BF16), jnp.ones((1, d), F32), normed_heads=0,
                     out_dtype=F32, tm=_tile(t, 1024), tn=_tile(d, 1024))
    rw_hi = router_w.astype(BF16)
    rw_lo = (router_w - rw_hi.astype(F32)).astype(BF16)
    rw = (jnp.zeros((d, LANES), BF16).at[:, :n_experts].set(rw_hi)
          .at[:, n_experts:2 * n_experts].set(rw_lo))
    rb = jnp.zeros((1, LANES), F32).at[0, :n_experts].set(router_b)
    x, packed, route = _pool_tail(u, w_group.astype(BF16), scale, w_out.astype(BF16), x, ffn_gain,
                                  rw, rb, seq=seq, n_experts=n_experts, tm=_tile(seq, 256))
    tm = _tile(t, 512)
    n_tiles = TOP_K * t // tm + n_experts
    e_flat = route[:, :TOP_K].astype(I32).reshape(-1)
    pos, counts, offsets, tile_expert, n_used = _routing_plan(
        e_flat, n_experts=n_experts, tm=tm, n_tiles=n_tiles)
    xs = _dispatch(pos, counts, offsets, packed, rows_per_token=d // 2 // LANES,
                   n_rows=n_tiles * tm, tc=_tile(t, 512), tm=tm)
    ys = _moe(tile_expert, n_used, xs, w_gate.astype(BF16), w_up.astype(BF16),
              w_down.astype(BF16), tm=tm, tf=_tile(w_gate.shape[2], 1024), tn=_tile(d, 512),
              n_tiles=n_tiles)
    return _combine(pos, ys, x, route, tc=_tile(t, 256))


def kernel(x, mix_norm, ffn_norm, rel_bias, attn_w_qkv, attn_q_gain, attn_k_gain, attn_sink,
           attn_w_o, ffn_w_gate, ffn_w_up, ffn_w_down, pool_w_in, pool_w_group, pool_scale,
           pool_w_out, moe_router_w, moe_router_b, moe_w_gate, moe_w_up, moe_w_down):
    b, s, d = x.shape
    y = x.reshape(b * s, d)
    for i in range(mix_norm.shape[0]):
        j = i // 2
        if i % 2 == 0:
            y = _attention_layer(y, s, mix_norm[i], ffn_norm[i], rel_bias, attn_w_qkv[j],
                                 attn_q_gain[j], attn_k_gain[j], attn_sink[j], attn_w_o[j],
                                 ffn_w_gate[j], ffn_w_up[j], ffn_w_down[j])
        else:
            y = _pool_moe_layer(y, s, mix_norm[i], ffn_norm[i], pool_w_in[j], pool_w_group[j],
                                pool_scale[j], pool_w_out[j], moe_router_w[j], moe_router_b[j],
                                moe_w_gate[j], moe_w_up[j], moe_w_down[j])
    return y.reshape(b, s, d)
```

```python
import functools
import math

import jax
import jax.numpy as jnp
from jax import lax
from jax.experimental import pallas as pl
from jax.experimental.pallas import tpu as pltpu

F32 = jnp.float32
BF16 = jnp.bfloat16
U32 = jnp.uint32
I32 = jnp.int32

EPS = 1e-6
NEG_INF = -1e30

LANES = 128
SUBLANES = 8
VMEM_LIMIT_BYTES = 56 * 1024 * 1024

HEAD_DIM = 128
GROUP = 4
WINDOW = 128
BLOCK = 128
KEY_SPAN = BLOCK + 2 * WINDOW
NUM_BUCKETS = 32
MAX_DISTANCE = 128
POOL_WINDOWS = (2, 4, 8, 16)
POOL_HALO = 8
TOP_K = 2
MOE_ROW_TILE = 768


def _cparams(semantics):
    return pltpu.CompilerParams(dimension_semantics=semantics,
                                vmem_limit_bytes=VMEM_LIMIT_BYTES)


def _rms(x, gain):
    ms = jnp.mean(x * x, axis=-1, keepdims=True)
    return x * lax.rsqrt(ms + EPS) * gain


def _norm_matmul_kernel(x_ref, w_ref, hg_ref, o_ref, *, normed_heads, chunk):
    x = x_ref[...]
    inv = lax.rsqrt(jnp.mean(x * x, axis=-1, keepdims=True) + EPS)
    xb = x.astype(BF16)
    for c0 in range(0, o_ref.shape[1], chunk):
        acc = jnp.dot(xb, w_ref[:, c0:c0 + chunk], preferred_element_type=F32) * inv
        for h0 in range(c0, c0 + chunk, HEAD_DIM):
            sl = slice(h0, h0 + HEAD_DIM)
            piece = acc[:, h0 - c0:h0 - c0 + HEAD_DIM]
            if h0 // HEAD_DIM < normed_heads:
                piece = _rms(piece, hg_ref[:, sl])
            o_ref[:, sl] = piece.astype(o_ref.dtype)


def _norm_matmul(x, w, head_gain, *, normed_heads, out_dtype, tm, chunk):
    t, d = x.shape
    n = w.shape[1]
    return pl.pallas_call(
        functools.partial(_norm_matmul_kernel, normed_heads=normed_heads, chunk=chunk),
        out_shape=jax.ShapeDtypeStruct((t, n), out_dtype),
        grid=(t // tm,),
        in_specs=[
            pl.BlockSpec((tm, d), lambda i: (i, 0)),
            pl.BlockSpec((d, n), lambda i: (0, 0)),
            pl.BlockSpec((1, n), lambda i: (0, 0)),
        ],
        out_specs=pl.BlockSpec((tm, n), lambda i: (i, 0)),
        compiler_params=_cparams(("parallel",)),
        name="norm_matmul",
    )(x, w, head_gain)


def _attn_kernel(sink_ref, q_ref, kv_ref, kvp_ref, kvn_ref, bias_ref, o_ref, *,
                 n_kv_heads, steps_per_seq):
    i = pl.program_id(0) % steps_per_seq
    tq = q_ref.shape[0]
    nqb = tq // BLOCK
    kw = n_kv_heads * HEAD_DIM
    prev_penalty = jnp.where(i > 0, 0.0, NEG_INF)
    next_penalty = jnp.where(i < steps_per_seq - 1, 0.0, NEG_INF)
    col =lax.broadcasted_iota(I32, (GROUP * BLOCK, KEY_SPAN), 1)

    for h in range(n_kv_heads):
        ksl = slice(h * HEAD_DIM, (h + 1) * HEAD_DIM)
        vsl = slice(kw + h * HEAD_DIM, kw + (h + 1) * HEAD_DIM)
        bias = bias_ref[h * GROUP:(h + 1) * GROUP].reshape(GROUP * BLOCK, KEY_SPAN)
        for qb in range(nqb):
            rows = slice(qb * BLOCK, (qb + 1) * BLOCK)
            prev_rows = slice((qb - 1) * BLOCK, qb * BLOCK)
            next_rows = slice((qb + 1) * BLOCK, (qb + 2) * BLOCK)
            if qb == 0:
                k_prev, v_prev = kvp_ref[:, ksl], kvp_ref[:, vsl]
            else:
                k_prev, v_prev = kv_ref[prev_rows, ksl], kv_ref[prev_rows, vsl]
            if qb == nqb - 1:
                k_next, v_next = kvn_ref[:, ksl], kvn_ref[:, vsl]
            else:
                k_next, v_next = kv_ref[next_rows, ksl], kv_ref[next_rows, vsl]
            k3 = jnp.concatenate([k_prev, kv_ref[rows, ksl], k_next], axis=0)
            v3 = jnp.concatenate([v_prev, kv_ref[rows, vsl], v_next], axis=0)
            q4 = jnp.concatenate(
                [q_ref[rows, (h * GROUP + g) * HEAD_DIM:(h * GROUP + g + 1) * HEAD_DIM]
                 for g in range(GROUP)], axis=0)
            s = lax.dot_general(q4, k3, (((1,), (1,)), ((), ())),
                                preferred_element_type=F32) + bias
            if qb == 0:
                s = s + jnp.where(col < WINDOW, prev_penalty, 0.0)
            if qb == nqb - 1:
                s = s + jnp.where(col >= WINDOW + BLOCK, next_penalty, 0.0)
            outs = []
            for g in range(GROUP):
                sg = s[g * BLOCK:(g + 1) * BLOCK]
                sink = sink_ref[h * GROUP + g]
                m = jnp.maximum(jnp.max(sg, axis=-1, keepdims=True), sink)
                p = jnp.exp(sg - m)
                denom = jnp.sum(p, axis=-1, keepdims=True) + jnp.exp(sink - m)
                pv = jnp.dot(p.astype(BF16), v3, preferred_element_type=F32)
                outs.append(pv / denom)
            for g in range(GROUP):
                o_ref[rows, (h * GROUP + g) * HEAD_DIM:(h * GROUP + g + 1) * HEAD_DIM] = (
                    outs[g].astype(o_ref.dtype))


def _attention(qkv, bias, sink, *, seq, n_heads, n_kv_heads, tq):
    t = qkv.shape[0]
    dq = n_heads * HEAD_DIM
    dkv = 2 * n_kv_heads * HEAD_DIM
    assert dq % dkv == 0
    kv_col = dq // dkv
    steps_per_seq = seq // tq
    blocks_per_step = tq // BLOCK
    last_block = t // BLOCK - 1
    return pl.pallas_call(
        functools.partial(_attn_kernel, n_kv_heads=n_kv_heads, steps_per_seq=steps_per_seq),
        out_shape=jax.ShapeDtypeStruct((t, dq), BF16),
        grid=(t // tq,),
        in_specs=[
            pl.BlockSpec(memory_space=pltpu.SMEM),
            pl.BlockSpec((tq, dq), lambda r: (r, 0)),
            pl.BlockSpec((tq, dkv), lambda r: (r, kv_col)),
            pl.BlockSpec((BLOCK, dkv),
                         lambda r: (jnp.maximum(r * blocks_per_step - 1, 0), kv_col)),
            pl.BlockSpec((BLOCK, dkv),
                         lambda r: (jnp.minimum((r + 1) * blocks_per_step, last_block), kv_col)),
            pl.BlockSpec((n_heads, BLOCK, KEY_SPAN), lambda r: (0, 0, 0)),
        ],
        out_specs=pl.BlockSpec((tq, dq), lambda r: (r, 0)),
        compiler_params=_cparams(("parallel",)),
        name="window_attention",
    )(sink, qkv, qkv, qkv, qkv, bias)


def _t5_bucket(rel):
    half = NUM_BUCKETS // 2
    ret = jnp.where(rel > 0, half, 0)
    n = jnp.abs(rel)
    max_exact = half // 2
    nf = jnp.maximum(n, 1).astype(F32)
    large = max_exact + (jnp.log(nf / max_exact) / math.log(MAX_DISTANCE / max_exact)
                         * (half - max_exact)).astype(I32)
    large = jnp.minimum(large, half - 1)
    return ret + jnp.where(n < max_exact, n, large)


def _band_bias(rel_bias):
    qi = jnp.arange(BLOCK)[:, None]
    kj = jnp.arange(KEY_SPAN)[None, :]
    rel = kj - WINDOW - qi
    onehot = (_t5_bucket(rel)[:, :, None] == jnp.arange(NUM_BUCKETS)[None, None, :]).astype(F32)
    bias = jnp.einsum("qkb,bh->hqk", onehot, rel_bias.astype(F32),
                      precision=lax.Precision.HIGHEST)
    return jnp.where((jnp.abs(rel) <= WINDOW)[None], bias, NEG_INF)


def _proj_res_norm_kernel(a_ref, w_ref, x_ref, g_ref, xo_ref, ho_ref):
    xo = x_ref[...] + jnp.dot(a_ref[...], w_ref[...], preferred_element_type=F32)
    xo_ref[...] = xo
    ho_ref[...] = _rms(xo, g_ref[...]).astype(BF16)


def _proj_res_norm(a, w, x, gain, *, tm):
    t, d = x.shape
    k = a.shape[1]
    return pl.pallas_call(
        _proj_res_norm_kernel,
        out_shape=(jax.ShapeDtypeStruct((t, d), F32), jax.ShapeDtypeStruct((t, d), BF16)),
        grid=(t // tm,),
        in_specs=[
            pl.BlockSpec((tm, k), lambda i: (i, 0)),
            pl.BlockSpec((k, d), lambda i: (0, 0)),
            pl.BlockSpec((tm, d), lambda i: (i, 0)),
            pl.BlockSpec((1, d), lambda i: (0, 0)),
        ],
        out_specs=(pl.BlockSpec((tm, d), lambda i: (i, 0)),
                   pl.BlockSpec((tm, d), lambda i: (i, 0))),
        compiler_params=_cparams(("parallel",)),
        name="proj_res_norm",
    )(a, w, x, gain.reshape(1, d))


def _swiglu_act(h, wg, wu):
    g = jnp.dot(h, wg, preferred_element_type=F32)
    u = jnp.dot(h, wu, preferred_element_type=F32)
    return (g * jax.nn.sigmoid(g) * u).astype(BF16)


def _down_proj(a_scr, wd):
    a = jnp.concatenate([a_scr[k] for k in range(a_scr.shape[0])], axis=1)
    return jnp.dot(a, wd, preferred_element_type=F32)


def _ffn_kernel(h_ref, x_ref, wg_ref, wu_ref, wd_ref, o_ref, a_scr):
    s = pl.program_id(1)
    nf = a_scr.shape[0]

    @pl.when(s < nf)
    def _():
        a_scr[s] = _swiglu_act(h_ref[...], wg_ref[...], wu_ref[...])

    @pl.when(s >= nf)
    def _():
        o_ref[...] = x_ref[...] + _down_proj(a_scr, wd_ref[...])


def _ffn(h, x, wg, wu, wd, *, tm, tf, tn):
    t, d = x.shape
    dff = wg.shape[1]
    nf = dff // tf
    nn = d // tn
    up_map = lambda i, s: (0, jnp.minimum(s, nf - 1))
    out_map = lambda i, s: (i, jnp.maximum(s - nf, 0))
    return pl.pallas_call(
        _ffn_kernel,
        out_shape=jax.ShapeDtypeStruct((t, d), F32),
        grid=(t // tm, nf + nn),
        in_specs=[
            pl.BlockSpec((tm, d), lambda i, s: (i, 0)),
            pl.BlockSpec((tm, tn), out_map),
            pl.BlockSpec((d, tf), up_map),
            pl.BlockSpec((d, tf), up_map),
            pl.BlockSpec((dff, tn), lambda i, s: (0, jnp.maximum(s - nf, 0))),
        ],
        out_specs=pl.BlockSpec((tm, tn), out_map),
        scratch_shapes=[pltpu.VMEM((nf, tm, tf), BF16)],
        compiler_params=_cparams(("parallel", "arbitrary")),
        name="dense_swiglu",
    )(h, x, wg, wu, wd)


def _pack_bf16_pairs(lo, hi):
    lo_bits = lax.bitcast_convert_type(lo.astype(BF16).astype(F32), U32)
    hi_bits = lax.bitcast_convert_type(hi.astype(BF16).astype(F32), U32)
    return (lo_bits >> 16) | hi_bits


def _unpack_bf16_pairs(words):
    lo = lax.bitcast_convert_type(words << 16, F32).astype(BF16)
    hi = lax.bitcast_convert_type(words & jnp.uint32(0xFFFF0000), F32).astype(BF16)
    return lo, hi


def _pool_tail_kernel(u_ref, up_ref, un_ref, wgrp_ref, scale_ref, wout_ref, x_ref, g_ref,
                      rw_ref, rb_ref, xo_ref, hp_ref, route_ref, *, seq, n_experts):
    tm, d = x_ref.shape
    gd = d // len(POOL_WINDOWS)
    ext = tm + 2 * POOL_HALO
    start = (pl.program_id(0) * tm) % seq
    pos = start + lax.broadcasted_iota(I32, (tm, 1), 0)

    ys = []
    for gi, win in enumerate(POOL_WINDOWS):
        cs = slice(gi * gd, (gi + 1) * gd)
        u = u_ref[:, cs]
        up = jnp.where(start > 0, up_ref[:, cs], 0.0)
        un = jnp.where(start + tm < seq, un_ref[:, cs], 0.0)
        a = jnp.concatenate([up, u, un], axis=0)
        a = a + pltpu.roll(a, 1, axis=0)
        w = 2
        while w < win:
            a = pltpu.roll(a, w // 2, axis=0) + pltpu.roll(a, ext - w // 2, axis=0)
            w *= 2
        half = win // 2
        count = (jnp.minimum(pos + half, seq) - jnp.maximum(pos - half, 0)).astype(F32)
        diff = a[POOL_HALO:POOL_HALO + tm] / count - u
        y = jnp.dot(diff.astype(BF16), wgrp_ref[gi], preferred_element_type=F32)
        ys.append((y * scale_ref[:, cs]).astype(BF16))
    y = jnp.concatenate(ys, axis=1)
    xo = x_ref[...] + jnp.dot(y, wout_ref[...], preferred_element_type=F32)
    xo_ref[...] = xo

    h = _rms(xo, g_ref[...])
    half_d = d // 2
    words = _pack_bf16_pairs(h[:, :half_d], h[:, half_d:])
    n_chunks = half_d // LANES
    for c in range(n_chunks):
        hp_ref[pl.ds(c, tm, stride=n_chunks), :] = words[:, c * LANES:(c + 1) * LANES]

    h_hi = h.astype(BF16)
    h_lo = (h - h_hi.astype(F32)).astype(BF16)
    prod = jnp.dot(jnp.concatenate([h_hi, h_lo], axis=0), rw_ref[...],
                   preferred_element_type=F32)
    part = prod[:tm] + prod[tm:]
    logits = part + pltpu.roll(part, LANES - n_experts, axis=1) + rb_ref[...]
    lane = lax.broadcasted_iota(I32, logits.shape, 1)
    logits = jnp.where(lane < n_experts, logits, -jnp.inf)
    lane_f = lane.astype(F32)
    m1 = jnp.max(logits, axis=-1, keepdims=True)
    i1 = jnp.min(jnp.where(logits == m1, lane_f, float(LANES)), axis=-1, keepdims=True)
    rest = jnp.where(lane_f == i1, -jnp.inf, logits)
    m2 = jnp.max(rest, axis=-1, keepdims=True)
    i2 = jnp.min(jnp.where(rest == m2, lane_f, float(LANES)), axis=-1, keepdims=True)
    e2 = jnp.exp(m2 - m1)
    w1 = 1.0 / (1.0 + e2)
    w2 = e2 / (1.0 + e2)
    route = jnp.where(lane == 0, i1,
                      jnp.where(lane == 1, i2,
                                jnp.where(lane == 2, w1, jnp.where(lane == 3, w2, 0.0))))
    route_ref[...] = route


def _pool_tail(u, wgrp, scale, wout, x, gain, rw, rb, *, seq, n_experts, tm):
    t, d = x.shape
    ng, gd, _ = wgrp.shape
    n_chunks = d // 2 // LANES
    halo_blocks = tm // POOL_HALO
    last_halo = t // POOL_HALO - 1
    return pl.pallas_call(
        functools.partial(_pool_tail_kernel, seq=seq, n_experts=n_experts),
        out_shape=(jax.ShapeDtypeStruct((t, d), F32),
                   jax.ShapeDtypeStruct((t * n_chunks, LANES), U32),
                   jax.ShapeDtypeStruct((t, LANES), F32)),
        grid=(t // tm,),
        in_specs=[
            pl.BlockSpec((tm, d), lambda i: (i, 0)),
            pl.BlockSpec((POOL_HALO, d), lambda i: (jnp.maximum(i * halo_blocks - 1, 0), 0)),
            pl.BlockSpec((POOL_HALO, d),
                         lambda i: (jnp.minimum((i + 1) * halo_blocks, last_halo), 0)),
            pl.BlockSpec((ng, gd, gd), lambda i: (0, 0, 0)),
            pl.BlockSpec((1, d), lambda i: (0, 0)),
            pl.BlockSpec((d, d), lambda i: (0, 0)),
            pl.BlockSpec((tm, d), lambda i: (i, 0)),
            pl.BlockSpec((1, d), lambda i: (0, 0)),
            pl.BlockSpec((d, LANES), lambda i: (0, 0)),
            pl.BlockSpec((1, LANES), lambda i: (0, 0)),
        ],
        out_specs=(pl.BlockSpec((tm, d), lambda i: (i, 0)),
                   pl.BlockSpec((tm * n_chunks, LANES), lambda i: (i, 0)),
                   pl.BlockSpec((tm, LANES), lambda i: (i, 0))),
        compiler_params=_cparams(("parallel",)),
        name="pool_tail",
    )(u, u, u, wgrp, scale.reshape(1, d), wout, x, gain.reshape(1, d), rw, rb)


def _routing_plan(e_flat, *, n_experts, tm, n_tiles):
    onehot = (e_flat[:, None] == jnp.arange(n_experts, dtype=I32)[None, :]).astype(I32)
    csum = jnp.cumsum(onehot, axis=0)
    rank = jnp.sum(onehot * csum, axis=1) - 1
    counts = csum[-1]
    tiles_per = (counts + tm - 1) // tm
    padded = tiles_per * tm
    offsets = jnp.cumsum(padded) - padded
    pos = jnp.sum(onehot * offsets[None, :], axis=1) + rank
    tile_end = jnp.cumsum(tiles_per)
    n_used = tile_end[-1]
    tile_ids = jnp.arange(n_tiles, dtype=I32)
    tile_expert = jnp.sum((tile_ids[:, None] >= tile_end[None, :]).astype(I32), axis=1)
    last_expert = jnp.sum((n_used - 1 >= tile_end).astype(I32))
    tile_expert = jnp.minimum(tile_expert, last_expert).astype(I32)
    return pos.astype(I32), counts.astype(I32), offsets.astype(I32), tile_expert, n_used.astype(I32)


def _dispatch_kernel(pos_ref, cnt_ref, off_ref, src_ref, dst_ref, sem, *, rpt, tm, n_experts):
    i = pl.program_id(0)
    tc = src_ref.shape[0] // rpt
    unroll = 8

    def row_copy(src_row, dst_row):
        return pltpu.make_async_copy(
            src_ref.at[pl.ds(pl.multiple_of(src_row * rpt, rpt), rpt)],
            dst_ref.at[pl.ds(pl.multiple_of(dst_row * rpt, rpt), rpt)], sem)

    def issue(jo, carry):
        for ji in range(unroll):
            j = jo * unroll + ji
            for k in range(TOP_K):
                row_copy(j, pos_ref[0, 0, TOP_K * j + k]).start()
        return carry

    lax.fori_loop(0, tc // unroll, issue, 0)
    for k in range(TOP_K):
        pltpu.make_async_copy(src_ref, dst_ref.at[pl.ds(0, tc * rpt)], sem).wait()

    @pl.when(i == pl.num_programs(0) - 1)
    def _():
        for e in range(n_experts):
            cnt = cnt_ref[e]
            n_pad = (-cnt) % tm
            first = off_ref[e] + cnt

            def fill(j, carry):
                row_copy(0, first + j).start()
                return carry

            def fill_wait(j, carry):
                row_copy(0, 0).wait()
                return carry

            lax.fori_loop(0, n_pad, fill, 0)
            lax.fori_loop(0, n_pad, fill_wait, 0)

        n_used = sum((cnt_ref[e] + tm - 1) // tm for e in range(n_experts))
        block_rows = tc * rpt

        def fill_block(j, carry):
            cp = pltpu.make_async_copy(
                src_ref,
                dst_ref.at[pl.ds(pl.multiple_of(j * block_rows, block_rows), block_rows)], sem)
            cp.start()
            cp.wait()
            return carry

        lax.fori_loop(n_used * (tm // tc), dst_ref.shape[0] // block_rows, fill_block, 0)


def _dispatch(pos, counts, offsets, packed, *, rows_per_token, n_rows, tc, tm):
    t = packed.shape[0] // rows_per_token
    n_steps = t // tc
    assert tm % tc == 0
    return pl.pallas_call(
        functools.partial(_dispatch_kernel, rpt=rows_per_token, tm=tm,
                          n_experts=counts.shape[0]),
        out_shape=jax.ShapeDtypeStruct((n_rows * rows_per_token, LANES), packed.dtype),
        grid=(n_steps,),
        in_specs=[
            pl.BlockSpec((1, 1, TOP_K * tc), lambda i: (i, 0, 0), memory_space=pltpu.SMEM),
            pl.BlockSpec(memory_space=pltpu.SMEM),
            pl.BlockSpec(memory_space=pltpu.SMEM),
            pl.BlockSpec((tc * rows_per_token, LANES), lambda i: (i, 0)),
        ],
        out_specs=pl.BlockSpec(memory_space=pl.ANY),
        scratch_shapes=[pltpu.SemaphoreType.DMA(())],
        compiler_params=_cparams(("arbitrary",)),
        name="moe_dispatch",
    )(pos.reshape(n_steps, 1, TOP_K * tc), counts, offsets, packed)


def _moe_kernel(te_ref, nu_ref, xs_ref, wg_ref, wu_ref, wd_ref, o_ref, x_scr, a_scr, *,
                in_chunks, out_chunks):
    del te_ref
    i = pl.program_id(0)
    s = pl.program_id(1)
    tm, d = x_scr.shape
    nf = a_scr.shape[0]
    tn = wd_ref.shape[1]
    nn = d // tn
    used = i < nu_ref[0]

    @pl.when(used & (s == 0))
    def _():
        for c in range(in_chunks):
            lo, hi = _unpack_bf16_pairs(xs_ref[pl.ds(c, tm, stride=in_chunks), :])
            x_scr[:, c * LANES:(c + 1) * LANES] = lo
            x_scr[:, d // 2 + c * LANES:d // 2 + (c + 1) * LANES] = hi

    @pl.when(used & (s < nf))
    def _():
        a_scr[s] = _swiglu_act(x_scr[...], wg_ref[...], wu_ref[...])

    def store_columns(n, y):
        for c in range(tn // LANES):
            chunk = n * (tn // LANES) + c
            o_ref[pl.ds(chunk, tm, stride=out_chunks), :] = y[:, c * LANES:(c + 1) * LANES]

    for n in range(nn):
        @pl.when(used & (s == nf + n))
        def _():
            store_columns(n, _down_proj(a_scr, wd_ref[...]))

        @pl.when(jnp.logical_not(used) & (s == nf + n))
        def _():
            store_columns(n, jnp.zeros((tm, tn), F32))


def _moe(tile_expert, n_used, xs, wg, wu, wd, *, tm, tf, tn, n_tiles):
    n_experts, d, dff = wg.shape
    in_chunks = d // 2 // LANES
    out_chunks = d // LANES
    nf = dff // tf
    nn = d // tn

    def row_map(i, s, te, nu):
        return (jnp.minimum(i, nu[0] - 1), 0)

    def up_map(i, s, te, nu):
        return (te[i], 0, jnp.where(i < nu[0], jnp.minimum(s, nf - 1), nf - 1))

    def down_map(i, s, te, nu):
        return (te[i], 0, jnp.where(i < nu[0], jnp.maximum(s - nf, 0), nn - 1))

    grid_spec = pltpu.PrefetchScalarGridSpec(
        num_scalar_prefetch=2,
        grid=(n_tiles, nf + nn),
        in_specs=[
            pl.BlockSpec((tm * in_chunks, LANES), row_map),
            pl.BlockSpec((None, d, tf), up_map),
            pl.BlockSpec((None, d, tf), up_map),
            pl.BlockSpec((None, dff, tn), down_map),
        ],
        out_specs=pl.BlockSpec((tm * out_chunks, LANES), lambda i, s, te, nu: (i, 0)),
        scratch_shapes=[pltpu.VMEM((tm, d), BF16), pltpu.VMEM((nf, tm, tf), BF16)],
    )
    return pl.pallas_call(
        functools.partial(_moe_kernel, in_chunks=in_chunks, out_chunks=out_chunks),
        out_shape=jax.ShapeDtypeStruct((n_tiles * tm * out_chunks, LANES), F32),
        grid_spec=grid_spec,
        compiler_params=_cparams(("arbitrary", "arbitrary")),
        name="moe_grouped_swiglu",
    )(tile_expert, n_used.reshape(1), xs, wg, wu, wd)


def _combine_kernel(pos_ref, ys_ref, x_ref, route_ref, o_ref, buf_a, buf_b, sem, *, rpt):
    tc = x_ref.shape[0]

    def row_copy(src_row, dst_row, buf):
        return pltpu.make_async_copy(
            ys_ref.at[pl.ds(pl.multiple_of(src_row * rpt, rpt), rpt)],
            buf.at[pl.ds(pl.multiple_of(dst_row * rpt, rpt), rpt)], sem)

    unroll = 8

    def issue(jo, carry):
        for ji in range(unroll):
            j = jo * unroll + ji
            row_copy(pos_ref[0, 0, TOP_K * j], j, buf_a).start()
            row_copy(pos_ref[0, 0, TOP_K * j + 1], j, buf_b).start()
        return carry

    lax.fori_loop(0, tc // unroll, issue, 0)
    for buf in (buf_a, buf_b):
        pltpu.make_async_copy(ys_ref.at[pl.ds(0, tc * rpt)], buf, sem).wait()

    w1 = route_ref[:, 2:3]
    w2 = route_ref[:, 3:4]
    for c in range(rpt):
        sl = slice(c * LANES, (c + 1) * LANES)
        o_ref[:, sl] = (x_ref[:, sl] + w1 * buf_a[pl.ds(c, tc, stride=rpt), :]
                        + w2 * buf_b[pl.ds(c, tc, stride=rpt), :])


def _combine(pos, ys, x, route, *, tc):
    t, d = x.shape
    rpt = d // LANES
    n_steps = t // tc
    return pl.pallas_call(
        functools.partial(_combine_kernel, rpt=rpt),
        out_shape=jax.ShapeDtypeStruct((t, d), F32),
        grid=(n_steps,),
        in_specs=[
            pl.BlockSpec((1, 1, TOP_K * tc), lambda i: (i, 0, 0), memory_space=pltpu.SMEM),
            pl.BlockSpec(memory_space=pl.ANY),
            pl.BlockSpec((tc, d), lambda i: (i, 0)),
            pl.BlockSpec((tc, LANES), lambda i: (i, 0)),
        ],
        out_specs=pl.BlockSpec((tc, d), lambda i: (i, 0)),
        scratch_shapes=[pltpu.VMEM((tc * rpt, LANES), F32), pltpu.VMEM((tc * rpt, LANES), F32),
                        pltpu.SemaphoreType.DMA(())],
        compiler_params=_cparams(("arbitrary",)),
        name="moe_combine",
    )(pos.reshape(n_steps, 1, TOP_K * tc), ys, x, route)


def _tile(n, want):
    want = min(want, n)
    while n % want:
        want //= 2
    return want


def _attention_layer(x, seq, mix_gain, ffn_gain, rel_bias, w_qkv, q_gain, k_gain, sink, w_o,
                     w_gate, w_up, w_down):
    t, d = x.shape
    n_heads = sink.shape[0]
    n_kv_heads = (w_qkv.shape[1] // HEAD_DIM - n_heads) // 2
    head_gain = jnp.concatenate([
        jnp.tile(q_gain * HEAD_DIM ** -0.5, n_heads),
        jnp.tile(k_gain, n_kv_heads),
        jnp.ones((n_kv_heads * HEAD_DIM,), F32)]).reshape(1, -1)
    qkv = _norm_matmul(x, (mix_gain[:, None] * w_qkv).astype(BF16), head_gain,
                       normed_heads=n_heads + n_kv_heads, out_dtype=BF16,
                       tm=_tile(t, 512), chunk=_tile(w_qkv.shape[1], 1024))
    attn = _attention(qkv, _band_bias(rel_bias), sink, seq=seq, n_heads=n_heads,
                      n_kv_heads=n_kv_heads, tq=_tile(seq, 512))
    x, h = _proj_res_norm(attn, w_o.astype(BF16), x, ffn_gain, tm=_tile(t, 512))
    return _ffn(h, x, w_gate.astype(BF16), w_up.astype(BF16), w_down.astype(BF16),
                tm=_tile(t, 1024), tf=_tile(w_gate.shape[1], 512), tn=_tile(d, 512))


def _pool_moe_layer(x, seq, mix_gain, ffn_gain, w_in, w_group, scale, w_out, router_w, router_b,
                    w_gate, w_up, w_down):
    t, d = x.shape
    n_experts = router_w.shape[1]
    u = _norm_matmul(x, (mix_gain[:, None] * w_in).astype(BF16), jnp.ones((1, d), F32),
                     normed_heads=0, out_dtype=F32, tm=_tile(t, 512), chunk=_tile(d, 1024))
    rw_hi = router_w.astype(BF16)
    rw_lo = (router_w - rw_hi.astype(F32)).astype(BF16)
    rw = (jnp.zeros((d, LANES), BF16).at[:, :n_experts].set(rw_hi)
          .at[:, n_experts:2 * n_experts].set(rw_lo))
    rb = jnp.zeros((1, LANES), F32).at[0, :n_experts].set(router_b)
    x, packed, route = _pool_tail(u, w_group.astype(BF16), scale, w_out.astype(BF16), x, ffn_gain,
                                  rw, rb, seq=seq, n_experts=n_experts, tm=_tile(seq, 256))
    tm = MOE_ROW_TILE
    n_tiles = TOP_K * t // tm + n_experts
    e_flat = route[:, :TOP_K].astype(I32).reshape(-1)
    pos, counts, offsets, tile_expert, n_used = _routing_plan(
        e_flat, n_experts=n_experts, tm=tm, n_tiles=n_tiles)
    xs = _dispatch(pos, counts, offsets, packed, rows_per_token=d // 2 // LANES,
                   n_rows=n_tiles * tm, tc=_tile(t, MOE_ROW_TILE // 3), tm=tm)
    ys = _moe(tile_expert, n_used, xs, w_gate.astype(BF16), w_up.astype(BF16),
              w_down.astype(BF16), tm=tm, tf=_tile(w_gate.shape[2], 512), tn=_tile(d, 256),
              n_tiles=n_tiles)
    return _combine(pos, ys, x, route, tc=_tile(t, 256))


def kernel(x, mix_norm, ffn_norm, rel_bias, attn_w_qkv, attn_q_gain, attn_k_gain, attn_sink,
           attn_w_o, ffn_w_gate, ffn_w_up, ffn_w_down, pool_w_in, pool_w_group, pool_scale,
           pool_w_out, moe_router_w, moe_router_b, moe_w_gate, moe_w_up, moe_w_down):
    b, s, d = x.shape
    y = x.reshape(b * s, d)
    for i in range(mix_norm.shape[0]):
        j = i // 2
        if i % 2 == 0:
            y = _attention_layer(y, s, mix_norm[i], ffn_norm[i], rel_bias, attn_w_qkv[j],
                                 attn_q_gain[j], attn_k_gain[j], attn_sink[j], attn_w_o[j],
                                 ffn_w_gate[j], ffn_w_up[j], ffn_w_down[j])
        else:
            y = _pool_moe_layer(y, s, mix_norm[i], ffn_norm[i], pool_w_in[j], pool_w_group[j],
                                pool_scale[j], pool_w_out[j], moe_router_w[j], moe_router_b[j],
                                moe_w_gate[j], moe_w_up[j], moe_w_down[j])
    return y.reshape(b, s, d)
```

```python
import functools
import math

import jax
import jax.numpy as jnp
from jax import lax
from jax.experimental import pallas as pl
from jax.experimental.pallas import tpu as pltpu

F32 = jnp.float32
BF16 = jnp.bfloat16
U32 = jnp.uint32
I32 = jnp.int32

EPS = 1e-6
NEG_INF = -1e30

LANES = 128
SUBLANES = 8
VMEM_LIMIT_BYTES = 56 * 1024 * 1024

HEAD_DIM = 128
GROUP = 4
WINDOW = 128
BLOCK = 128
KEY_SPAN = BLOCK + 2 * WINDOW
NUM_BUCKETS = 32
MAX_DISTANCE = 128
POOL_WINDOWS = (2, 4, 8, 16)
POOL_HALO = 8
TOP_K = 2
MOE_ROW_TILE = 768


def _cparams(semantics):
    return pltpu.CompilerParams(dimension_semantics=semantics,
                                vmem_limit_bytes=VMEM_LIMIT_BYTES)


def _rms(x, gain):
    ms = jnp.mean(x * x, axis=-1, keepdims=True)
    return x * lax.rsqrt(ms + EPS) * gain


def _norm_matmul_kernel(x_ref, w_ref, hg_ref, o_ref, *, normed_heads, chunk):
    x = x_ref[...]
    inv = lax.rsqrt(jnp.mean(x * x, axis=-1, keepdims=True) + EPS)
    xb = x.astype(BF16)
    for c0 in range(0, o_ref.shape[1], chunk):
        acc = jnp.dot(xb, w_ref[:, c0:c0 + chunk], preferred_element_type=F32) * inv
        for h0 in range(c0, c0 + chunk, HEAD_DIM):
            sl = slice(h0, h0 + HEAD_DIM)
            piece = acc[:, h0 - c0:h0 - c0 + HEAD_DIM]
            if h0 // HEAD_DIM < normed_heads:
                piece = _rms(piece, hg_ref[:, sl])
            o_ref[:, sl] = piece.astype(o_ref.dtype)


def _norm_matmul(x, w, head_gain, *, normed_heads, out_dtype, tm, chunk):
    t, d = x.shape
    n = w.shape[1]
    return pl.pallas_call(
        functools.partial(_norm_matmul_kernel, normed_heads=normed_heads, chunk=chunk),
        out_shape=jax.ShapeDtypeStruct((t, n), out_dtype),
        grid=(t // tm,),
        in_specs=[
            pl.BlockSpec((tm, d), lambda i: (i, 0)),
            pl.BlockSpec((d, n), lambda i: (0, 0)),
            pl.BlockSpec((1, n), lambda i: (0, 0)),
        ],
        out_specs=pl.BlockSpec((tm, n), lambda i: (i, 0)),
        compiler_params=_cparams(("parallel",)),
        name="norm_matmul",
    )(x, w, head_gain)


def _attn_kernel(sink_ref, q_ref, kv_ref, kvp_ref, kvn_ref, bias_ref, o_ref, *,
                 n_kv_heads, steps_per_seq):
    i = pl.program_id(0) % steps_per_seq
    tq = q_ref.shape[0]
    nqb = tq // BLOCK
    kw = n_kv_heads * HEAD_DIM
    prev_penalty = jnp.where(i > 0, 0.0, NEG_INF)
    next_penalty = jnp.where(i < steps_per_seq - 1, 0.0, NEG_INF)
    col =lax.broadcasted_iota(I32, (GROUP * BLOCK, KEY_SPAN), 1)

    for h in range(n_kv_heads):
        ksl = slice(h * HEAD_DIM, (h + 1) * HEAD_DIM)
        vsl = slice(kw + h * HEAD_DIM, kw + (h + 1) * HEAD_DIM)
        bias = bias_ref[h * GROUP:(h + 1) * GROUP].reshape(GROUP * BLOCK, KEY_SPAN)
        for qb in range(nqb):
            rows = slice(qb * BLOCK, (qb + 1) * BLOCK)
            prev_rows = slice((qb - 1) * BLOCK, qb * BLOCK)
            next_rows = slice((qb + 1) * BLOCK, (qb + 2) * BLOCK)
            if qb == 0:
                k_prev, v_prev = kvp_ref[:, ksl], kvp_ref[:, vsl]
            else:
                k_prev, v_prev = kv_ref[prev_rows, ksl], kv_ref[prev_rows, vsl]
            if qb == nqb - 1:
                k_next, v_next = kvn_ref[:, ksl], kvn_ref[:, vsl]
            else:
                k_next, v_next = kv_ref[next_rows, ksl], kv_ref[next_rows, vsl]
            k3 = jnp.concatenate([k_prev, kv_ref[rows, ksl], k_next], axis=0)
            v3 = jnp.concatenate([v_prev, kv_ref[rows, vsl], v_next], axis=0)
            q4 = jnp.concatenate(
                [q_ref[rows, (h * GROUP + g) * HEAD_DIM:(h * GROUP + g + 1) * HEAD_DIM]
                 for g in range(GROUP)], axis=0)
            s = lax.dot_general(q4, k3, (((1,), (1,)), ((), ())),
                                preferred_element_type=F32) + bias
            if qb == 0:
                s = s + jnp.where(col < WINDOW, prev_penalty, 0.0)
            if qb == nqb - 1:
                s = s + jnp.where(col >= WINDOW + BLOCK, next_penalty, 0.0)
            outs = []
            for g in range(GROUP):
                sg = s[g * BLOCK:(g + 1) * BLOCK]
                sink = sink_ref[h * GROUP + g]
                m = jnp.maximum(jnp.max(sg, axis=-1, keepdims=True), sink)
                p = jnp.exp(sg - m)
                denom = jnp.sum(p, axis=-1, keepdims=True) + jnp.exp(sink - m)
                pv = jnp.dot(p.astype(BF16), v3, preferred_element_type=F32)
                outs.append(pv / denom)
            for g in range(GROUP):
                o_ref[rows, (h * GROUP + g) * HEAD_DIM:(h * GROUP + g + 1) * HEAD_DIM] = (
                    outs[g].astype(o_ref.dtype))


def _attention(qkv, bias, sink, *, seq, n_heads, n_kv_heads, tq):
    t = qkv.shape[0]
    dq = n_heads * HEAD_DIM
    dkv = 2 * n_kv_heads * HEAD_DIM
    assert dq % dkv == 0
    kv_col = dq // dkv
    steps_per_seq = seq // tq
    blocks_per_step = tq // BLOCK
    last_block = t // BLOCK - 1
    return pl.pallas_call(
        functools.partial(_attn_kernel, n_kv_heads=n_kv_heads, steps_per_seq=steps_per_seq),
        out_shape=jax.ShapeDtypeStruct((t, dq), BF16),
        grid=(t // tq,),
        in_specs=[
            pl.BlockSpec(memory_space=pltpu.SMEM),
            pl.BlockSpec((tq, dq), lambda r: (r, 0)),
            pl.BlockSpec((tq, dkv), lambda r: (r, kv_col)),
            pl.BlockSpec((BLOCK, dkv),
                         lambda r: (jnp.maximum(r * blocks_per_step - 1, 0), kv_col)),
            pl.BlockSpec((BLOCK, dkv),
                         lambda r: (jnp.minimum((r + 1) * blocks_per_step, last_block), kv_col)),
            pl.BlockSpec((n_heads, BLOCK, KEY_SPAN), lambda r: (0, 0, 0)),
        ],
        out_specs=pl.BlockSpec((tq, dq), lambda r: (r, 0)),
        compiler_params=_cparams(("parallel",)),
        name="window_attention",
    )(sink, qkv, qkv, qkv, qkv, bias)


def _t5_bucket(rel):
    half = NUM_BUCKETS // 2
    ret = jnp.where(rel > 0, half, 0)
    n = jnp.abs(rel)
    max_exact = half // 2
    nf = jnp.maximum(n, 1).astype(F32)
    large = max_exact + (jnp.log(nf / max_exact) / math.log(MAX_DISTANCE / max_exact)
                         * (half - max_exact)).astype(I32)
    large = jnp.minimum(large, half - 1)
    return ret + jnp.where(n < max_exact, n, large)


def _band_bias(rel_bias):
    qi = jnp.arange(BLOCK)[:, None]
    kj = jnp.arange(KEY_SPAN)[None, :]
    rel = kj - WINDOW - qi
    onehot = (_t5_bucket(rel)[:, :, None] == jnp.arange(NUM_BUCKETS)[None, None, :]).astype(F32)
    bias = jnp.einsum("qkb,bh->hqk", onehot, rel_bias.astype(F32),
                      precision=lax.Precision.HIGHEST)
    return jnp.where((jnp.abs(rel) <= WINDOW)[None], bias, NEG_INF)


def _proj_res_norm_kernel(a_ref, w_ref, x_ref, g_ref, xo_ref, ho_ref):
    xo = x_ref[...] + jnp.dot(a_ref[...], w_ref[...], preferred_element_type=F32)
    xo_ref[...] = xo
    ho_ref[...] = _rms(xo, g_ref[...]).astype(BF16)


def _proj_res_norm(a, w, x, gain, *, tm):
    t, d = x.shape
    k = a.shape[1]
    return pl.pallas_call(
        _proj_res_norm_kernel,
        out_shape=(jax.ShapeDtypeStruct((t, d), F32), jax.ShapeDtypeStruct((t, d), BF16)),
        grid=(t // tm,),
        in_specs=[
            pl.BlockSpec((tm, k), lambda i: (i, 0)),
            pl.BlockSpec((k, d), lambda i: (0, 0)),
            pl.BlockSpec((tm, d), lambda i: (i, 0)),
            pl.BlockSpec((1, d), lambda i: (0, 0)),
        ],
        out_specs=(pl.BlockSpec((tm, d), lambda i: (i, 0)),
                   pl.BlockSpec((tm, d), lambda i: (i, 0))),
        compiler_params=_cparams(("parallel",)),
        name="proj_res_norm",
    )(a, w, x, gain.reshape(1, d))


def _swiglu_act(h, wg, wu):
    g = jnp.dot(h, wg, preferred_element_type=F32)
    u = jnp.dot(h, wu, preferred_element_type=F32)
    return (g * jax.nn.sigmoid(g) * u).astype(BF16)


def _down_proj(a_scr, wd):
    a = jnp.concatenate([a_scr[k] for k in range(a_scr.shape[0])], axis=1)
    return jnp.dot(a, wd, preferred_element_type=F32)


def _ffn_kernel(h_ref, x_ref, wg_ref, wu_ref, wd_ref, o_ref, a_scr):
    s = pl.program_id(1)
    nf = a_scr.shape[0]

    @pl.when(s < nf)
    def _():
        a_scr[s] = _swiglu_act(h_ref[...], wg_ref[...], wu_ref[...])

    @pl.when(s >= nf)
    def _():
        o_ref[...] = x_ref[...] + _down_proj(a_scr, wd_ref[...])


def _column_tiles(w, tile):
    *lead, k, n = w.shape
    return jnp.moveaxis(w.reshape(*lead, k, n // tile, tile), -2, -3)


def _ffn(h, x, wg, wu, wd, *, tm):
    t, d = x.shape
    nf, _, tf = wg.shape
    nn, dff, tn = wd.shape
    up_map = lambda i, s: (jnp.minimum(s, nf - 1), 0, 0)
    out_map = lambda i, s: (i, jnp.maximum(s - nf, 0))
    return pl.pallas_call(
        _ffn_kernel,
        out_shape=jax.ShapeDtypeStruct((t, d), F32),
        grid=(t // tm, nf + nn),
        in_specs=[
            pl.BlockSpec((tm, d), lambda i, s: (i, 0)),
            pl.BlockSpec((tm, tn), out_map),
            pl.BlockSpec((None, d, tf), up_map),
            pl.BlockSpec((None, d, tf), up_map),
            pl.BlockSpec((None, dff, tn), lambda i, s: (jnp.maximum(s - nf, 0), 0, 0)),
        ],
        out_specs=pl.BlockSpec((tm, tn), out_map),
        scratch_shapes=[pltpu.VMEM((nf, tm, tf), BF16)],
        compiler_params=_cparams(("parallel", "arbitrary")),
        name="dense_swiglu",
    )(h, x, wg, wu, wd)


def _pack_bf16_pairs(lo, hi):
    lo_bits = lax.bitcast_convert_type(lo.astype(BF16).astype(F32), U32)
    hi_bits = lax.bitcast_convert_type(hi.astype(BF16).astype(F32), U32)
    return (lo_bits >> 16) | hi_bits


def _unpack_bf16_pairs(words):
    lo = lax.bitcast_convert_type(words << 16, F32).astype(BF16)
    hi = lax.bitcast_convert_type(words & jnp.uint32(0xFFFF0000), F32).astype(BF16)
    return lo, hi


def _pool_tail_kernel(u_ref, up_ref, un_ref, wgrp_ref, scale_ref, wout_ref, x_ref, g_ref,
                      rw_ref, rb_ref, xo_ref, hp_ref, route_ref, *, seq, n_experts):
    tm, d = x_ref.shape
    gd = d // len(POOL_WINDOWS)
    ext = tm + 2 * POOL_HALO
    start = (pl.program_id(0) * tm) % seq
    pos = start + lax.broadcasted_iota(I32, (tm, 1), 0)

    ys = []
    for gi, win in enumerate(POOL_WINDOWS):
        cs = slice(gi * gd, (gi + 1) * gd)
        u = u_ref[:, cs]
        up = jnp.where(start > 0, up_ref[:, cs], 0.0)
        un = jnp.where(start + tm < seq, un_ref[:, cs], 0.0)
        a = jnp.concatenate([up, u, un], axis=0)
        a = a + pltpu.roll(a, 1, axis=0)
        w = 2
        while w < win:
            a = pltpu.roll(a, w // 2, axis=0) + pltpu.roll(a, ext - w // 2, axis=0)
            w *= 2
        half = win // 2
        count = (jnp.minimum(pos + half, seq) - jnp.maximum(pos - half, 0)).astype(F32)
        diff = a[POOL_HALO:POOL_HALO + tm] / count - u
        y = jnp.dot(diff.astype(BF16), wgrp_ref[gi], preferred_element_type=F32)
        ys.append((y * scale_ref[:, cs]).astype(BF16))
    y = jnp.concatenate(ys, axis=1)
    xo = x_ref[...] + jnp.dot(y, wout_ref[...], preferred_element_type=F32)
    xo_ref[...] = xo

    h = _rms(xo, g_ref[...])
    half_d = d // 2
    words = _pack_bf16_pairs(h[:, :half_d], h[:, half_d:])
    n_chunks = half_d // LANES
    for c in range(n_chunks):
        hp_ref[pl.ds(c, tm, stride=n_chunks), :] = words[:, c * LANES:(c + 1) * LANES]

    h_hi = h.astype(BF16)
    h_lo = (h - h_hi.astype(F32)).astype(BF16)
    prod = jnp.dot(jnp.concatenate([h_hi, h_lo], axis=0), rw_ref[...],
                   preferred_element_type=F32)
    part = prod[:tm] + prod[tm:]
    logits = part + pltpu.roll(part, LANES - n_experts, axis=1) + rb_ref[...]
    lane = lax.broadcasted_iota(I32, logits.shape, 1)
    logits = jnp.where(lane < n_experts, logits, -jnp.inf)
    lane_f = lane.astype(F32)
    m1 = jnp.max(logits, axis=-1, keepdims=True)
    i1 = jnp.min(jnp.where(logits == m1, lane_f, float(LANES)), axis=-1, keepdims=True)
    rest = jnp.where(lane_f == i1, -jnp.inf, logits)
    m2 = jnp.max(rest, axis=-1, keepdims=True)
    i2 = jnp.min(jnp.where(rest == m2, lane_f, float(LANES)), axis=-1, keepdims=True)
    e2 = jnp.exp(m2 - m1)
    w1 = 1.0 / (1.0 + e2)
    w2 = e2 / (1.0 + e2)
    route = jnp.where(lane == 0, i1,
                      jnp.where(lane == 1, i2,
                                jnp.where(lane == 2, w1, jnp.where(lane == 3, w2, 0.0))))
    route_ref[...] = route


def _pool_tail(u, wgrp, scale, wout, x, gain, rw, rb, *, seq, n_experts, tm):
    t, d = x.shape
    ng, gd, _ = wgrp.shape
    n_chunks = d // 2 // LANES
    halo_blocks = tm // POOL_HALO
    last_halo = t // POOL_HALO - 1
    return pl.pallas_call(
        functools.partial(_pool_tail_kernel, seq=seq, n_experts=n_experts),
        out_shape=(jax.ShapeDtypeStruct((t, d), F32),
                   jax.ShapeDtypeStruct((t * n_chunks, LANES), U32),
                   jax.ShapeDtypeStruct((t, LANES), F32)),
        grid=(t // tm,),
        in_specs=[
            pl.BlockSpec((tm, d), lambda i: (i, 0)),
            pl.BlockSpec((POOL_HALO, d), lambda i: (jnp.maximum(i * halo_blocks - 1, 0), 0)),
            pl.BlockSpec((POOL_HALO, d),
                         lambda i: (jnp.minimum((i + 1) * halo_blocks, last_halo), 0)),
            pl.BlockSpec((ng, gd, gd), lambda i: (0, 0, 0)),
            pl.BlockSpec((1, d), lambda i: (0, 0)),
            pl.BlockSpec((d, d), lambda i: (0, 0)),
            pl.BlockSpec((tm, d), lambda i: (i, 0)),
            pl.BlockSpec((1, d), lambda i: (0, 0)),
            pl.BlockSpec((d, LANES), lambda i: (0, 0)),
            pl.BlockSpec((1, LANES), lambda i: (0, 0)),
        ],
        out_specs=(pl.BlockSpec((tm, d), lambda i: (i, 0)),
                   pl.BlockSpec((tm * n_chunks, LANES), lambda i: (i, 0)),
                   pl.BlockSpec((tm, LANES), lambda i: (i, 0))),
        compiler_params=_cparams(("parallel",)),
        name="pool_tail",
    )(u, u, u, wgrp, scale.reshape(1, d), wout, x, gain.reshape(1, d), rw, rb)


def _routing_plan(e_flat, *, n_experts, tm, n_tiles):
    onehot = (e_flat[:, None] == jnp.arange(n_experts, dtype=I32)[None, :]).astype(I32)
    csum = jnp.cumsum(onehot, axis=0)
    rank = jnp.sum(onehot * csum, axis=1) - 1
    counts = csum[-1]
    tiles_per = (counts + tm - 1) // tm
    padded = tiles_per * tm
    offsets = jnp.cumsum(padded) - padded
    pos = jnp.sum(onehot * offsets[None, :], axis=1) + rank
    tile_end = jnp.cumsum(tiles_per)
    n_used = tile_end[-1]
    tile_ids = jnp.arange(n_tiles, dtype=I32)
    tile_expert = jnp.sum((tile_ids[:, None] >= tile_end[None, :]).astype(I32), axis=1)
    last_expert = jnp.sum((n_used - 1 >= tile_end).astype(I32))
    tile_expert = jnp.minimum(tile_expert, last_expert).astype(I32)
    return pos.astype(I32), counts.astype(I32), offsets.astype(I32), tile_expert, n_used.astype(I32)


def _dispatch_kernel(pos_ref, cnt_ref, off_ref, src_ref, dst_ref, sem, *, rpt, tm, n_experts):
    i = pl.program_id(0)
    tc = src_ref.shape[0] // rpt
    unroll = 8

    def row_copy(src_row, dst_row):
        return pltpu.make_async_copy(
            src_ref.at[pl.ds(pl.multiple_of(src_row * rpt, rpt), rpt)],
            dst_ref.at[pl.ds(pl.multiple_of(dst_row * rpt, rpt), rpt)], sem)

    def issue(jo, carry):
        for ji in range(unroll):
            j = jo * unroll + ji
            for k in range(TOP_K):
                row_copy(j, pos_ref[0, 0, TOP_K * j + k]).start()
        return carry

    lax.fori_loop(0, tc // unroll, issue, 0)
    for k in range(TOP_K):
        pltpu.make_async_copy(src_ref, dst_ref.at[pl.ds(0, tc * rpt)], sem).wait()

    @pl.when(i == pl.num_programs(0) - 1)
    def _():
        for e in range(n_experts):
            cnt = cnt_ref[e]
            n_pad = (-cnt) % tm
            first = off_ref[e] + cnt

            def fill(j, carry):
                row_copy(0, first + j).start()
                return carry

            def fill_wait(j, carry):
                row_copy(0, 0).wait()
                return carry

            lax.fori_loop(0, n_pad, fill, 0)
            lax.fori_loop(0, n_pad, fill_wait, 0)

        n_used = sum((cnt_ref[e] + tm - 1) // tm for e in range(n_experts))
        fill_tokens = math.gcd(tm, tc)
        block_rows = fill_tokens * rpt

        def fill_block(j, carry):
            cp = pltpu.make_async_copy(
                src_ref.at[pl.ds(0, block_rows)],
                dst_ref.at[pl.ds(pl.multiple_of(j * block_rows, block_rows), block_rows)], sem)
            cp.start()
            cp.wait()
            return carry

        lax.fori_loop(n_used * (tm // fill_tokens), dst_ref.shape[0] // block_rows, fill_block, 0)


def _dispatch(pos, counts, offsets, packed, *, rows_per_token, n_rows, tc, tm):
    t = packed.shape[0] // rows_per_token
    n_steps = t // tc
    return pl.pallas_call(
        functools.partial(_dispatch_kernel, rpt=rows_per_token, tm=tm,
                          n_experts=counts.shape[0]),
        out_shape=jax.ShapeDtypeStruct((n_rows * rows_per_token, LANES), packed.dtype),
        grid=(n_steps,),
        in_specs=[
            pl.BlockSpec((1, 1, TOP_K * tc), lambda i: (i, 0, 0), memory_space=pltpu.SMEM),
            pl.BlockSpec(memory_space=pltpu.SMEM),
            pl.BlockSpec(memory_space=pltpu.SMEM),
            pl.BlockSpec((tc * rows_per_token, LANES), lambda i: (i, 0)),
        ],
        out_specs=pl.BlockSpec(memory_space=pl.ANY),
        scratch_shapes=[pltpu.SemaphoreType.DMA(())],
        compiler_params=_cparams(("arbitrary",)),
        name="moe_dispatch",
    )(pos.reshape(n_steps, 1, TOP_K * tc), counts, offsets, packed)


def _moe_kernel(te_ref, nu_ref, xs_ref, wg_ref, wu_ref, wd_ref, o_ref, x_scr, a_scr, *,
                in_chunks, out_chunks):
    del te_ref
    i = pl.program_id(0)
    s = pl.program_id(1)
    tm, d = x_scr.shape
    nf = a_scr.shape[0]
    tn = wd_ref.shape[1]
    nn = d // tn
    used = i < nu_ref[0]

    @pl.when(used & (s == 0))
    def _():
        for c in range(in_chunks):
            lo, hi = _unpack_bf16_pairs(xs_ref[pl.ds(c, tm, stride=in_chunks), :])
            x_scr[:, c * LANES:(c + 1) * LANES] = lo
            x_scr[:, d // 2 + c * LANES:d // 2 + (c + 1) * LANES] = hi

    @pl.when(used & (s < nf))
    def _():
        a_scr[s] = _swiglu_act(x_scr[...], wg_ref[...], wu_ref[...])

    def store_columns(n, y):
        for c in range(tn // LANES):
            chunk = n * (tn // LANES) + c
            o_ref[pl.ds(chunk, tm, stride=out_chunks), :] = y[:, c * LANES:(c + 1) * LANES]

    for n in range(nn):
        @pl.when(used & (s == nf + n))
        def _():
            store_columns(n, _down_proj(a_scr, wd_ref[...]))

        @pl.when(jnp.logical_not(used) & (s == nf + n))
        def _():
            store_columns(n, jnp.zeros((tm, tn), F32))


def _moe(tile_expert, n_used, xs, wg, wu, wd, *, tm, n_tiles):
    _, nf, d, tf = wg.shape
    _, nn, dff, tn = wd.shape
    in_chunks = d // 2 // LANES
    out_chunks = d // LANES

    def row_map(i, s, te, nu):
        return (jnp.minimum(i, nu[0] - 1), 0)

    def up_map(i, s, te, nu):
        return (te[i], jnp.where(i < nu[0], jnp.minimum(s, nf - 1), nf - 1), 0, 0)

    def down_map(i, s, te, nu):
        return (te[i], jnp.where(i < nu[0], jnp.maximum(s - nf, 0), nn - 1), 0, 0)

    grid_spec = pltpu.PrefetchScalarGridSpec(
        num_scalar_prefetch=2,
        grid=(n_tiles, nf + nn),
        in_specs=[
            pl.BlockSpec((tm * in_chunks, LANES), row_map),
            pl.BlockSpec((None, None, d, tf), up_map),
            pl.BlockSpec((None, None, d, tf), up_map),
            pl.BlockSpec((None, None, dff, tn), down_map),
        ],
        out_specs=pl.BlockSpec((tm * out_chunks, LANES), lambda i, s, te, nu: (i, 0)),
        scratch_shapes=[pltpu.VMEM((tm, d), BF16), pltpu.VMEM((nf, tm, tf), BF16)],
    )
    return pl.pallas_call(
        functools.partial(_moe_kernel, in_chunks=in_chunks, out_chunks=out_chunks),
        out_shape=jax.ShapeDtypeStruct((n_tiles * tm * out_chunks, LANES), F32),
        grid_spec=grid_spec,
        compiler_params=_cparams(("arbitrary", "arbitrary")),
        name="moe_grouped_swiglu",
    )(tile_expert, n_used.reshape(1), xs, wg, wu, wd)


def _combine_kernel(pos_ref, ys_ref, x_ref, route_ref, o_ref, buf_a, buf_b, sem, *, rpt):
    tc = x_ref.shape[0]

    def row_copy(src_row, dst_row, buf):
        return pltpu.make_async_copy(
            ys_ref.at[pl.ds(pl.multiple_of(src_row * rpt, rpt), rpt)],
            buf.at[pl.ds(pl.multiple_of(dst_row * rpt, rpt), rpt)], sem)

    unroll = 8

    def issue(jo, carry):
        for ji in range(unroll):
            j = jo * unroll + ji
            row_copy(pos_ref[0, 0, TOP_K * j], j, buf_a).start()
            row_copy(pos_ref[0, 0, TOP_K * j + 1], j, buf_b).start()
        return carry

    lax.fori_loop(0, tc // unroll, issue, 0)
    for buf in (buf_a, buf_b):
        pltpu.make_async_copy(ys_ref.at[pl.ds(0, tc * rpt)], buf, sem).wait()

    w1 = route_ref[:, 2:3]
    w2 = route_ref[:, 3:4]
    for c in range(rpt):
        sl = slice(c * LANES, (c + 1) * LANES)
        o_ref[:, sl] = (x_ref[:, sl] + w1 * buf_a[pl.ds(c, tc, stride=rpt), :]
                        + w2 * buf_b[pl.ds(c, tc, stride=rpt), :])


def _combine(pos, ys, x, route, *, tc):
    t, d = x.shape
    rpt = d // LANES
    n_steps = t // tc
    return pl.pallas_call(
        functools.partial(_combine_kernel, rpt=rpt),
        out_shape=jax.ShapeDtypeStruct((t, d), F32),
        grid=(n_steps,),
        in_specs=[
            pl.BlockSpec((1, 1, TOP_K * tc), lambda i: (i, 0, 0), memory_space=pltpu.SMEM),
            pl.BlockSpec(memory_space=pl.ANY),
            pl.BlockSpec((tc, d), lambda i: (i, 0)),
            pl.BlockSpec((tc, LANES), lambda i: (i, 0)),
        ],
        out_specs=pl.BlockSpec((tc, d), lambda i: (i, 0)),
        scratch_shapes=[pltpu.VMEM((tc * rpt, LANES), F32), pltpu.VMEM((tc * rpt, LANES), F32),
                        pltpu.SemaphoreType.DMA(())],
        compiler_params=_cparams(("arbitrary",)),
        name="moe_combine",
    )(pos.reshape(n_steps, 1, TOP_K * tc), ys, x, route)


def _tile(n, want):
    want = min(want, n)
    while n % want:
        want //= 2
    return want


def _attention_layer(x, seq, mix_gain, ffn_gain, rel_bias, w_qkv, q_gain, k_gain, sink, w_o,
                     w_gate, w_up, w_down):
    t, d = x.shape
    n_heads = sink.shape[0]
    n_kv_heads = (w_qkv.shape[1] // HEAD_DIM - n_heads) // 2
    head_gain = jnp.concatenate([
        jnp.tile(q_gain * HEAD_DIM ** -0.5, n_heads),
        jnp.tile(k_gain, n_kv_heads),
        jnp.ones((n_kv_heads * HEAD_DIM,), F32)]).reshape(1, -1)
    qkv = _norm_matmul(x, (mix_gain[:, None] * w_qkv).astype(BF16), head_gain,
                       normed_heads=n_heads + n_kv_heads, out_dtype=BF16,
                       tm=_tile(t, 512), chunk=_tile(w_qkv.shape[1], 1024))
    attn = _attention(qkv, _band_bias(rel_bias), sink, seq=seq, n_heads=n_heads,
                      n_kv_heads=n_kv_heads, tq=_tile(seq, 512))
    x, h = _proj_res_norm(attn, w_o.astype(BF16), x, ffn_gain, tm=_tile(t, 512))
    tf = _tile(w_gate.shape[1], 512)
    return _ffn(h, x, _column_tiles(w_gate.astype(BF16), tf), _column_tiles(w_up.astype(BF16), tf),
                _column_tiles(w_down.astype(BF16), _tile(d, 512)), tm=_tile(t, 1024))


def _pool_moe_layer(x, seq, mix_gain, ffn_gain, w_in, w_group, scale, w_out, router_w, router_b,
                    w_gate, w_up, w_down):
    t, d = x.shape
    n_experts = router_w.shape[1]
    u = _norm_matmul(x, (mix_gain[:, None] * w_in).astype(BF16), jnp.ones((1, d), F32),
                     normed_heads=0, out_dtype=F32, tm=_tile(t, 512), chunk=_tile(d, 1024))
    rw_hi = router_w.astype(BF16)
    rw_lo = (router_w - rw_hi.astype(F32)).astype(BF16)
    rw = (jnp.zeros((d, LANES), BF16).at[:, :n_experts].set(rw_hi)
          .at[:, n_experts:2 * n_experts].set(rw_lo))
    rb = jnp.zeros((1, LANES), F32).at[0, :n_experts].set(router_b)
    x, packed, route = _pool_tail(u, w_group.astype(BF16), scale, w_out.astype(BF16), x, ffn_gain,
                                  rw, rb, seq=seq, n_experts=n_experts, tm=_tile(seq, 256))
    tm = MOE_ROW_TILE
    n_tiles = TOP_K * t // tm + n_experts
    e_flat = route[:, :TOP_K].astype(I32).reshape(-1)
    pos, counts, offsets, tile_expert, n_used = _routing_plan(
        e_flat, n_experts=n_experts, tm=tm, n_tiles=n_tiles)
    xs = _dispatch(pos, counts, offsets, packed, rows_per_token=d // 2 // LANES,
                   n_rows=n_tiles * tm, tc=_tile(t, 512), tm=tm)
    tf = _tile(w_gate.shape[2], 512)
    ys = _moe(tile_expert, n_used, xs, _column_tiles(w_gate.astype(BF16), tf),
              _column_tiles(w_up.astype(BF16), tf),
              _column_tiles(w_down.astype(BF16), _tile(d, 256)), tm=tm, n_tiles=n_tiles)
    return _combine(pos, ys, x, route, tc=_tile(t, 256))


def kernel(x, mix_norm, ffn_norm, rel_bias, attn_w_qkv, attn_q_gain, attn_k_gain, attn_sink,
           attn_w_o, ffn_w_gate, ffn_w_up, ffn_w_down, pool_w_in, pool_w_group, pool_scale,
           pool_w_out, moe_router_w, moe_router_b, moe_w_gate, moe_w_up, moe_w_down):
    b, s, d = x.shape
    y = x.reshape(b * s, d)
    for i in range(mix_norm.shape[0]):
        j = i // 2
        if i % 2 == 0:
            y = _attention_layer(y, s, mix_norm[i], ffn_norm[i], rel_bias, attn_w_qkv[j],
                                 attn_q_gain[j], attn_k_gain[j], attn_sink[j], attn_w_o[j],
                                 ffn_w_gate[j], ffn_w_up[j], ffn_w_down[j])
        else:
            y = _pool_moe_layer(y, s, mix_norm[i], ffn_norm[i], pool_w_in[j], pool_w_group[j],
                                pool_scale[j], pool_w_out[j], moe_router_w[j], moe_router_b[j],
                                moe_w_gate[j], moe_w_up[j], moe_w_down[j])
    return y.reshape(b, s, d)
```

```python
import functools
import math

import jax
import jax.numpy as jnp
from jax import lax
from jax.experimental import pallas as pl
from jax.experimental.pallas import tpu as pltpu

F32 = jnp.float32
BF16 = jnp.bfloat16
U32 = jnp.uint32
I32 = jnp.int32

EPS = 1e-6
NEG_INF = -1e30

LANES = 128
VMEM_LIMIT_BYTES = 56 * 1024 * 1024

HEAD_DIM = 128
GROUP = 4
WINDOW = 128
BLOCK = 128
KEY_SPAN = BLOCK + 2 * WINDOW
NUM_BUCKETS = 32
MAX_DISTANCE = 128
POOL_WINDOWS = (2, 4, 8, 16)
POOL_HALO = 8
TOP_K = 2
MOE_ROW_TILE = 512


def _cparams(semantics):
    return pltpu.CompilerParams(dimension_semantics=semantics,
                                vmem_limit_bytes=VMEM_LIMIT_BYTES)


def _rms(x, gain):
    ms = jnp.mean(x * x, axis=-1, keepdims=True)
    return x * lax.rsqrt(ms + EPS) * gain


def _tile(n, want):
    want = min(want, n)
    while n % want:
        want //= 2
    return want


def _norm_matmul_kernel(x_ref, w_ref, hg_ref, o_ref, *, normed_heads, chunk):
    x = x_ref[...]
    inv = lax.rsqrt(jnp.mean(x * x, axis=-1, keepdims=True) + EPS)
    xb = x.astype(BF16)
    for c0 in range(0, o_ref.shape[1], chunk):
        acc = jnp.dot(xb, w_ref[:, c0:c0 + chunk], preferred_element_type=F32) * inv
        for h0 in range(c0, c0 + chunk, HEAD_DIM):
            sl = slice(h0, h0 + HEAD_DIM)
            piece = acc[:, h0 - c0:h0 - c0 + HEAD_DIM]
            if h0 // HEAD_DIM < normed_heads:
                piece = _rms(piece, hg_ref[:, sl])
            o_ref[:, sl] = piece.astype(o_ref.dtype)


def _norm_matmul(x, w, head_gain, *, normed_heads, out_dtype, tm, chunk):
    t, d = x.shape
    n = w.shape[1]
    return pl.pallas_call(
        functools.partial(_norm_matmul_kernel, normed_heads=normed_heads, chunk=chunk),
        out_shape=jax.ShapeDtypeStruct((t, n), out_dtype),
        grid=(t // tm,),
        in_specs=[
            pl.BlockSpec((tm, d), lambda i: (i, 0)),
            pl.BlockSpec((d, n), lambda i: (0, 0)),
            pl.BlockSpec((1, n), lambda i: (0, 0)),
        ],
        out_specs=pl.BlockSpec((tm, n), lambda i: (i, 0)),
        compiler_params=_cparams(("parallel",)),
        name="norm_matmul",
    )(x, w, head_gain)


def _attn_kernel(sink_ref, q_ref, kv_ref, kvp_ref, kvn_ref, bias_ref, o_ref, *,
                 n_kv_heads, steps_per_seq):
    i = pl.program_id(0) % steps_per_seq
    tq = q_ref.shape[0]
    nqb = tq // BLOCK
    kw = n_kv_heads * HEAD_DIM
    prev_penalty = jnp.where(i > 0, 0.0, NEG_INF)
    next_penalty = jnp.where(i < steps_per_seq - 1, 0.0, NEG_INF)
    col = lax.broadcasted_iota(I32, (GROUP * BLOCK, KEY_SPAN), 1)

    for h in range(n_kv_heads):
        ksl = slice(h * HEAD_DIM, (h + 1) * HEAD_DIM)
        vsl = slice(kw + h * HEAD_DIM, kw + (h + 1) * HEAD_DIM)
        bias = bias_ref[h * GROUP:(h + 1) * GROUP].reshape(GROUP * BLOCK, KEY_SPAN)
        for qb in range(nqb):
            rows = slice(qb * BLOCK, (qb + 1) * BLOCK)
            prev_rows = slice((qb - 1) * BLOCK, qb * BLOCK)
            next_rows = slice((qb + 1) * BLOCK, (qb + 2) * BLOCK)
            if qb == 0:
                k_prev, v_prev = kvp_ref[:, ksl], kvp_ref[:, vsl]
            else:
                k_prev, v_prev = kv_ref[prev_rows, ksl], kv_ref[prev_rows, vsl]
            if qb == nqb - 1:
                k_next, v_next = kvn_ref[:, ksl], kvn_ref[:, vsl]
            else:
                k_next, v_next = kv_ref[next_rows, ksl], kv_ref[next_rows, vsl]
            k3 = jnp.concatenate([k_prev, kv_ref[rows, ksl], k_next], axis=0)
            v3 = jnp.concatenate([v_prev, kv_ref[rows, vsl], v_next], axis=0)
            q4 = jnp.concatenate(
                [q_ref[rows, (h * GROUP + g) * HEAD_DIM:(h * GROUP + g + 1) * HEAD_DIM]
                 for g in range(GROUP)], axis=0)
            s = lax.dot_general(q4, k3, (((1,), (1,)), ((), ())),
                                preferred_element_type=F32) + bias
            if qb == 0:
                s = s + jnp.where(col < WINDOW, prev_penalty, 0.0)
            if qb == nqb - 1:
                s = s + jnp.where(col >= WINDOW + BLOCK, next_penalty, 0.0)
            outs = []
            for g in range(GROUP):
                sg = s[g * BLOCK:(g + 1) * BLOCK]
                sink = sink_ref[h * GROUP + g]
                m = jnp.maximum(jnp.max(sg, axis=-1, keepdims=True), sink)
                p = jnp.exp(sg - m)
                denom = jnp.sum(p, axis=-1, keepdims=True) + jnp.exp(sink - m)
                pv = jnp.dot(p.astype(BF16), v3, preferred_element_type=F32)
                outs.append(pv / denom)
            for g in range(GROUP):
                o_ref[rows, (h * GROUP + g) * HEAD_DIM:(h * GROUP + g + 1) * HEAD_DIM] = (
                    outs[g].astype(o_ref.dtype))


def _attention(qkv, bias, sink, *, seq, n_heads, n_kv_heads, tq):
    t = qkv.shape[0]
    dq = n_heads * HEAD_DIM
    dkv = 2 * n_kv_heads * HEAD_DIM
    assert dq % dkv == 0
    kv_col = dq // dkv
    steps_per_seq = seq // tq
    blocks_per_step = tq // BLOCK
    last_block = t // BLOCK - 1
    return pl.pallas_call(
        functools.partial(_attn_kernel, n_kv_heads=n_kv_heads, steps_per_seq=steps_per_seq),
        out_shape=jax.ShapeDtypeStruct((t, dq), BF16),
        grid=(t // tq,),
        in_specs=[
            pl.BlockSpec(memory_space=pltpu.SMEM),
            pl.BlockSpec((tq, dq), lambda r: (r, 0)),
            pl.BlockSpec((tq, dkv), lambda r: (r, kv_col)),
            pl.BlockSpec((BLOCK, dkv),
                         lambda r: (jnp.maximum(r * blocks_per_step - 1, 0), kv_col)),
            pl.BlockSpec((BLOCK, dkv),
                         lambda r: (jnp.minimum((r + 1) * blocks_per_step, last_block), kv_col)),
            pl.BlockSpec((n_heads, BLOCK, KEY_SPAN), lambda r: (0, 0, 0)),
        ],
        out_specs=pl.BlockSpec((tq, dq), lambda r: (r, 0)),
        compiler_params=_cparams(("parallel",)),
        name="window_attention",
    )(sink, qkv, qkv, qkv, qkv, bias)


def _t5_bucket(rel):
    half = NUM_BUCKETS // 2
    ret = jnp.where(rel > 0, half, 0)
    n = jnp.abs(rel)
    max_exact = half // 2
    nf = jnp.maximum(n, 1).astype(F32)
    large = max_exact + (jnp.log(nf / max_exact) / math.log(MAX_DISTANCE / max_exact)
                         * (half - max_exact)).astype(I32)
    large = jnp.minimum(large, half - 1)
    return ret + jnp.where(n < max_exact, n, large)


def _band_bias(rel_bias):
    qi = jnp.arange(BLOCK)[:, None]
    kj = jnp.arange(KEY_SPAN)[None, :]
    rel = kj - WINDOW - qi
    onehot = (_t5_bucket(rel)[:, :, None] == jnp.arange(NUM_BUCKETS)[None, None, :]).astype(F32)
    bias = jnp.einsum("qkb,bh->hqk", onehot, rel_bias.astype(F32),
                      precision=lax.Precision.HIGHEST)
    return jnp.where((jnp.abs(rel) <= WINDOW)[None], bias, NEG_INF)


def _proj_res_norm_kernel(a_ref, w_ref, x_ref, g_ref, xo_ref, ho_ref):
    xo = x_ref[...] + jnp.dot(a_ref[...], w_ref[...], preferred_element_type=F32)
    xo_ref[...] = xo
    ho_ref[...] = _rms(xo, g_ref[...]).astype(BF16)


def _proj_res_norm(a, w, x, gain, *, tm):
    t, d = x.shape
    k = a.shape[1]
    return pl.pallas_call(
        _proj_res_norm_kernel,
        out_shape=(jax.ShapeDtypeStruct((t, d), F32), jax.ShapeDtypeStruct((t, d), BF16)),
        grid=(t // tm,),
        in_specs=[
            pl.BlockSpec((tm, k), lambda i: (i, 0)),
            pl.BlockSpec((k, d), lambda i: (0, 0)),
            pl.BlockSpec((tm, d), lambda i: (i, 0)),
            pl.BlockSpec((1, d), lambda i: (0, 0)),
        ],
        out_specs=(pl.BlockSpec((tm, d), lambda i: (i, 0)),
                   pl.BlockSpec((tm, d), lambda i: (i, 0))),
        compiler_params=_cparams(("parallel",)),
        name="proj_res_norm",
    )(a, w, x, gain.reshape(1, d))


def _swiglu_act(h, wg, wu):
    g = jnp.dot(h, wg, preferred_element_type=F32)
    u = jnp.dot(h, wu, preferred_element_type=F32)
    return (g * jax.nn.sigmoid(g) * u).astype(BF16)


def _down_proj(a_scr, wd):
    a = jnp.concatenate([a_scr[k] for k in range(a_scr.shape[0])], axis=1)
    return jnp.dot(a, wd, preferred_element_type=F32)


def _ffn_kernel(h_ref, x_ref, wg_ref, wu_ref, wd_ref, o_ref, a_scr):
    s = pl.program_id(1)
    nf = a_scr.shape[0]

    @pl.when(s < nf)
    def _():
        a_scr[s] = _swiglu_act(h_ref[...], wg_ref[...], wu_ref[...])

    @pl.when(s >= nf)
    def _():
        o_ref[...] = x_ref[...] + _down_proj(a_scr, wd_ref[...])


def _ffn(h, x, wg, wu, wd, *, tm, tf, tn):
    t, d = x.shape
    dff = wg.shape[1]
    nf = dff // tf
    nn = d // tn
    up_map = lambda i, s: (0, jnp.minimum(s, nf - 1))
    out_map = lambda i, s: (i, jnp.maximum(s - nf, 0))
    return pl.pallas_call(
        _ffn_kernel,
        out_shape=jax.ShapeDtypeStruct((t, d), F32),
        grid=(t // tm, nf + nn),
        in_specs=[
            pl.BlockSpec((tm, d), lambda i, s: (i, 0)),
            pl.BlockSpec((tm, tn), out_map),
            pl.BlockSpec((d, tf), up_map),
            pl.BlockSpec((d, tf), up_map),
            pl.BlockSpec((dff, tn), lambda i, s: (0, jnp.maximum(s - nf, 0))),
        ],
        out_specs=pl.BlockSpec((tm, tn), out_map),
        scratch_shapes=[pltpu.VMEM((nf, tm, tf), BF16)],
        compiler_params=_cparams(("parallel", "arbitrary")),
        name="dense_swiglu",
    )(h, x, wg, wu, wd)


def _pack_bf16_pairs(lo, hi):
    lo_bits = lax.bitcast_convert_type(lo.astype(BF16).astype(F32), U32)
    hi_bits = lax.bitcast_convert_type(hi.astype(BF16).astype(F32), U32)
    return (lo_bits >> 16) | hi_bits


def _unpack_bf16_pairs(words):
    lo = lax.bitcast_convert_type(words << 16, F32).astype(BF16)
    hi = lax.bitcast_convert_type(words & jnp.uint32(0xFFFF0000), F32).astype(BF16)
    return lo, hi


def _pool_tail_kernel(u_ref, up_ref, un_ref, wgrp_ref, scale_ref, wout_ref, x_ref, g_ref,
                      rw_ref, rb_ref, xo_ref, hp_ref, route_ref, *, seq, n_experts):
    tm, d = x_ref.shape
    gd = d // len(POOL_WINDOWS)
    ext = tm + 2 * POOL_HALO
    start = (pl.program_id(0) * tm) % seq
    pos = start + lax.broadcasted_iota(I32, (tm, 1), 0)

    ys = []
    for gi, win in enumerate(POOL_WINDOWS):
        cs = slice(gi * gd, (gi + 1) * gd)
        u = u_ref[:, cs]
        up = jnp.where(start > 0, up_ref[:, cs], 0.0)
        un = jnp.where(start + tm < seq, un_ref[:, cs], 0.0)
        a = jnp.concatenate([up, u, un], axis=0)
        a = a + pltpu.roll(a, 1, axis=0)
        w = 2
        while w < win:
            a = pltpu.roll(a, w // 2, axis=0) + pltpu.roll(a, ext - w // 2, axis=0)
            w *= 2
        half = win // 2
        count = (jnp.minimum(pos + half, seq) - jnp.maximum(pos - half, 0)).astype(F32)
        diff = a[POOL_HALO:POOL_HALO + tm] / count - u
        y = jnp.dot(diff.astype(BF16), wgrp_ref[gi], preferred_element_type=F32)
        ys.append((y * scale_ref[:, cs]).astype(BF16))
    y = jnp.concatenate(ys, axis=1)
    xo = x_ref[...] + jnp.dot(y, wout_ref[...], preferred_element_type=F32)
    xo_ref[...] = xo

    h = _rms(xo, g_ref[...])
    hp_ref[...] = _pack_bf16_pairs(h[:, :d // 2], h[:, d // 2:])

    h_hi = h.astype(BF16)
    h_lo = (h - h_hi.astype(F32)).astype(BF16)
    prod = jnp.dot(jnp.concatenate([h_hi, h_lo], axis=0), rw_ref[...],
                   preferred_element_type=F32)
    part = prod[:tm] + prod[tm:]
    logits = part + pltpu.roll(part, LANES - n_experts, axis=1) + rb_ref[...]
    lane = lax.broadcasted_iota(I32, logits.shape, 1)
    logits = jnp.where(lane < n_experts, logits, -jnp.inf)
    lane_f = lane.astype(F32)
    m1 = jnp.max(logits, axis=-1, keepdims=True)
    i1 = jnp.min(jnp.where(logits == m1, lane_f, float(LANES)), axis=-1, keepdims=True)
    rest = jnp.where(lane_f == i1, -jnp.inf, logits)
    m2 = jnp.max(rest, axis=-1, keepdims=True)
    i2 = jnp.min(jnp.where(rest == m2, lane_f, float(LANES)), axis=-1, keepdims=True)
    e2 = jnp.exp(m2 - m1)
    w1 = 1.0 / (1.0 + e2)
    w2 = e2 / (1.0 + e2)
    route = jnp.where(lane == 0, i1,
                      jnp.where(lane == 1, i2,
                                jnp.where(lane == 2, w1, jnp.where(lane == 3, w2, 0.0))))
    route_ref[...] = route


def _pool_tail(u, wgrp, scale, wout, x, gain, rw, rb, *, seq, n_experts, tm):
    t, d = x.shape
    ng, gd, _ = wgrp.shape
    halo_blocks = tm // POOL_HALO
    last_halo = t // POOL_HALO - 1
    return pl.pallas_call(
        functools.partial(_pool_tail_kernel, seq=seq, n_experts=n_experts),
        out_shape=(jax.ShapeDtypeStruct((t, d), F32),
                   jax.ShapeDtypeStruct((t, d // 2), U32),
                   jax.ShapeDtypeStruct((t, LANES), F32)),
        grid=(t // tm,),
        in_specs=[
            pl.BlockSpec((tm, d), lambda i: (i, 0)),
            pl.BlockSpec((POOL_HALO, d), lambda i: (jnp.maximum(i * halo_blocks - 1, 0), 0)),
            pl.BlockSpec((POOL_HALO, d),
                         lambda i: (jnp.minimum((i + 1) * halo_blocks, last_halo), 0)),
            pl.BlockSpec((ng, gd, gd), lambda i: (0, 0, 0)),
            pl.BlockSpec((1, d), lambda i: (0, 0)),
            pl.BlockSpec((d, d), lambda i: (0, 0)),
            pl.BlockSpec((tm, d), lambda i: (i, 0)),
            pl.BlockSpec((1, d), lambda i: (0, 0)),
            pl.BlockSpec((d, LANES), lambda i: (0, 0)),
            pl.BlockSpec((1, LANES), lambda i: (0, 0)),
        ],
        out_specs=(pl.BlockSpec((tm, d), lambda i: (i, 0)),
                   pl.BlockSpec((tm, d // 2), lambda i: (i, 0)),
                   pl.BlockSpec((tm, LANES), lambda i: (i, 0))),
        compiler_params=_cparams(("parallel",)),
        name="pool_tail",
    )(u, u, u, wgrp, scale.reshape(1, d), wout, x, gain.reshape(1, d), rw, rb)


def _routing_plan(e_flat, *, n_experts, tm, n_tiles):
    onehot = (e_flat[:, None] == jnp.arange(n_experts, dtype=I32)[None, :]).astype(I32)
    csum = jnp.cumsum(onehot, axis=0)
    rank = jnp.sum(onehot * csum, axis=1) - 1
    counts = csum[-1]
    tiles_per = (counts + tm - 1) // tm
    padded = tiles_per * tm
    offsets = jnp.cumsum(padded) - padded
    pos = jnp.sum(onehot * offsets[None, :], axis=1) + rank
    tile_end = jnp.cumsum(tiles_per)
    n_used = tile_end[-1]
    tile_ids = jnp.arange(n_tiles, dtype=I32)
    tile_expert = jnp.sum((tile_ids[:, None] >= tile_end[None, :]).astype(I32), axis=1)
    last_expert = jnp.sum((n_used - 1 >= tile_end).astype(I32))
    tile_expert = jnp.minimum(tile_expert, last_expert).astype(I32)
    return pos.astype(I32), counts.astype(I32), offsets.astype(I32), tile_expert, n_used.astype(I32)


def _dispatch_kernel(pos_ref, cnt_ref, off_ref, src_ref, dst_ref, sem, *, tm, n_experts):
    i = pl.program_id(0)
    tc = src_ref.shape[0]
    unroll = 8

    def row_copy(src_row, dst_row):
        return pltpu.make_async_copy(src_ref.at[pl.ds(src_row, 1), :],
                                     dst_ref.at[pl.ds(dst_row, 1), :], sem)

    def issue(jo, carry):
        for ji in range(unroll):
            j = jo * unroll + ji
            for k in range(TOP_K):
                row_copy(j, pos_ref[0, 0, TOP_K * j + k]).start()
        return carry

    lax.fori_loop(0, tc // unroll, issue, 0)
    for k in range(TOP_K):
        pltpu.make_async_copy(src_ref, dst_ref.at[pl.ds(0, tc), :], sem).wait()

    @pl.when(i == pl.num_programs(0) - 1)
    def _():
        for e in range(n_experts):
            cnt = cnt_ref[e]
            n_pad = (-cnt) % tm
            first = off_ref[e] + cnt

            def fill(j, carry):
                row_copy(0, first + j).start()
                return carry

            def fill_wait(j, carry):
                row_copy(0, 0).wait()
                return carry

            lax.fori_loop(0, n_pad, fill, 0)
            lax.fori_loop(0, n_pad, fill_wait, 0)

        n_used = sum((cnt_ref[e] + tm - 1) // tm for e in range(n_experts))
        fill_rows = math.gcd(tm, tc)

        def fill_block(j, carry):
            cp = pltpu.make_async_copy(
                src_ref.at[pl.ds(0, fill_rows), :],
                dst_ref.at[pl.ds(pl.multiple_of(j * fill_rows, fill_rows), fill_rows), :], sem)
            cp.start()
            cp.wait()
            return carry

        lax.fori_loop(n_used * (tm // fill_rows), dst_ref.shape[0] // fill_rows, fill_block, 0)


def _dispatch(pos, counts, offsets, packed, *, n_rows, tc, tm):
    t, width = packed.shape
    n_steps = t // tc
    return pl.pallas_call(
        functools.partial(_dispatch_kernel, tm=tm, n_experts=counts.shape[0]),
        out_shape=jax.ShapeDtypeStruct((n_rows, width), packed.dtype),
        grid=(n_steps,),
        in_specs=[
            pl.BlockSpec((1, 1, TOP_K * tc), lambda i: (i, 0, 0), memory_space=pltpu.SMEM),
            pl.BlockSpec(memory_space=pltpu.SMEM),
            pl.BlockSpec(memory_space=pltpu.SMEM),
            pl.BlockSpec((tc, width), lambda i: (i, 0)),
        ],
        out_specs=pl.BlockSpec(memory_space=pl.ANY),
        scratch_shapes=[pltpu.SemaphoreType.DMA(())],
        compiler_params=_cparams(("arbitrary",)),
        name="moe_dispatch",
    )(pos.reshape(n_steps, 1, TOP_K * tc), counts, offsets, packed)


def _moe_kernel(te_ref, nu_ref, xs_ref, wg_ref, wu_ref, wd_ref, o_ref, x_scr, a_scr):
    del te_ref
    i = pl.program_id(0)
    s = pl.program_id(1)
    d = x_scr.shape[1]
    nf = a_scr.shape[0]
    used = i < nu_ref[0]

    @pl.when(used & (s == 0))
    def _():
        lo, hi = _unpack_bf16_pairs(xs_ref[...])
        x_scr[:, :d // 2] = lo
        x_scr[:, d // 2:] = hi

    @pl.when(used & (s < nf))
    def _():
        a_scr[s] = _swiglu_act(x_scr[...], wg_ref[...], wu_ref[...])

    @pl.when(used & (s >= nf))
    def _():
        o_ref[...] = _down_proj(a_scr, wd_ref[...])

    @pl.when(jnp.logical_not(used) & (s >= nf))
    def _():
        o_ref[...] = jnp.zeros(o_ref.shape, o_ref.dtype)


def _moe(tile_expert, n_used, xs, wg, wu, wd, *, tm, tf, tn, n_tiles):
    n_experts, d, dff = wg.shape
    nf = dff // tf
    nn = d // tn

    def row_map(i, s, te, nu):
        return (jnp.minimum(i, nu[0] - 1), 0)

    def up_map(i, s, te, nu):
        return (te[i], 0, jnp.where(i < nu[0], jnp.minimum(s, nf - 1), nf - 1))

    def down_map(i, s, te, nu):
        return (te[i], 0, jnp.where(i < nu[0], jnp.maximum(s - nf, 0), nn - 1))

    grid_spec = pltpu.PrefetchScalarGridSpec(
        num_scalar_prefetch=2,
        grid=(n_tiles, nf + nn),
        in_specs=[
            pl.BlockSpec((tm, d // 2), row_map),
            pl.BlockSpec((None, d, tf), up_map),
            pl.BlockSpec((None, d, tf), up_map),
            pl.BlockSpec((None, dff, tn), down_map),
        ],
        out_specs=pl.BlockSpec((tm, tn), lambda i, s, te, nu: (i, jnp.maximum(s - nf, 0))),
        scratch_shapes=[pltpu.VMEM((tm, d), BF16), pltpu.VMEM((nf, tm, tf), BF16)],
    )
    return pl.pallas_call(
        _moe_kernel,
        out_shape=jax.ShapeDtypeStruct((n_tiles * tm, d), F32),
        grid_spec=grid_spec,
        compiler_params=_cparams(("arbitrary", "arbitrary")),
        name="moe_grouped_swiglu",
    )(tile_expert, n_used.reshape(1), xs, wg, wu, wd)


def _combine_kernel(pos_ref, ys_ref, x_ref, route_ref, o_ref, buf_a, buf_b, sem):
    tc = x_ref.shape[0]
    unroll = 8

    def row_copy(src_row, dst_row, buf):
        return pltpu.make_async_copy(ys_ref.at[pl.ds(src_row, 1), :],
                                     buf.at[pl.ds(dst_row, 1), :], sem)

    def issue(jo, carry):
        for ji in range(unroll):
            j = jo * unroll + ji
            row_copy(pos_ref[0, 0, TOP_K * j], j, buf_a).start()
            row_copy(pos_ref[0, 0, TOP_K * j + 1], j, buf_b).start()
        return carry

    lax.fori_loop(0, tc // unroll, issue, 0)
    for buf in (buf_a, buf_b):
        pltpu.make_async_copy(ys_ref.at[pl.ds(0, tc), :], buf, sem).wait()

    w1 = route_ref[:, 2:3]
    w2 = route_ref[:, 3:4]
    o_ref[...] = x_ref[...] + w1 * buf_a[...] + w2 * buf_b[...]


def _combine(pos, ys, x, route, *, tc):
    t, d = x.shape
    n_steps = t // tc
    return pl.pallas_call(
        _combine_kernel,
        out_shape=jax.ShapeDtypeStruct((t, d), F32),
        grid=(n_steps,),
        in_specs=[
            pl.BlockSpec((1, 1, TOP_K * tc), lambda i: (i, 0, 0), memory_space=pltpu.SMEM),
            pl.BlockSpec(memory_space=pl.ANY),
            pl.BlockSpec((tc, d), lambda i: (i, 0)),
            pl.BlockSpec((tc, LANES), lambda i: (i, 0)),
        ],
        out_specs=pl.BlockSpec((tc, d), lambda i: (i, 0)),
        scratch_shapes=[pltpu.VMEM((tc, d), F32), pltpu.VMEM((tc, d), F32),
                        pltpu.SemaphoreType.DMA(())],
        compiler_params=_cparams(("arbitrary",)),
        name="moe_combine",
    )(pos.reshape(n_steps, 1, TOP_K * tc), ys, x, route)


def _attention_layer(x, seq, mix_gain, ffn_gain, rel_bias, w_qkv, q_gain, k_gain, sink, w_o,
                     w_gate, w_up, w_down):
    t, d = x.shape
    n_heads = sink.shape[0]
    n_kv_heads = (w_qkv.shape[1] // HEAD_DIM - n_heads) // 2
    head_gain = jnp.concatenate([
        jnp.tile(q_gain * HEAD_DIM ** -0.5, n_heads),
        jnp.tile(k_gain, n_kv_heads),
        jnp.ones((n_kv_heads * HEAD_DIM,), F32)]).reshape(1, -1)
    qkv = _norm_matmul(x, (mix_gain[:, None] * w_qkv).astype(BF16), head_gain,
                       normed_heads=n_heads + n_kv_heads, out_dtype=BF16,
                       tm=_tile(t, 512), chunk=_tile(w_qkv.shape[1], 1024))
    attn = _attention(qkv, _band_bias(rel_bias), sink, seq=seq, n_heads=n_heads,
                      n_kv_heads=n_kv_heads, tq=_tile(seq, 512))
    x, h = _proj_res_norm(attn, w_o.astype(BF16), x, ffn_gain, tm=_tile(t, 512))
    return _ffn(h, x, w_gate.astype(BF16), w_up.astype(BF16), w_down.astype(BF16),
                tm=_tile(t, 1024), tf=_tile(w_gate.shape[1], 512), tn=_tile(d, 512))


def _pool_moe_layer(x, seq, mix_gain, ffn_gain, w_in, w_group, scale, w_out, router_w, router_b,
                    w_gate, w_up, w_down):
    t, d = x.shape
    n_experts = router_w.shape[1]
    u = _norm_matmul(x, (mix_gain[:, None] * w_in).astype(BF16), jnp.ones((1, d), F32),
                     normed_heads=0, out_dtype=F32, tm=_tile(t, 512), chunk=_tile(d, 1024))
    rw_hi = router_w.astype(BF16)
    rw_lo = (router_w - rw_hi.astype(F32)).astype(BF16)
    rw = (jnp.zeros((d, LANES), BF16).at[:, :n_experts].set(rw_hi)
          .at[:, n_experts:2 * n_experts].set(rw_lo))
    rb = jnp.zeros((1, LANES), F32).at[0, :n_experts].set(router_b)
    x, packed, route = _pool_tail(u, w_group.astype(BF16), scale, w_out.astype(BF16), x, ffn_gain,
                                  rw, rb, seq=seq, n_experts=n_experts, tm=_tile(seq, 256))
    tm = MOE_ROW_TILE
    n_tiles = TOP_K * t // tm + n_experts
    e_flat = route[:, :TOP_K].astype(I32).reshape(-1)
    pos, counts, offsets, tile_expert, n_used = _routing_plan(
        e_flat, n_experts=n_experts, tm=tm, n_tiles=n_tiles)
    xs = _dispatch(pos, counts, offsets, packed, n_rows=n_tiles * tm, tc=_tile(t, 512), tm=tm)
    ys = _moe(tile_expert, n_used, xs, w_gate.astype(BF16), w_up.astype(BF16),
              w_down.astype(BF16), tm=tm, tf=_tile(w_gate.shape[2], 1024), tn=_tile(d, 512),
              n_tiles=n_tiles)
    return _combine(pos, ys, x, route, tc=_tile(t, 256))


def kernel(x, mix_norm, ffn_norm, rel_bias, attn_w_qkv, attn_q_gain, attn_k_gain, attn_sink,
           attn_w_o, ffn_w_gate, ffn_w_up, ffn_w_down, pool_w_in, pool_w_group, pool_scale,
           pool_w_out, moe_router_w, moe_router_b, moe_w_gate, moe_w_up, moe_w_down):
    b, s, d = x.shape
    y = x.reshape(b * s, d)
    for i in range(mix_norm.shape[0]):
        j = i // 2
        if i % 2 == 0:
            y = _attention_layer(y, s, mix_norm[i], ffn_norm[i], rel_bias, attn_w_qkv[j],
                                 attn_q_gain[j], attn_k_gain[j], attn_sink[j], attn_w_o[j],
                                 ffn_w_gate[j], ffn_w_up[j], ffn_w_down[j])
        else:
            y = _pool_moe_layer(y, s, mix_norm[i], ffn_norm[i], pool_w_in[j], pool_w_group[j],
                                pool_scale[j], pool_w_out[j], moe_router_w[j], moe_router_b[j],
                                moe_w_gate[j], moe_w_up[j], moe_w_down[j])
    return y.reshape(b, s, d)
```

```python
import functools
import math

import jax
import jax.numpy as jnp
from jax import lax
from jax.experimental import pallas as pl
from jax.experimental.pallas import tpu as pltpu

F32 = jnp.float32
BF16 = jnp.bfloat16
U32 = jnp.uint32
I32 = jnp.int32

EPS = 1e-6
NEG_INF = -1e30

LANES = 128
VMEM_LIMIT_BYTES = 56 * 1024 * 1024

HEAD_DIM = 128
GROUP = 4
WINDOW = 128
BLOCK = 128
KEY_SPAN = BLOCK + 2 * WINDOW
NUM_BUCKETS = 32
MAX_DISTANCE = 128
POOL_WINDOWS = (2, 4, 8, 16)
POOL_HALO = 8
TOP_K = 2
MOE_ROW_TILE = 1024


def _cparams(semantics):
    return pltpu.CompilerParams(dimension_semantics=semantics,
                                vmem_limit_bytes=VMEM_LIMIT_BYTES)


def _rms(x, gain):
    ms = jnp.mean(x * x, axis=-1, keepdims=True)
    return x * lax.rsqrt(ms + EPS) * gain


def _tile(n, want):
    want = min(want, n)
    while n % want:
        want //= 2
    return want


def _norm_matmul_kernel(x_ref, w_ref, hg_ref, o_ref, *, normed_heads, chunk):
    x = x_ref[...]
    inv = lax.rsqrt(jnp.mean(x * x, axis=-1, keepdims=True) + EPS)
    xb = x.astype(BF16)
    for c0 in range(0, o_ref.shape[1], chunk):
        acc = jnp.dot(xb, w_ref[:, c0:c0 + chunk], preferred_element_type=F32) * inv
        for h0 in range(c0, c0 + chunk, HEAD_DIM):
            sl = slice(h0, h0 + HEAD_DIM)
            piece = acc[:, h0 - c0:h0 - c0 + HEAD_DIM]
            if h0 // HEAD_DIM < normed_heads:
                piece = _rms(piece, hg_ref[:, sl])
            o_ref[:, sl] = piece.astype(o_ref.dtype)


def _norm_matmul(x, w, head_gain, *, normed_heads, out_dtype, tm, chunk):
    t, d = x.shape
    n = w.shape[1]
    return pl.pallas_call(
        functools.partial(_norm_matmul_kernel, normed_heads=normed_heads, chunk=chunk),
        out_shape=jax.ShapeDtypeStruct((t, n), out_dtype),
        grid=(t // tm,),
        in_specs=[
            pl.BlockSpec((tm, d), lambda i: (i, 0)),
            pl.BlockSpec((d, n), lambda i: (0, 0)),
            pl.BlockSpec((1, n), lambda i: (0, 0)),
        ],
        out_specs=pl.BlockSpec((tm, n), lambda i: (i, 0)),
        compiler_params=_cparams(("parallel",)),
        name="norm_matmul",
    )(x, w, head_gain)


def _attn_kernel(sink_ref, q_ref, kv_ref, kvp_ref, kvn_ref, bias_ref, o_ref, *,
                 n_kv_heads, steps_per_seq):
    i = pl.program_id(0) % steps_per_seq
    tq = q_ref.shape[0]
    nqb = tq // BLOCK
    kw = n_kv_heads * HEAD_DIM
    prev_penalty = jnp.where(i > 0, 0.0, NEG_INF)
    next_penalty = jnp.where(i < steps_per_seq - 1, 0.0, NEG_INF)
    col = lax.broadcasted_iota(I32, (GROUP * BLOCK, KEY_SPAN), 1)

    for h in range(n_kv_heads):
        ksl = slice(h * HEAD_DIM, (h + 1) * HEAD_DIM)
        vsl = slice(kw + h * HEAD_DIM, kw + (h + 1) * HEAD_DIM)
        bias = bias_ref[h * GROUP:(h + 1) * GROUP].reshape(GROUP * BLOCK, KEY_SPAN)
        for qb in range(nqb):
            rows = slice(qb * BLOCK, (qb + 1) * BLOCK)
            prev_rows = slice((qb - 1) * BLOCK, qb * BLOCK)
            next_rows = slice((qb + 1) * BLOCK, (qb + 2) * BLOCK)
            if qb == 0:
                k_prev, v_prev = kvp_ref[:, ksl], kvp_ref[:, vsl]
            else:
                k_prev, v_prev = kv_ref[prev_rows, ksl], kv_ref[prev_rows, vsl]
            if qb == nqb - 1:
                k_next, v_next = kvn_ref[:, ksl], kvn_ref[:, vsl]
            else:
                k_next, v_next = kv_ref[next_rows, ksl], kv_ref[next_rows, vsl]
            k3 = jnp.concatenate([k_prev, kv_ref[rows, ksl], k_next], axis=0)
            v3 = jnp.concatenate([v_prev, kv_ref[rows, vsl], v_next], axis=0)
            q4 = jnp.concatenate(
                [q_ref[rows, (h * GROUP + g) * HEAD_DIM:(h * GROUP + g + 1) * HEAD_DIM]
                 for g in range(GROUP)], axis=0)
            s = lax.dot_general(q4, k3, (((1,), (1,)), ((), ())),
                                preferred_element_type=F32) + bias
            if qb == 0:
                s = s + jnp.where(col < WINDOW, prev_penalty, 0.0)
            if qb == nqb - 1:
                s = s + jnp.where(col >= WINDOW + BLOCK, next_penalty, 0.0)
            outs = []
            for g in range(GROUP):
                sg = s[g * BLOCK:(g + 1) * BLOCK]
                sink = sink_ref[h * GROUP + g]
                m = jnp.maximum(jnp.max(sg, axis=-1, keepdims=True), sink)
                p = jnp.exp(sg - m)
                denom = jnp.sum(p, axis=-1, keepdims=True) + jnp.exp(sink - m)
                pv = jnp.dot(p.astype(BF16), v3, preferred_element_type=F32)
                outs.append(pv / denom)
            for g in range(GROUP):
                o_ref[rows, (h * GROUP + g) * HEAD_DIM:(h * GROUP + g + 1) * HEAD_DIM] = (
                    outs[g].astype(o_ref.dtype))


def _attention(qkv, bias, sink, *, seq, n_heads, n_kv_heads, tq):
    t = qkv.shape[0]
    dq = n_heads * HEAD_DIM
    dkv = 2 * n_kv_heads * HEAD_DIM
    assert dq % dkv == 0
    kv_col = dq // dkv
    steps_per_seq = seq // tq
    blocks_per_step = tq // BLOCK
    last_block = t // BLOCK - 1
    return pl.pallas_call(
        functools.partial(_attn_kernel, n_kv_heads=n_kv_heads, steps_per_seq=steps_per_seq),
        out_shape=jax.ShapeDtypeStruct((t, dq), BF16),
        grid=(t // tq,),
        in_specs=[
            pl.BlockSpec(memory_space=pltpu.SMEM),
            pl.BlockSpec((tq, dq), lambda r: (r, 0)),
            pl.BlockSpec((tq, dkv), lambda r: (r, kv_col)),
            pl.BlockSpec((BLOCK, dkv),
                         lambda r: (jnp.maximum(r * blocks_per_step - 1, 0), kv_col)),
            pl.BlockSpec((BLOCK, dkv),
                         lambda r: (jnp.minimum((r + 1) * blocks_per_step, last_block), kv_col)),
            pl.BlockSpec((n_heads, BLOCK, KEY_SPAN), lambda r: (0, 0, 0)),
        ],
        out_specs=pl.BlockSpec((tq, dq), lambda r: (r, 0)),
        compiler_params=_cparams(("parallel",)),
        name="window_attention",
    )(sink, qkv, qkv, qkv, qkv, bias)


def _t5_bucket(rel):
    half = NUM_BUCKETS // 2
    ret = jnp.where(rel > 0, half, 0)
    n = jnp.abs(rel)
    max_exact = half // 2
    nf = jnp.maximum(n, 1).astype(F32)
    large = max_exact + (jnp.log(nf / max_exact) / math.log(MAX_DISTANCE / max_exact)
                         * (half - max_exact)).astype(I32)
    large = jnp.minimum(large, half - 1)
    return ret + jnp.where(n < max_exact, n, large)


def _band_bias(rel_bias):
    qi = jnp.arange(BLOCK)[:, None]
    kj = jnp.arange(KEY_SPAN)[None, :]
    rel = kj - WINDOW - qi
    onehot = (_t5_bucket(rel)[:, :, None] == jnp.arange(NUM_BUCKETS)[None, None, :]).astype(F32)
    bias = jnp.einsum("qkb,bh->hqk", onehot, rel_bias.astype(F32),
                      precision=lax.Precision.HIGHEST)
    return jnp.where((jnp.abs(rel) <= WINDOW)[None], bias, NEG_INF)


def _proj_res_norm_kernel(a_ref, w_ref, x_ref, g_ref, xo_ref, ho_ref):
    xo = x_ref[...] + jnp.dot(a_ref[...], w_ref[...], preferred_element_type=F32)
    xo_ref[...] = xo
    ho_ref[...] = _rms(xo, g_ref[...]).astype(BF16)


def _proj_res_norm(a, w, x, gain, *, tm):
    t, d = x.shape
    k = a.shape[1]
    return pl.pallas_call(
        _proj_res_norm_kernel,
        out_shape=(jax.ShapeDtypeStruct((t, d), F32), jax.ShapeDtypeStruct((t, d), BF16)),
        grid=(t // tm,),
        in_specs=[
            pl.BlockSpec((tm, k), lambda i: (i, 0)),
            pl.BlockSpec((k, d), lambda i: (0, 0)),
            pl.BlockSpec((tm, d), lambda i: (i, 0)),
            pl.BlockSpec((1, d), lambda i: (0, 0)),
        ],
        out_specs=(pl.BlockSpec((tm, d), lambda i: (i, 0)),
                   pl.BlockSpec((tm, d), lambda i: (i, 0))),
        compiler_params=_cparams(("parallel",)),
        name="proj_res_norm",
    )(a, w, x, gain.reshape(1, d))


def _swiglu_act(h, wg, wu):
    g = jnp.dot(h, wg.astype(BF16), preferred_element_type=F32)
    u = jnp.dot(h, wu.astype(BF16), preferred_element_type=F32)
    return (g * jax.nn.sigmoid(g) * u).astype(BF16)


def _down_proj(a_scr, wd):
    a = jnp.concatenate([a_scr[k] for k in range(a_scr.shape[0])], axis=1)
    return jnp.dot(a, wd.astype(BF16), preferred_element_type=F32)


def _ffn_kernel(h_ref, x_ref, wg_ref, wu_ref, wd_ref, o_ref, a_scr):
    s = pl.program_id(1)
    nf = a_scr.shape[0]

    @pl.when(s < nf)
    def _():
        a_scr[s] = _swiglu_act(h_ref[...], wg_ref[...], wu_ref[...])

    @pl.when(s >= nf)
    def _():
        o_ref[...] = x_ref[...] + _down_proj(a_scr, wd_ref[...])


def _ffn(h, x, wg, wu, wd, *, tm, tf, tn):
    t, d = x.shape
    dff = wg.shape[1]
    nf = dff // tf
    nn = d // tn
    up_map = lambda i, s: (0, jnp.minimum(s, nf - 1))
    out_map = lambda i, s: (i, jnp.maximum(s - nf, 0))
    return pl.pallas_call(
        _ffn_kernel,
        out_shape=jax.ShapeDtypeStruct((t, d), F32),
        grid=(t // tm, nf + nn),
        in_specs=[
            pl.BlockSpec((tm, d), lambda i, s: (i, 0)),
            pl.BlockSpec((tm, tn), out_map),
            pl.BlockSpec((d, tf), up_map),
            pl.BlockSpec((d, tf), up_map),
            pl.BlockSpec((dff, tn), lambda i, s: (0, jnp.maximum(s - nf, 0))),
        ],
        out_specs=pl.BlockSpec((tm, tn), out_map),
        scratch_shapes=[pltpu.VMEM((nf, tm, tf), BF16)],
        compiler_params=_cparams(("parallel", "arbitrary")),
        name="dense_swiglu",
    )(h, x, wg, wu, wd)


def _pack_bf16_pairs(lo, hi):
    lo_bits = lax.bitcast_convert_type(lo.astype(BF16).astype(F32), U32)
    hi_bits = lax.bitcast_convert_type(hi.astype(BF16).astype(F32), U32)
    return (lo_bits >> 16) | hi_bits


def _unpack_bf16_pairs(words):
    lo = lax.bitcast_convert_type(words << 16, F32).astype(BF16)
    hi = lax.bitcast_convert_type(words & jnp.uint32(0xFFFF0000), F32).astype(BF16)
    return lo, hi


def _pool_tail_kernel(u_ref, up_ref, un_ref, wgrp_ref, scale_ref, wout_ref, x_ref, g_ref,
                      rw_ref, rb_ref, xo_ref, hp_ref, route_ref, *, seq, n_experts):
    tm, d = x_ref.shape
    gd = d // len(POOL_WINDOWS)
    ext = tm + 2 * POOL_HALO
    start = (pl.program_id(0) * tm) % seq
    pos = start + lax.broadcasted_iota(I32, (tm, 1), 0)

    ys = []
    for gi, win in enumerate(POOL_WINDOWS):
        cs = slice(gi * gd, (gi + 1) * gd)
        u = u_ref[:, cs]
        up = jnp.where(start > 0, up_ref[:, cs], 0.0)
        un = jnp.where(start + tm < seq, un_ref[:, cs], 0.0)
        a = jnp.concatenate([up, u, un], axis=0)
        a = a + pltpu.roll(a, 1, axis=0)
        w = 2
        while w < win:
            a = pltpu.roll(a, w // 2, axis=0) + pltpu.roll(a, ext - w // 2, axis=0)
            w *= 2
        half = win // 2
        count = (jnp.minimum(pos + half, seq) - jnp.maximum(pos - half, 0)).astype(F32)
        diff = a[POOL_HALO:POOL_HALO + tm] / count - u
        y = jnp.dot(diff.astype(BF16), wgrp_ref[gi], preferred_element_type=F32)
        ys.append((y * scale_ref[:, cs]).astype(BF16))
    y = jnp.concatenate(ys, axis=1)
    xo = x_ref[...] + jnp.dot(y, wout_ref[...], preferred_element_type=F32)
    xo_ref[...] = xo

    h = _rms(xo, g_ref[...])
    hp_ref[...] = _pack_bf16_pairs(h[:, :d // 2], h[:, d // 2:])

    h_hi = h.astype(BF16)
    h_lo = (h - h_hi.astype(F32)).astype(BF16)
    prod = jnp.dot(jnp.concatenate([h_hi, h_lo], axis=0), rw_ref[...],
                   preferred_element_type=F32)
    part = prod[:tm] + prod[tm:]
    logits = part + pltpu.roll(part, LANES - n_experts, axis=1) + rb_ref[...]
    lane = lax.broadcasted_iota(I32, logits.shape, 1)
    logits = jnp.where(lane < n_experts, logits, -jnp.inf)
    lane_f = lane.astype(F32)
    m1 = jnp.max(logits, axis=-1, keepdims=True)
    i1 = jnp.min(jnp.where(logits == m1, lane_f, float(LANES)), axis=-1, keepdims=True)
    rest = jnp.where(lane_f == i1, -jnp.inf, logits)
    m2 = jnp.max(rest, axis=-1, keepdims=True)
    i2 = jnp.min(jnp.where(rest == m2, lane_f, float(LANES)), axis=-1, keepdims=True)
    e2 = jnp.exp(m2 - m1)
    w1 = 1.0 / (1.0 + e2)
    w2 = e2 / (1.0 + e2)
    route = jnp.where(lane == 0, i1,
                      jnp.where(lane == 1, i2,
                                jnp.where(lane == 2, w1, jnp.where(lane == 3, w2, 0.0))))
    route_ref[...] = route


def _pool_tail(u, wgrp, scale, wout, x, gain, rw, rb, *, seq, n_experts, tm):
    t, d = x.shape
    ng, gd, _ = wgrp.shape
    halo_blocks = tm // POOL_HALO
    last_halo = t // POOL_HALO - 1
    return pl.pallas_call(
        functools.partial(_pool_tail_kernel, seq=seq, n_experts=n_experts),
        out_shape=(jax.ShapeDtypeStruct((t, d), F32),
                   jax.ShapeDtypeStruct((t, d // 2), U32),
                   jax.ShapeDtypeStruct((t, LANES), F32)),
        grid=(t // tm,),
        in_specs=[
            pl.BlockSpec((tm, d), lambda i: (i, 0)),
            pl.BlockSpec((POOL_HALO, d), lambda i: (jnp.maximum(i * halo_blocks - 1, 0), 0)),
            pl.BlockSpec((POOL_HALO, d),
                         lambda i: (jnp.minimum((i + 1) * halo_blocks, last_halo), 0)),
            pl.BlockSpec((ng, gd, gd), lambda i: (0, 0, 0)),
            pl.BlockSpec((1, d), lambda i: (0, 0)),
            pl.BlockSpec((d, d), lambda i: (0, 0)),
            pl.BlockSpec((tm, d), lambda i: (i, 0)),
            pl.BlockSpec((1, d), lambda i: (0, 0)),
            pl.BlockSpec((d, LANES), lambda i: (0, 0)),
            pl.BlockSpec((1, LANES), lambda i: (0, 0)),
        ],
        out_specs=(pl.BlockSpec((tm, d), lambda i: (i, 0)),
                   pl.BlockSpec((tm, d // 2), lambda i: (i, 0)),
                   pl.BlockSpec((tm, LANES), lambda i: (i, 0))),
        compiler_params=_cparams(("parallel",)),
        name="pool_tail",
    )(u, u, u, wgrp, scale.reshape(1, d), wout, x, gain.reshape(1, d), rw, rb)


def _routing_plan(e_flat, *, n_experts, tm, n_tiles):
    onehot = (e_flat[:, None] == jnp.arange(n_experts, dtype=I32)[None, :]).astype(I32)
    csum = jnp.cumsum(onehot, axis=0)
    rank = jnp.sum(onehot * csum, axis=1) - 1
    counts = csum[-1]
    tiles_per = (counts + tm - 1) // tm
    padded = tiles_per * tm
    offsets = jnp.cumsum(padded) - padded
    pos = jnp.sum(onehot * offsets[None, :], axis=1) + rank
    tile_end = jnp.cumsum(tiles_per)
    n_used = tile_end[-1]
    tile_ids = jnp.arange(n_tiles, dtype=I32)
    tile_expert = jnp.sum((tile_ids[:, None] >= tile_end[None, :]).astype(I32), axis=1)
    last_expert = jnp.sum((n_used - 1 >= tile_end).astype(I32))
    tile_expert = jnp.minimum(tile_expert, last_expert).astype(I32)
    return pos.astype(I32), counts.astype(I32), offsets.astype(I32), tile_expert, n_used.astype(I32)


def _dispatch_kernel(pos_ref, cnt_ref, off_ref, src_ref, dst_ref, sem, *, tm, n_experts):
    i = pl.program_id(0)
    tc = src_ref.shape[0]
    unroll = 8

    def row_copy(src_row, dst_row):
        return pltpu.make_async_copy(src_ref.at[pl.ds(src_row, 1), :],
                                     dst_ref.at[pl.ds(dst_row, 1), :], sem)

    def issue(jo, carry):
        for ji in range(unroll):
            j = jo * unroll + ji
            for k in range(TOP_K):
                row_copy(j, pos_ref[0, 0, TOP_K * j + k]).start()
        return carry

    lax.fori_loop(0, tc // unroll, issue, 0)
    for k in range(TOP_K):
        pltpu.make_async_copy(src_ref, dst_ref.at[pl.ds(0, tc), :], sem).wait()

    @pl.when(i == pl.num_programs(0) - 1)
    def _():
        for e in range(n_experts):
            cnt = cnt_ref[e]
            n_pad = (-cnt) % tm
            first = off_ref[e] + cnt

            def fill(j, carry):
                row_copy(0, first + j).start()
                return carry

            def fill_wait(j, carry):
                row_copy(0, 0).wait()
                return carry

            lax.fori_loop(0, n_pad, fill, 0)
            lax.fori_loop(0, n_pad, fill_wait, 0)

        n_used = sum((cnt_ref[e] + tm - 1) // tm for e in range(n_experts))
        fill_rows = math.gcd(tm, tc)

        def fill_block(j, carry):
            cp = pltpu.make_async_copy(
                src_ref.at[pl.ds(0, fill_rows), :],
                dst_ref.at[pl.ds(pl.multiple_of(j * fill_rows, fill_rows), fill_rows), :], sem)
            cp.start()
            cp.wait()
            return carry

        lax.fori_loop(n_used * (tm // fill_rows), dst_ref.shape[0] // fill_rows, fill_block, 0)


def _dispatch(pos, counts, offsets, packed, *, n_rows, tc, tm):
    t, width = packed.shape
    n_steps = t // tc
    return pl.pallas_call(
        functools.partial(_dispatch_kernel, tm=tm, n_experts=counts.shape[0]),
        out_shape=jax.ShapeDtypeStruct((n_rows, width), packed.dtype),
        grid=(n_steps,),
        in_specs=[
            pl.BlockSpec((1, 1, TOP_K * tc), lambda i: (i, 0, 0), memory_space=pltpu.SMEM),
            pl.BlockSpec(memory_space=pltpu.SMEM),
            pl.BlockSpec(memory_space=pltpu.SMEM),
            pl.BlockSpec((tc, width), lambda i: (i, 0)),
        ],
        out_specs=pl.BlockSpec(memory_space=pl.ANY),
        scratch_shapes=[pltpu.SemaphoreType.DMA(())],
        compiler_params=_cparams(("arbitrary",)),
        name="moe_dispatch",
    )(pos.reshape(n_steps, 1, TOP_K * tc), counts, offsets, packed)


def _moe_kernel(te_ref, nu_ref, xs_ref, wg_ref, wu_ref, wd_ref, o_ref, x_scr, a_scr):
    del te_ref
    i = pl.program_id(0)
    s = pl.program_id(1)
    d = x_scr.shape[1]
    nf = a_scr.shape[0]
    used = i < nu_ref[0]

    @pl.when(used & (s == 0))
    def _():
        lo, hi = _unpack_bf16_pairs(xs_ref[...])
        x_scr[:, :d // 2] = lo
        x_scr[:, d // 2:] = hi

    @pl.when(used & (s < nf))
    def _():
        a_scr[s] = _swiglu_act(x_scr[...], wg_ref[...], wu_ref[...])

    @pl.when(used & (s >= nf))
    def _():
        o_ref[...] = _down_proj(a_scr, wd_ref[...])

    @pl.when(jnp.logical_not(used) & (s >= nf))
    def _():
        o_ref[...] = jnp.zeros(o_ref.shape, o_ref.dtype)


def _moe(tile_expert, n_used, xs, wg, wu, wd, *, tm, tf, tn, n_tiles):
    n_experts, d, dff = wg.shape
    nf = dff // tf
    nn = d // tn

    def row_map(i, s, te, nu):
        return (jnp.minimum(i, nu[0] - 1), 0)

    def up_map(i, s, te, nu):
        return (te[i], 0, jnp.where(i < nu[0], jnp.minimum(s, nf - 1), nf - 1))

    def down_map(i, s, te, nu):
        return (te[i], 0, jnp.where(i < nu[0], jnp.maximum(s - nf, 0), nn - 1))

    grid_spec = pltpu.PrefetchScalarGridSpec(
        num_scalar_prefetch=2,
        grid=(n_tiles, nf + nn),
        in_specs=[
            pl.BlockSpec((tm, d // 2), row_map),
            pl.BlockSpec((None, d, tf), up_map),
            pl.BlockSpec((None, d, tf), up_map),
            pl.BlockSpec((None, dff, tn), down_map),
        ],
        out_specs=pl.BlockSpec((tm, tn), lambda i, s, te, nu: (i, jnp.maximum(s - nf, 0))),
        scratch_shapes=[pltpu.VMEM((tm, d), BF16), pltpu.VMEM((nf, tm, tf), BF16)],
    )
    return pl.pallas_call(
        _moe_kernel,
        out_shape=jax.ShapeDtypeStruct((n_tiles * tm, d), F32),
        grid_spec=grid_spec,
        compiler_params=_cparams(("arbitrary", "arbitrary")),
        name="moe_grouped_swiglu",
    )(tile_expert, n_used.reshape(1), xs, wg, wu, wd)


def _combine_kernel(pos_ref, ys_ref, x_ref, route_ref, o_ref, buf_a, buf_b, sem):
    tc = x_ref.shape[0]
    unroll = 8

    def row_copy(src_row, dst_row, buf):
        return pltpu.make_async_copy(ys_ref.at[pl.ds(src_row, 1), :],
                                     buf.at[pl.ds(dst_row, 1), :], sem)

    def issue(jo, carry):
        for ji in range(unroll):
            j = jo * unroll + ji
            row_copy(pos_ref[0, 0, TOP_K * j], j, buf_a).start()
            row_copy(pos_ref[0, 0, TOP_K * j + 1], j, buf_b).start()
        return carry

    lax.fori_loop(0, tc // unroll, issue, 0)
    for buf in (buf_a, buf_b):
        pltpu.make_async_copy(ys_ref.at[pl.ds(0, tc), :], buf, sem).wait()

    w1 = route_ref[:, 2:3]
    w2 = route_ref[:, 3:4]
    o_ref[...] = x_ref[...] + w1 * buf_a[...] + w2 * buf_b[...]


def _combine(pos, ys, x, route, *, tc):
    t, d = x.shape
    n_steps = t // tc
    return pl.pallas_call(
        _combine_kernel,
        out_shape=jax.ShapeDtypeStruct((t, d), F32),
        grid=(n_steps,),
        in_specs=[
            pl.BlockSpec((1, 1, TOP_K * tc), lambda i: (i, 0, 0), memory_space=pltpu.SMEM),
            pl.BlockSpec(memory_space=pl.ANY),
            pl.BlockSpec((tc, d), lambda i: (i, 0)),
            pl.BlockSpec((tc, LANES), lambda i: (i, 0)),
        ],
        out_specs=pl.BlockSpec((tc, d), lambda i: (i, 0)),
        scratch_shapes=[pltpu.VMEM((tc, d), F32), pltpu.VMEM((tc, d), F32),
                        pltpu.SemaphoreType.DMA(())],
        compiler_params=_cparams(("arbitrary",)),
        name="moe_combine",
    )(pos.reshape(n_steps, 1, TOP_K * tc), ys, x, route)


def _attention_layer(x, seq, mix_gain, ffn_gain, rel_bias, w_qkv, q_gain, k_gain, sink, w_o,
                     w_gate, w_up, w_down):
    t, d = x.shape
    n_heads = sink.shape[0]
    n_kv_heads = (w_qkv.shape[1] // HEAD_DIM - n_heads) // 2
    head_gain = jnp.concatenate([
        jnp.tile(q_gain * HEAD_DIM ** -0.5, n_heads),
        jnp.tile(k_gain, n_kv_heads),
        jnp.ones((n_kv_heads * HEAD_DIM,), F32)]).reshape(1, -1)
    qkv = _norm_matmul(x, (mix_gain[:, None] * w_qkv).astype(BF16), head_gain,
                       normed_heads=n_heads + n_kv_heads, out_dtype=BF16,
                       tm=_tile(t, 512), chunk=_tile(w_qkv.shape[1], 1024))
    attn = _attention(qkv, _band_bias(rel_bias), sink, seq=seq, n_heads=n_heads,
                      n_kv_heads=n_kv_heads, tq=_tile(seq, 512))
    x, h = _proj_res_norm(attn, w_o.astype(BF16), x, ffn_gain, tm=_tile(t, 512))
    return _ffn(h, x, w_gate.astype(BF16), w_up.astype(BF16), w_down.astype(BF16),
                tm=_tile(t, 1024), tf=_tile(w_gate.shape[1], 512), tn=_tile(d, 512))


def _pool_moe_layer(x, seq, mix_gain, ffn_gain, w_in, w_group, scale, w_out, router_w, router_b,
                    w_gate, w_up, w_down):
    t, d = x.shape
    n_experts = router_w.shape[1]
    u = _norm_matmul(x, (mix_gain[:, None] * w_in).astype(BF16), jnp.ones((1, d), F32),
                     normed_heads=0, out_dtype=F32, tm=_tile(t, 512), chunk=_tile(d, 1024))
    rw_hi = router_w.astype(BF16)
    rw_lo = (router_w - rw_hi.astype(F32)).astype(BF16)
    rw = (jnp.zeros((d, LANES), BF16).at[:, :n_experts].set(rw_hi)
          .at[:, n_experts:2 * n_experts].set(rw_lo))
    rb = jnp.zeros((1, LANES), F32).at[0, :n_experts].set(router_b)
    x, packed, route = _pool_tail(u, w_group.astype(BF16), scale, w_out.astype(BF16), x, ffn_gain,
                                  rw, rb, seq=seq, n_experts=n_experts, tm=_tile(seq, 256))
    tm = MOE_ROW_TILE
    n_tiles = TOP_K * t // tm + n_experts
    e_flat = route[:, :TOP_K].astype(I32).reshape(-1)
    pos, counts, offsets, tile_expert, n_used = _routing_plan(
        e_flat, n_experts=n_experts, tm=tm, n_tiles=n_tiles)
    xs = _dispatch(pos, counts, offsets, packed, n_rows=n_tiles * tm, tc=_tile(t, 512), tm=tm)
    ys = _moe(tile_expert, n_used, xs, w_gate, w_up, w_down, tm=tm,
              tf=_tile(w_gate.shape[2], 256), tn=_tile(d, 256), n_tiles=n_tiles)
    return _combine(pos, ys, x, route, tc=_tile(t, 256))


def kernel(x, mix_norm, ffn_norm, rel_bias, attn_w_qkv, attn_q_gain, attn_k_gain, attn_sink,
           attn_w_o, ffn_w_gate, ffn_w_up, ffn_w_down, pool_w_in, pool_w_group, pool_scale,
           pool_w_out, moe_router_w, moe_router_b, moe_w_gate, moe_w_up, moe_w_down):
    b, s, d = x.shape
    y = x.reshape(b * s, d)
    for i in range(mix_norm.shape[0]):
        j = i // 2
        if i % 2 == 0:
            y = _attention_layer(y, s, mix_norm[i], ffn_norm[i], rel_bias, attn_w_qkv[j],
                                 attn_q_gain[j], attn_k_gain[j], attn_sink[j], attn_w_o[j],
                                 ffn_w_gate[j], ffn_w_up[j], ffn_w_down[j])
        else:
            y = _pool_moe_layer(y, s, mix_norm[i], ffn_norm[i], pool_w_in[j], pool_w_group[j],
                                pool_scale[j], pool_w_out[j], moe_router_w[j], moe_router_b[j],
                                moe_w_gate[j], moe_w_up[j], moe_w_down[j])
    return y.reshape(b, s, d)
```

```python
import functools
import math

import jax
import jax.numpy as jnp
from jax import lax
from jax.experimental import pallas as pl
from jax.experimental.pallas import tpu as pltpu

F32 = jnp.float32
BF16 = jnp.bfloat16
U32 = jnp.uint32
I32 = jnp.int32

EPS = 1e-6
NEG_INF = -1e30

LANES = 128
VMEM_LIMIT_BYTES = 56 * 1024 * 1024

HEAD_DIM = 128
GROUP = 4
WINDOW = 128
BLOCK = 128
KEY_SPAN = BLOCK + 2 * WINDOW
NUM_BUCKETS = 32
MAX_DISTANCE = 128
POOL_WINDOWS = (2, 4, 8, 16)
POOL_HALO = 8
TOP_K = 2
MOE_ROW_TILE = 1024
MOE_ROW_BLOCKS = 4


def _cparams(semantics):
    return pltpu.CompilerParams(dimension_semantics=semantics,
                                vmem_limit_bytes=VMEM_LIMIT_BYTES)


def _rms(x, gain):
    ms = jnp.mean(x * x, axis=-1, keepdims=True)
    return x * lax.rsqrt(ms + EPS) * gain


def _tile(n, want):
    want = min(want, n)
    while n % want:
        want //= 2
    return want


def _norm_matmul_kernel(x_ref, w_ref, hg_ref, o_ref, *, normed_heads, chunk):
    x = x_ref[...]
    inv = lax.rsqrt(jnp.mean(x * x, axis=-1, keepdims=True) + EPS)
    xb = x.astype(BF16)
    for c0 in range(0, o_ref.shape[1], chunk):
        acc = jnp.dot(xb, w_ref[:, c0:c0 + chunk], preferred_element_type=F32) * inv
        for h0 in range(c0, c0 + chunk, HEAD_DIM):
            sl = slice(h0, h0 + HEAD_DIM)
            piece = acc[:, h0 - c0:h0 - c0 + HEAD_DIM]
            if h0 // HEAD_DIM < normed_heads:
                piece = _rms(piece, hg_ref[:, sl])
            o_ref[:, sl] = piece.astype(o_ref.dtype)


def _norm_matmul(x, w, head_gain, *, normed_heads, out_dtype, tm, chunk):
    t, d = x.shape
    n = w.shape[1]
    return pl.pallas_call(
        functools.partial(_norm_matmul_kernel, normed_heads=normed_heads, chunk=chunk),
        out_shape=jax.ShapeDtypeStruct((t, n), out_dtype),
        grid=(t // tm,),
        in_specs=[
            pl.BlockSpec((tm, d), lambda i: (i, 0)),
            pl.BlockSpec((d, n), lambda i: (0, 0)),
            pl.BlockSpec((1, n), lambda i: (0, 0)),
        ],
        out_specs=pl.BlockSpec((tm, n), lambda i: (i, 0)),
        compiler_params=_cparams(("parallel",)),
        name="norm_matmul",
    )(x, w, head_gain)


def _attn_kernel(sink_ref, q_ref, kv_ref, kvp_ref, kvn_ref, bias_ref, o_ref, *,
                 n_kv_heads, steps_per_seq):
    i = pl.program_id(0) % steps_per_seq
    tq = q_ref.shape[0]
    nqb = tq // BLOCK
    kw = n_kv_heads * HEAD_DIM
    prev_penalty = jnp.where(i > 0, 0.0, NEG_INF)
    next_penalty = jnp.where(i < steps_per_seq - 1, 0.0, NEG_INF)
    col = lax.broadcasted_iota(I32, (GROUP * BLOCK, KEY_SPAN), 1)

    for h in range(n_kv_heads):
        ksl = slice(h * HEAD_DIM, (h + 1) * HEAD_DIM)
        vsl = slice(kw + h * HEAD_DIM, kw + (h + 1) * HEAD_DIM)
        bias = bias_ref[h * GROUP:(h + 1) * GROUP].reshape(GROUP * BLOCK, KEY_SPAN)
        for qb in range(nqb):
            rows = slice(qb * BLOCK, (qb + 1) * BLOCK)
            prev_rows = slice((qb - 1) * BLOCK, qb * BLOCK)
            next_rows = slice((qb + 1) * BLOCK, (qb + 2) * BLOCK)
            if qb == 0:
                k_prev, v_prev = kvp_ref[:, ksl], kvp_ref[:, vsl]
            else:
                k_prev, v_prev = kv_ref[prev_rows, ksl], kv_ref[prev_rows, vsl]
            if qb == nqb - 1:
                k_next, v_next = kvn_ref[:, ksl], kvn_ref[:, vsl]
            else:
                k_next, v_next = kv_ref[next_rows, ksl], kv_ref[next_rows, vsl]
            k3 = jnp.concatenate([k_prev, kv_ref[rows, ksl], k_next], axis=0)
            v3 = jnp.concatenate([v_prev, kv_ref[rows, vsl], v_next], axis=0)
            q4 = jnp.concatenate(
                [q_ref[rows, (h * GROUP + g) * HEAD_DIM:(h * GROUP + g + 1) * HEAD_DIM]
                 for g in range(GROUP)], axis=0)
            s = lax.dot_general(q4, k3, (((1,), (1,)), ((), ())),
                                preferred_element_type=F32) + bias
            if qb == 0:
                s = s + jnp.where(col < WINDOW, prev_penalty, 0.0)
            if qb == nqb - 1:
                s = s + jnp.where(col >= WINDOW + BLOCK, next_penalty, 0.0)
            outs = []
            for g in range(GROUP):
                sg = s[g * BLOCK:(g + 1) * BLOCK]
                sink = sink_ref[h * GROUP + g]
                m = jnp.maximum(jnp.max(sg, axis=-1, keepdims=True), sink)
                p = jnp.exp(sg - m)
                denom = jnp.sum(p, axis=-1, keepdims=True) + jnp.exp(sink - m)
                pv = jnp.dot(p.astype(BF16), v3, preferred_element_type=F32)
                outs.append(pv / denom)
            for g in range(GROUP):
                o_ref[rows, (h * GROUP + g) * HEAD_DIM:(h * GROUP + g + 1) * HEAD_DIM] = (
                    outs[g].astype(o_ref.dtype))


def _attention(qkv, bias, sink, *, seq, n_heads, n_kv_heads, tq):
    t = qkv.shape[0]
    dq = n_heads * HEAD_DIM
    dkv = 2 * n_kv_heads * HEAD_DIM
    assert dq % dkv == 0
    kv_col = dq // dkv
    steps_per_seq = seq // tq
    blocks_per_step = tq // BLOCK
    last_block = t // BLOCK - 1
    return pl.pallas_call(
        functools.partial(_attn_kernel, n_kv_heads=n_kv_heads, steps_per_seq=steps_per_seq),
        out_shape=jax.ShapeDtypeStruct((t, dq), BF16),
        grid=(t // tq,),
        in_specs=[
            pl.BlockSpec(memory_space=pltpu.SMEM),
            pl.BlockSpec((tq, dq), lambda r: (r, 0)),
            pl.BlockSpec((tq, dkv), lambda r: (r, kv_col)),
            pl.BlockSpec((BLOCK, dkv),
                         lambda r: (jnp.maximum(r * blocks_per_step - 1, 0), kv_col)),
            pl.BlockSpec((BLOCK, dkv),
                         lambda r: (jnp.minimum((r + 1) * blocks_per_step, last_block), kv_col)),
            pl.BlockSpec((n_heads, BLOCK, KEY_SPAN), lambda r: (0, 0, 0)),
        ],
        out_specs=pl.BlockSpec((tq, dq), lambda r: (r, 0)),
        compiler_params=_cparams(("parallel",)),
        name="window_attention",
    )(sink, qkv, qkv, qkv, qkv, bias)


def _t5_bucket(rel):
    half = NUM_BUCKETS // 2
    ret = jnp.where(rel > 0, half, 0)
    n = jnp.abs(rel)
    max_exact = half // 2
    nf = jnp.maximum(n, 1).astype(F32)
    large = max_exact + (jnp.log(nf / max_exact) / math.log(MAX_DISTANCE / max_exact)
                         * (half - max_exact)).astype(I32)
    large = jnp.minimum(large, half - 1)
    return ret + jnp.where(n < max_exact, n, large)


def _band_bias(rel_bias):
    qi = jnp.arange(BLOCK)[:, None]
    kj = jnp.arange(KEY_SPAN)[None, :]
    rel = kj - WINDOW - qi
    onehot = (_t5_bucket(rel)[:, :, None] == jnp.arange(NUM_BUCKETS)[None, None, :]).astype(F32)
    bias = jnp.einsum("qkb,bh->hqk", onehot, rel_bias.astype(F32),
                      precision=lax.Precision.HIGHEST)
    return jnp.where((jnp.abs(rel) <= WINDOW)[None], bias, NEG_INF)


def _proj_res_norm_kernel(a_ref, w_ref, x_ref, g_ref, xo_ref, ho_ref):
    xo = x_ref[...] + jnp.dot(a_ref[...], w_ref[...], preferred_element_type=F32)
    xo_ref[...] = xo
    ho_ref[...] = _rms(xo, g_ref[...]).astype(BF16)


def _proj_res_norm(a, w, x, gain, *, tm):
    t, d = x.shape
    k = a.shape[1]
    return pl.pallas_call(
        _proj_res_norm_kernel,
        out_shape=(jax.ShapeDtypeStruct((t, d), F32), jax.ShapeDtypeStruct((t, d), BF16)),
        grid=(t // tm,),
        in_specs=[
            pl.BlockSpec((tm, k), lambda i: (i, 0)),
            pl.BlockSpec((k, d), lambda i: (0, 0)),
            pl.BlockSpec((tm, d), lambda i: (i, 0)),
            pl.BlockSpec((1, d), lambda i: (0, 0)),
        ],
        out_specs=(pl.BlockSpec((tm, d), lambda i: (i, 0)),
                   pl.BlockSpec((tm, d), lambda i: (i, 0))),
        compiler_params=_cparams(("parallel",)),
        name="proj_res_norm",
    )(a, w, x, gain.reshape(1, d))


def _swiglu_act(h, wg, wu):
    g = jnp.dot(h, wg.astype(BF16), preferred_element_type=F32)
    u = jnp.dot(h, wu.astype(BF16), preferred_element_type=F32)
    return (g * jax.nn.sigmoid(g) * u).astype(BF16)


def _down_proj(a_scr, wd, rows):
    a = jnp.concatenate([a_scr[k, :rows] for k in range(a_scr.shape[0])], axis=1)
    return jnp.dot(a, wd.astype(BF16), preferred_element_type=F32)


def _ffn_kernel(h_ref, x_ref, wg_ref, wu_ref, wd_ref, o_ref, a_scr):
    s = pl.program_id(1)
    nf = a_scr.shape[0]

    @pl.when(s < nf)
    def _():
        a_scr[s] = _swiglu_act(h_ref[...], wg_ref[...], wu_ref[...])

    @pl.when(s >= nf)
    def _():
        o_ref[...] = x_ref[...] + _down_proj(a_scr, wd_ref[...], o_ref.shape[0])


def _ffn(h, x, wg, wu, wd, *, tm, tf, tn):
    t, d = x.shape
    dff = wg.shape[1]
    nf = dff // tf
    nn = d // tn
    up_map = lambda i, s: (0, jnp.minimum(s, nf - 1))
    out_map = lambda i, s: (i, jnp.maximum(s - nf, 0))
    return pl.pallas_call(
        _ffn_kernel,
        out_shape=jax.ShapeDtypeStruct((t, d), F32),
        grid=(t // tm, nf + nn),
        in_specs=[
            pl.BlockSpec((tm, d), lambda i, s: (i, 0)),
            pl.BlockSpec((tm, tn), out_map),
            pl.BlockSpec((d, tf), up_map),
            pl.BlockSpec((d, tf), up_map),
            pl.BlockSpec((dff, tn), lambda i, s: (0, jnp.maximum(s - nf, 0))),
        ],
        out_specs=pl.BlockSpec((tm, tn), out_map),
        scratch_shapes=[pltpu.VMEM((nf, tm, tf), BF16)],
        compiler_params=_cparams(("parallel", "arbitrary")),
        name="dense_swiglu",
    )(h, x, wg, wu, wd)


def _pack_bf16_pairs(lo, hi):
    lo_bits = lax.bitcast_convert_type(lo.astype(BF16).astype(F32), U32)
    hi_bits = lax.bitcast_convert_type(hi.astype(BF16).astype(F32), U32)
    return (lo_bits >> 16) | hi_bits


def _unpack_bf16_pairs(words):
    lo = lax.bitcast_convert_type(words << 16, F32).astype(BF16)
    hi = lax.bitcast_convert_type(words & jnp.uint32(0xFFFF0000), F32).astype(BF16)
    return lo, hi


def _pool_tail_kernel(u_ref, up_ref, un_ref, wgrp_ref, scale_ref, wout_ref, x_ref, g_ref,
                      rw_ref, rb_ref, xo_ref, hp_ref, route_ref, *, seq, n_experts):
    tm, d = x_ref.shape
    gd = d // len(POOL_WINDOWS)
    ext = tm + 2 * POOL_HALO
    start = (pl.program_id(0) * tm) % seq
    pos = start + lax.broadcasted_iota(I32, (tm, 1), 0)

    ys = []
    for gi, win in enumerate(POOL_WINDOWS):
        cs = slice(gi * gd, (gi + 1) * gd)
        u = u_ref[:, cs]
        up = jnp.where(start > 0, up_ref[:, cs], 0.0)
        un = jnp.where(start + tm < seq, un_ref[:, cs], 0.0)
        a = jnp.concatenate([up, u, un], axis=0)
        a = a + pltpu.roll(a, 1, axis=0)
        w = 2
        while w < win:
            a = pltpu.roll(a, w // 2, axis=0) + pltpu.roll(a, ext - w // 2, axis=0)
            w *= 2
        half = win // 2
        count = (jnp.minimum(pos + half, seq) - jnp.maximum(pos - half, 0)).astype(F32)
        diff = a[POOL_HALO:POOL_HALO + tm] / count - u
        y = jnp.dot(diff.astype(BF16), wgrp_ref[gi], preferred_element_type=F32)
        ys.append((y * scale_ref[:, cs]).astype(BF16))
    y = jnp.concatenate(ys, axis=1)
    xo = x_ref[...] + jnp.dot(y, wout_ref[...], preferred_element_type=F32)
    xo_ref[...] = xo

    h = _rms(xo, g_ref[...])
    hp_ref[...] = _pack_bf16_pairs(h[:, :d // 2], h[:, d // 2:])

    h_hi = h.astype(BF16)
    h_lo = (h - h_hi.astype(F32)).astype(BF16)
    prod = jnp.dot(jnp.concatenate([h_hi, h_lo], axis=0), rw_ref[...],
                   preferred_element_type=F32)
    part = prod[:tm] + prod[tm:]
    logits = part + pltpu.roll(part, LANES - n_experts, axis=1) + rb_ref[...]
    lane = lax.broadcasted_iota(I32, logits.shape, 1)
    logits = jnp.where(lane < n_experts, logits, -jnp.inf)
    lane_f = lane.astype(F32)
    m1 = jnp.max(logits, axis=-1, keepdims=True)
    i1 = jnp.min(jnp.where(logits == m1, lane_f, float(LANES)), axis=-1, keepdims=True)
    rest = jnp.where(lane_f == i1, -jnp.inf, logits)
    m2 = jnp.max(rest, axis=-1, keepdims=True)
    i2 = jnp.min(jnp.where(rest == m2, lane_f, float(LANES)), axis=-1, keepdims=True)
    e2 = jnp.exp(m2 - m1)
    w1 = 1.0 / (1.0 + e2)
    w2 = e2 / (1.0 + e2)
    route = jnp.where(lane == 0, i1,
                      jnp.where(lane == 1, i2,
                                jnp.where(lane == 2, w1, jnp.where(lane == 3, w2, 0.0))))
    route_ref[...] = route


def _pool_tail(u, wgrp, scale, wout, x, gain, rw, rb, *, seq, n_experts, tm):
    t, d = x.shape
    ng, gd, _ = wgrp.shape
    halo_blocks = tm // POOL_HALO
    last_halo = t // POOL_HALO - 1
    return pl.pallas_call(
        functools.partial(_pool_tail_kernel, seq=seq, n_experts=n_experts),
        out_shape=(jax.ShapeDtypeStruct((t, d), F32),
                   jax.ShapeDtypeStruct((t, d // 2), U32),
                   jax.ShapeDtypeStruct((t, LANES), F32)),
        grid=(t // tm,),
        in_specs=[
            pl.BlockSpec((tm, d), lambda i: (i, 0)),
            pl.BlockSpec((POOL_HALO, d), lambda i: (jnp.maximum(i * halo_blocks - 1, 0), 0)),
            pl.BlockSpec((POOL_HALO, d),
                         lambda i: (jnp.minimum((i + 1) * halo_blocks, last_halo), 0)),
            pl.BlockSpec((ng, gd, gd), lambda i: (0, 0, 0)),
            pl.BlockSpec((1, d), lambda i: (0, 0)),
            pl.BlockSpec((d, d), lambda i: (0, 0)),
            pl.BlockSpec((tm, d), lambda i: (i, 0)),
            pl.BlockSpec((1, d), lambda i: (0, 0)),
            pl.BlockSpec((d, LANES), lambda i: (0, 0)),
            pl.BlockSpec((1, LANES), lambda i: (0, 0)),
        ],
        out_specs=(pl.BlockSpec((tm, d), lambda i: (i, 0)),
                   pl.BlockSpec((tm, d // 2), lambda i: (i, 0)),
                   pl.BlockSpec((tm, LANES), lambda i: (i, 0))),
        compiler_params=_cparams(("parallel",)),
        name="pool_tail",
    )(u, u, u, wgrp, scale.reshape(1, d), wout, x, gain.reshape(1, d), rw, rb)


def _routing_plan(e_flat, *, n_experts, tm, n_tiles):
    onehot = (e_flat[:, None] == jnp.arange(n_experts, dtype=I32)[None, :]).astype(I32)
    csum = jnp.cumsum(onehot, axis=0)
    rank = jnp.sum(onehot * csum, axis=1) - 1
    counts = csum[-1]
    tiles_per = (counts + tm - 1) // tm
    padded = tiles_per * tm
    offsets = jnp.cumsum(padded) - padded
    pos = jnp.sum(onehot * offsets[None, :], axis=1) + rank
    tile_end = jnp.cumsum(tiles_per)
    n_used = tile_end[-1]
    tile_ids = jnp.arange(n_tiles, dtype=I32)
    tile_expert = jnp.sum((tile_ids[:, None] >= tile_end[None, :]).astype(I32), axis=1)
    last_expert = jnp.sum((n_used - 1 >= tile_end).astype(I32))
    tile_expert = jnp.minimum(tile_expert, last_expert).astype(I32)
    onehot_t = (tile_expert[:, None] == jnp.arange(n_experts, dtype=I32)[None, :]).astype(I32)
    first_tile = jnp.sum(onehot_t * (tile_end - tiles_per)[None, :], axis=1)
    left = jnp.sum(onehot_t * counts[None, :], axis=1) - (tile_ids - first_tile) * tm
    tile_rows = jnp.where(tile_ids < n_used, jnp.clip(left, 0, tm), 0).astype(I32)
    return (pos.astype(I32), counts.astype(I32), offsets.astype(I32), tile_expert,
            n_used.astype(I32), tile_rows)


def _dispatch_kernel(pos_ref, cnt_ref, off_ref, src_ref, dst_ref, sem, *, tm, n_experts):
    i = pl.program_id(0)
    tc = src_ref.shape[0]
    unroll = 8

    def row_copy(src_row, dst_row):
        return pltpu.make_async_copy(src_ref.at[pl.ds(src_row, 1), :],
                                     dst_ref.at[pl.ds(dst_row, 1), :], sem)

    def issue(jo, carry):
        for ji in range(unroll):
            j = jo * unroll + ji
            for k in range(TOP_K):
                row_copy(j, pos_ref[0, 0, TOP_K * j + k]).start()
        return carry

    lax.fori_loop(0, tc // unroll, issue, 0)
    for k in range(TOP_K):
        pltpu.make_async_copy(src_ref, dst_ref.at[pl.ds(0, tc), :], sem).wait()

    @pl.when(i == pl.num_programs(0) - 1)
    def _():
        for e in range(n_experts):
            cnt = cnt_ref[e]
            n_pad = (-cnt) % tm
            first = off_ref[e] + cnt

            def fill(j, carry):
                row_copy(0, first + j).start()
                return carry

            def fill_wait(j, carry):
                row_copy(0, 0).wait()
                return carry

            lax.fori_loop(0, n_pad, fill, 0)
            lax.fori_loop(0, n_pad, fill_wait, 0)

        n_used = sum((cnt_ref[e] + tm - 1) // tm for e in range(n_experts))
        fill_rows = math.gcd(tm, tc)

        def fill_block(j, carry):
            cp = pltpu.make_async_copy(
                src_ref.at[pl.ds(0, fill_rows), :],
                dst_ref.at[pl.ds(pl.multiple_of(j * fill_rows, fill_rows), fill_rows), :], sem)
            cp.start()
            cp.wait()
            return carry

        lax.fori_loop(n_used * (tm // fill_rows), dst_ref.shape[0] // fill_rows, fill_block, 0)


def _dispatch(pos, counts, offsets, packed, *, n_rows, tc, tm):
    t, width = packed.shape
    n_steps = t // tc
    return pl.pallas_call(
        functools.partial(_dispatch_kernel, tm=tm, n_experts=counts.shape[0]),
        out_shape=jax.ShapeDtypeStruct((n_rows, width), packed.dtype),
        grid=(n_steps,),
        in_specs=[
            pl.BlockSpec((1, 1, TOP_K * tc), lambda i: (i, 0, 0), memory_space=pltpu.SMEM),
            pl.BlockSpec(memory_space=pltpu.SMEM),
            pl.BlockSpec(memory_space=pltpu.SMEM),
            pl.BlockSpec((tc, width), lambda i: (i, 0)),
        ],
        out_specs=pl.BlockSpec(memory_space=pl.ANY),
        scratch_shapes=[pltpu.SemaphoreType.DMA(())],
        compiler_params=_cparams(("arbitrary",)),
        name="moe_dispatch",
    )(pos.reshape(n_steps, 1, TOP_K * tc), counts, offsets, packed)


def _moe_kernel(te_ref, nu_ref, rows_ref, xs_ref, wg_ref, wu_ref, wd_ref, o_ref, x_scr, a_scr):
    del te_ref, nu_ref
    i = pl.program_id(0)
    s = pl.program_id(1)
    tm, d = x_scr.shape
    nf = a_scr.shape[0]
    block = tm // MOE_ROW_BLOCKS
    n_blocks = (rows_ref[i] + block - 1) // block

    @pl.when((n_blocks > 0) & (s == 0))
    def _():
        lo, hi = _unpack_bf16_pairs(xs_ref[...])
        x_scr[:, :d // 2] = lo
        x_scr[:, d // 2:] = hi

    for k in range(1, MOE_ROW_BLOCKS + 1):
        m = k * block

        @pl.when((n_blocks == k) & (s < nf))
        def _():
            a_scr[s, :m] = _swiglu_act(x_scr[:m], wg_ref[...], wu_ref[...])

        @pl.when((n_blocks == k) & (s >= nf))
        def _():
            o_ref[:m] = _down_proj(a_scr, wd_ref[...], m)
            if m < tm:
                o_ref[m:] = jnp.zeros((tm - m, o_ref.shape[1]), o_ref.dtype)

    @pl.when((n_blocks == 0) & (s >= nf))
    def _():
        o_ref[...] = jnp.zeros(o_ref.shape, o_ref.dtype)


def _moe(tile_expert, n_used, tile_rows, xs, wg, wu, wd, *, tm, tf, tn, n_tiles):
    n_experts, d, dff = wg.shape
    nf = dff // tf
    nn = d // tn

    def row_map(i, s, te, nu, rows):
        return (jnp.minimum(i, nu[0] - 1), 0)

    def up_map(i, s, te, nu, rows):
        return (te[i], 0, jnp.where(i < nu[0], jnp.minimum(s, nf - 1), nf - 1))

    def down_map(i, s, te, nu, rows):
        return (te[i], 0, jnp.where(i < nu[0], jnp.maximum(s - nf, 0), nn - 1))

    grid_spec = pltpu.PrefetchScalarGridSpec(
        num_scalar_prefetch=3,
        grid=(n_tiles, nf + nn),
        in_specs=[
            pl.BlockSpec((tm, d // 2), row_map),
            pl.BlockSpec((None, d, tf), up_map),
            pl.BlockSpec((None, d, tf), up_map),
            pl.BlockSpec((None, dff, tn), down_map),
        ],
        out_specs=pl.BlockSpec((tm, tn),
                               lambda i, s, te, nu, rows: (i, jnp.maximum(s - nf, 0))),
        scratch_shapes=[pltpu.VMEM((tm, d), BF16), pltpu.VMEM((nf, tm, tf), BF16)],
    )
    return pl.pallas_call(
        _moe_kernel,
        out_shape=jax.ShapeDtypeStruct((n_tiles * tm, d), F32),
        grid_spec=grid_spec,
        compiler_params=_cparams(("arbitrary", "arbitrary")),
        name="moe_grouped_swiglu",
    )(tile_expert, n_used.reshape(1), tile_rows, xs, wg, wu, wd)


def _combine_kernel(pos_ref, ys_ref, x_ref, route_ref, o_ref, buf_a, buf_b, sem):
    tc = x_ref.shape[0]
    unroll = 8

    def row_copy(src_row, dst_row, buf):
        return pltpu.make_async_copy(ys_ref.at[pl.ds(src_row, 1), :],
                                     buf.at[pl.ds(dst_row, 1), :], sem)

    def issue(jo, carry):
        for ji in range(unroll):
            j = jo * unroll + ji
            row_copy(pos_ref[0, 0, TOP_K * j], j, buf_a).start()
            row_copy(pos_ref[0, 0, TOP_K * j + 1], j, buf_b).start()
        return carry

    lax.fori_loop(0, tc // unroll, issue, 0)
    for buf in (buf_a, buf_b):
        pltpu.make_async_copy(ys_ref.at[pl.ds(0, tc), :], buf, sem).wait()

    w1 = route_ref[:, 2:3]
    w2 = route_ref[:, 3:4]
    o_ref[...] = x_ref[...] + w1 * buf_a[...] + w2 * buf_b[...]


def _combine(pos, ys, x, route, *, tc):
    t, d = x.shape
    n_steps = t // tc
    return pl.pallas_call(
        _combine_kernel,
        out_shape=jax.ShapeDtypeStruct((t, d), F32),
        grid=(n_steps,),
        in_specs=[
            pl.BlockSpec((1, 1, TOP_K * tc), lambda i: (i, 0, 0), memory_space=pltpu.SMEM),
            pl.BlockSpec(memory_space=pl.ANY),
            pl.BlockSpec((tc, d), lambda i: (i, 0)),
            pl.BlockSpec((tc, LANES), lambda i: (i, 0)),
        ],
        out_specs=pl.BlockSpec((tc, d), lambda i: (i, 0)),
        scratch_shapes=[pltpu.VMEM((tc, d), F32), pltpu.VMEM((tc, d), F32),
                        pltpu.SemaphoreType.DMA(())],
        compiler_params=_cparams(("arbitrary",)),
        name="moe_combine",
    )(pos.reshape(n_steps, 1, TOP_K * tc), ys, x, route)


def _attention_layer(x, seq, mix_gain, ffn_gain, rel_bias, w_qkv, q_gain, k_gain, sink, w_o,
                     w_gate, w_up, w_down):
    t, d = x.shape
    n_heads = sink.shape[0]
    n_kv_heads = (w_qkv.shape[1] // HEAD_DIM - n_heads) // 2
    head_gain = jnp.concatenate([
        jnp.tile(q_gain * HEAD_DIM ** -0.5, n_heads),
        jnp.tile(k_gain, n_kv_heads),
        jnp.ones((n_kv_heads * HEAD_DIM,), F32)]).reshape(1, -1)
    qkv = _norm_matmul(x, (mix_gain[:, None] * w_qkv).astype(BF16), head_gain,
                       normed_heads=n_heads + n_kv_heads, out_dtype=BF16,
                       tm=_tile(t, 512), chunk=_tile(w_qkv.shape[1], 1024))
    attn = _attention(qkv, _band_bias(rel_bias), sink, seq=seq, n_heads=n_heads,
                      n_kv_heads=n_kv_heads, tq=_tile(seq, 512))
    x, h = _proj_res_norm(attn, w_o.astype(BF16), x, ffn_gain, tm=_tile(t, 512))
    return _ffn(h, x, w_gate.astype(BF16), w_up.astype(BF16), w_down.astype(BF16),
                tm=_tile(t, 1024), tf=_tile(w_gate.shape[1], 512), tn=_tile(d, 512))


def _pool_moe_layer(x, seq, mix_gain, ffn_gain, w_in, w_group, scale, w_out, router_w, router_b,
                    w_gate, w_up, w_down):
    t, d = x.shape
    n_experts = router_w.shape[1]
    u = _norm_matmul(x, (mix_gain[:, None] * w_in).astype(BF16), jnp.ones((1, d), F32),
                     normed_heads=0, out_dtype=F32, tm=_tile(t, 512), chunk=_tile(d, 1024))
    rw_hi = router_w.astype(BF16)
    rw_lo = (router_w - rw_hi.astype(F32)).astype(BF16)
    rw = (jnp.zeros((d, LANES), BF16).at[:, :n_experts].set(rw_hi)
          .at[:, n_experts:2 * n_experts].set(rw_lo))
    rb = jnp.zeros((1, LANES), F32).at[0, :n_experts].set(router_b)
    x, packed, route = _pool_tail(u, w_group.astype(BF16), scale, w_out.astype(BF16), x, ffn_gain,
                                  rw, rb, seq=seq, n_experts=n_experts, tm=_tile(seq, 256))
    tm = MOE_ROW_TILE
    n_tiles = TOP_K * t // tm + n_experts
    e_flat = route[:, :TOP_K].astype(I32).reshape(-1)
    pos, counts, offsets, tile_expert, n_used, tile_rows = _routing_plan(
        e_flat, n_experts=n_experts, tm=tm, n_tiles=n_tiles)
    xs = _dispatch(pos, counts, offsets, packed, n_rows=n_tiles * tm, tc=_tile(t, 512), tm=tm)
    ys = _moe(tile_expert, n_used, tile_rows, xs, w_gate, w_up, w_down, tm=tm,
              tf=_tile(w_gate.shape[2], 256), tn=_tile(d, 256), n_tiles=n_tiles)
    return _combine(pos, ys, x, route, tc=_tile(t, 256))


def kernel(x, mix_norm, ffn_norm, rel_bias, attn_w_qkv, attn_q_gain, attn_k_gain, attn_sink,
           attn_w_o, ffn_w_gate, ffn_w_up, ffn_w_down, pool_w_in, pool_w_group, pool_scale,
           pool_w_out, moe_router_w, moe_router_b, moe_w_gate, moe_w_up, moe_w_down):
    b, s, d = x.shape
    y = x.reshape(b * s, d)
    for i in range(mix_norm.shape[0]):
        j = i // 2
        if i % 2 == 0:
            y = _attention_layer(y, s, mix_norm[i], ffn_norm[i], rel_bias, attn_w_qkv[j],
                                 attn_q_gain[j], attn_k_gain[j], attn_sink[j], attn_w_o[j],
                                 ffn_w_gate[j], ffn_w_up[j], ffn_w_down[j])
        else:
            y = _pool_moe_layer(y, s, mix_norm[i], ffn_norm[i], pool_w_in[j], pool_w_group[j],
                                pool_scale[j], pool_w_out[j], moe_router_w[j], moe_router_b[j],
                                moe_w_gate[j], moe_w_up[j], moe_w_down[j])
    return y.reshape(b, s, d)
```

```python
import functools
import math

import jax
import jax.numpy as jnp
from jax import lax
from jax.experimental import pallas as pl
from jax.experimental.pallas import tpu as pltpu

F32 = jnp.float32
BF16 = jnp.bfloat16
U32 = jnp.uint32
I32 = jnp.int32

EPS = 1e-6
NEG_INF = -1e30

LANES = 128
VMEM_LIMIT_BYTES = 56 * 1024 * 1024

HEAD_DIM = 128
GROUP = 4
WINDOW = 128
BLOCK = 128
KEY_SPAN = BLOCK + 2 * WINDOW
NUM_BUCKETS = 32
MAX_DISTANCE = 128
POOL_WINDOWS = (2, 4, 8, 16)
POOL_HALO = 8
TOP_K = 2
MOE_ROW_TILE = 1024
MOE_ROW_BLOCKS = 4


def _cparams(semantics):
    return pltpu.CompilerParams(dimension_semantics=semantics,
                                vmem_limit_bytes=VMEM_LIMIT_BYTES)


def _rms(x, gain):
    ms = jnp.mean(x * x, axis=-1, keepdims=True)
    return x * lax.rsqrt(ms + EPS) * gain


def _tile(n, want):
    want = min(want, n)
    while n % want:
        want //= 2
    return want


def _norm_matmul_kernel(x_ref, w_ref, hg_ref, o_ref, *, normed_heads, chunk):
    x = x_ref[...]
    inv = lax.rsqrt(jnp.mean(x * x, axis=-1, keepdims=True) + EPS)
    xb = x.astype(BF16)
    for c0 in range(0, o_ref.shape[1], chunk):
        acc = jnp.dot(xb, w_ref[:, c0:c0 + chunk], preferred_element_type=F32) * inv
        for h0 in range(c0, c0 + chunk, HEAD_DIM):
            sl = slice(h0, h0 + HEAD_DIM)
            piece = acc[:, h0 - c0:h0 - c0 + HEAD_DIM]
            if h0 // HEAD_DIM < normed_heads:
                piece = _rms(piece, hg_ref[:, sl])
            o_ref[:, sl] = piece.astype(o_ref.dtype)


def _norm_matmul(x, w, head_gain, *, normed_heads, out_dtype, tm, chunk):
    t, d = x.shape
    n = w.shape[1]
    return pl.pallas_call(
        functools.partial(_norm_matmul_kernel, normed_heads=normed_heads, chunk=chunk),
        out_shape=jax.ShapeDtypeStruct((t, n), out_dtype),
        grid=(t // tm,),
        in_specs=[
            pl.BlockSpec((tm, d), lambda i: (i, 0)),
            pl.BlockSpec((d, n), lambda i: (0, 0)),
            pl.BlockSpec((1, n), lambda i: (0, 0)),
        ],
        out_specs=pl.BlockSpec((tm, n), lambda i: (i, 0)),
        compiler_params=_cparams(("parallel",)),
        name="norm_matmul",
    )(x, w, head_gain)


def _attn_kernel(sink_ref, q_ref, kv_ref, kvp_ref, kvn_ref, bias_ref, o_ref, *,
                 n_kv_heads, steps_per_seq):
    i = pl.program_id(0) % steps_per_seq
    tq = q_ref.shape[0]
    nqb = tq // BLOCK
    kw = n_kv_heads * HEAD_DIM
    prev_penalty = jnp.where(i > 0, 0.0, NEG_INF)
    next_penalty = jnp.where(i < steps_per_seq - 1, 0.0, NEG_INF)
    col = lax.broadcasted_iota(I32, (GROUP * BLOCK, KEY_SPAN), 1)

    for h in range(n_kv_heads):
        ksl = slice(h * HEAD_DIM, (h + 1) * HEAD_DIM)
        vsl = slice(kw + h * HEAD_DIM, kw + (h + 1) * HEAD_DIM)
        bias = bias_ref[h * GROUP:(h + 1) * GROUP].reshape(GROUP * BLOCK, KEY_SPAN)
        for qb in range(nqb):
            rows = slice(qb * BLOCK, (qb + 1) * BLOCK)
            prev_rows = slice((qb - 1) * BLOCK, qb * BLOCK)
            next_rows = slice((qb + 1) * BLOCK, (qb + 2) * BLOCK)
            if qb == 0:
                k_prev, v_prev = kvp_ref[:, ksl], kvp_ref[:, vsl]
            else:
                k_prev, v_prev = kv_ref[prev_rows, ksl], kv_ref[prev_rows, vsl]
            if qb == nqb - 1:
                k_next, v_next = kvn_ref[:, ksl], kvn_ref[:, vsl]
            else:
                k_next, v_next = kv_ref[next_rows, ksl], kv_ref[next_rows, vsl]
            k3 = jnp.concatenate([k_prev, kv_ref[rows, ksl], k_next], axis=0)
            v3 = jnp.concatenate([v_prev, kv_ref[rows, vsl], v_next], axis=0)
            q4 = jnp.concatenate(
                [q_ref[rows, (h * GROUP + g) * HEAD_DIM:(h * GROUP + g + 1) * HEAD_DIM]
                 for g in range(GROUP)], axis=0)
            s = lax.dot_general(q4, k3, (((1,), (1,)), ((), ())),
                                preferred_element_type=F32) + bias
            if qb == 0:
                s = s + jnp.where(col < WINDOW, prev_penalty, 0.0)
            if qb == nqb - 1:
                s = s + jnp.where(col >= WINDOW + BLOCK, next_penalty, 0.0)
            outs = []
            for g in range(GROUP):
                sg = s[g * BLOCK:(g + 1) * BLOCK]
                sink = sink_ref[h * GROUP + g]
                m = jnp.maximum(jnp.max(sg, axis=-1, keepdims=True), sink)
                p = jnp.exp(sg - m)
                denom = jnp.sum(p, axis=-1, keepdims=True) + jnp.exp(sink - m)
                pv = jnp.dot(p.astype(BF16), v3, preferred_element_type=F32)
                outs.append(pv / denom)
            for g in range(GROUP):
                o_ref[rows, (h * GROUP + g) * HEAD_DIM:(h * GROUP + g + 1) * HEAD_DIM] = (
                    outs[g].astype(o_ref.dtype))


def _attention(qkv, bias, sink, *, seq, n_heads, n_kv_heads, tq):
    t = qkv.shape[0]
    dq = n_heads * HEAD_DIM
    dkv = 2 * n_kv_heads * HEAD_DIM
    assert dq % dkv == 0
    kv_col = dq // dkv
    steps_per_seq = seq // tq
    blocks_per_step = tq // BLOCK
    last_block = t // BLOCK - 1
    return pl.pallas_call(
        functools.partial(_attn_kernel, n_kv_heads=n_kv_heads, steps_per_seq=steps_per_seq),
        out_shape=jax.ShapeDtypeStruct((t, dq), BF16),
        grid=(t // tq,),
        in_specs=[
            pl.BlockSpec(memory_space=pltpu.SMEM),
            pl.BlockSpec((tq, dq), lambda r: (r, 0)),
            pl.BlockSpec((tq, dkv), lambda r: (r, kv_col)),
            pl.BlockSpec((BLOCK, dkv),
                         lambda r: (jnp.maximum(r * blocks_per_step - 1, 0), kv_col)),
            pl.BlockSpec((BLOCK, dkv),
                         lambda r: (jnp.minimum((r + 1) * blocks_per_step, last_block), kv_col)),
            pl.BlockSpec((n_heads, BLOCK, KEY_SPAN), lambda r: (0, 0, 0)),
        ],
        out_specs=pl.BlockSpec((tq, dq), lambda r: (r, 0)),
        compiler_params=_cparams(("parallel",)),
        name="window_attention",
    )(sink, qkv, qkv, qkv, qkv, bias)


def _t5_bucket(rel):
    half = NUM_BUCKETS // 2
    ret = jnp.where(rel > 0, half, 0)
    n = jnp.abs(rel)
    max_exact = half // 2
    nf = jnp.maximum(n, 1).astype(F32)
    large = max_exact + (jnp.log(nf / max_exact) / math.log(MAX_DISTANCE / max_exact)
                         * (half - max_exact)).astype(I32)
    large = jnp.minimum(large, half - 1)
    return ret + jnp.where(n < max_exact, n, large)


def _band_bias(rel_bias):
    qi = jnp.arange(BLOCK)[:, None]
    kj = jnp.arange(KEY_SPAN)[None, :]
    rel = kj - WINDOW - qi
    onehot = (_t5_bucket(rel)[:, :, None] == jnp.arange(NUM_BUCKETS)[None, None, :]).astype(F32)
    bias = jnp.einsum("qkb,bh->hqk", onehot, rel_bias.astype(F32),
                      precision=lax.Precision.HIGHEST)
    return jnp.where((jnp.abs(rel) <= WINDOW)[None], bias, NEG_INF)


def _proj_res_norm_kernel(a_ref, w_ref, x_ref, g_ref, xo_ref, ho_ref):
    xo = x_ref[...] + jnp.dot(a_ref[...], w_ref[...], preferred_element_type=F32)
    xo_ref[...] = xo
    ho_ref[...] = _rms(xo, g_ref[...]).astype(BF16)


def _proj_res_norm(a, w, x, gain, *, tm):
    t, d = x.shape
    k = a.shape[1]
    return pl.pallas_call(
        _proj_res_norm_kernel,
        out_shape=(jax.ShapeDtypeStruct((t, d), F32), jax.ShapeDtypeStruct((t, d), BF16)),
        grid=(t // tm,),
        in_specs=[
            pl.BlockSpec((tm, k), lambda i: (i, 0)),
            pl.BlockSpec((k, d), lambda i: (0, 0)),
            pl.BlockSpec((tm, d), lambda i: (i, 0)),
            pl.BlockSpec((1, d), lambda i: (0, 0)),
        ],
        out_specs=(pl.BlockSpec((tm, d), lambda i: (i, 0)),
                   pl.BlockSpec((tm, d), lambda i: (i, 0))),
        compiler_params=_cparams(("parallel",)),
        name="proj_res_norm",
    )(a, w, x, gain.reshape(1, d))


def _swiglu_act(h, wg, wu):
    g = jnp.dot(h, wg.astype(BF16), preferred_element_type=F32)
    u = jnp.dot(h, wu.astype(BF16), preferred_element_type=F32)
    return (g * jax.nn.sigmoid(g) * u).astype(BF16)


def _down_proj(a_scr, wd, rows):
    a = jnp.concatenate([a_scr[k, :rows] for k in range(a_scr.shape[0])], axis=1)
    return jnp.dot(a, wd.astype(BF16), preferred_element_type=F32)


def _ffn_kernel(h_ref, x_ref, wg_ref, wu_ref, wd_ref, o_ref, a_scr):
    s = pl.program_id(1)
    nf = a_scr.shape[0]

    @pl.when(s < nf)
    def _():
        a_scr[s] = _swiglu_act(h_ref[...], wg_ref[...], wu_ref[...])

    @pl.when(s >= nf)
    def _():
        o_ref[...] = x_ref[...] + _down_proj(a_scr, wd_ref[...], o_ref.shape[0])


def _ffn(h, x, wg, wu, wd, *, tm, tf, tn):
    t, d = x.shape
    dff = wg.shape[1]
    nf = dff // tf
    nn = d // tn
    up_map = lambda i, s: (0, jnp.minimum(s, nf - 1))
    out_map = lambda i, s: (i, jnp.maximum(s - nf, 0))
    return pl.pallas_call(
        _ffn_kernel,
        out_shape=jax.ShapeDtypeStruct((t, d), F32),
        grid=(t // tm, nf + nn),
        in_specs=[
            pl.BlockSpec((tm, d), lambda i, s: (i, 0)),
            pl.BlockSpec((tm, tn), out_map),
            pl.BlockSpec((d, tf), up_map),
            pl.BlockSpec((d, tf), up_map),
            pl.BlockSpec((dff, tn), lambda i, s: (0, jnp.maximum(s - nf, 0))),
        ],
        out_specs=pl.BlockSpec((tm, tn), out_map),
        scratch_shapes=[pltpu.VMEM((nf, tm, tf), BF16)],
        compiler_params=_cparams(("parallel", "arbitrary")),
        name="dense_swiglu",
    )(h, x, wg, wu, wd)


def _pack_bf16_pairs(lo, hi):
    lo_bits = lax.bitcast_convert_type(lo.astype(BF16).astype(F32), U32)
    hi_bits = lax.bitcast_convert_type(hi.astype(BF16).astype(F32), U32)
    return (lo_bits >> 16) | hi_bits


def _unpack_bf16_pairs(words):
    lo = lax.bitcast_convert_type(words << 16, F32).astype(BF16)
    hi = lax.bitcast_convert_type(words & jnp.uint32(0xFFFF0000), F32).astype(BF16)
    return lo, hi


def _pool_tail_kernel(u_ref, up_ref, un_ref, wgrp_ref, scale_ref, wout_ref, x_ref, g_ref,
                      rw_ref, rb_ref, xo_ref, hp_ref, route_ref, *, seq, n_experts):
    tm, d = x_ref.shape
    gd = d // len(POOL_WINDOWS)
    ext = tm + 2 * POOL_HALO
    start = (pl.program_id(0) * tm) % seq
    pos = start + lax.broadcasted_iota(I32, (tm, 1), 0)

    ys = []
    for gi, win in enumerate(POOL_WINDOWS):
        cs = slice(gi * gd, (gi + 1) * gd)
        u = u_ref[:, cs]
        up = jnp.where(start > 0, up_ref[:, cs], 0.0)
        un = jnp.where(start + tm < seq, un_ref[:, cs], 0.0)
        a = jnp.concatenate([up, u, un], axis=0)
        a = a + pltpu.roll(a, 1, axis=0)
        w = 2
        while w < win:
            a = pltpu.roll(a, w // 2, axis=0) + pltpu.roll(a, ext - w // 2, axis=0)
            w *= 2
        half = win // 2
        count = (jnp.minimum(pos + half, seq) - jnp.maximum(pos - half, 0)).astype(F32)
        diff = a[POOL_HALO:POOL_HALO + tm] / count - u
        y = jnp.dot(diff.astype(BF16), wgrp_ref[gi], preferred_element_type=F32)
        ys.append((y * scale_ref[:, cs]).astype(BF16))
    y = jnp.concatenate(ys, axis=1)
    xo = x_ref[...] + jnp.dot(y, wout_ref[...], preferred_element_type=F32)
    xo_ref[...] = xo

    h = _rms(xo, g_ref[...])
    hp_ref[...] = _pack_bf16_pairs(h[:, :d // 2], h[:, d // 2:])

    h_hi = h.astype(BF16)
    h_lo = (h - h_hi.astype(F32)).astype(BF16)
    prod = jnp.dot(jnp.concatenate([h_hi, h_lo], axis=0), rw_ref[...],
                   preferred_element_type=F32)
    part = prod[:tm] + prod[tm:]
    logits = part + pltpu.roll(part, LANES - n_experts, axis=1) + rb_ref[...]
    lane = lax.broadcasted_iota(I32, logits.shape, 1)
    logits = jnp.where(lane < n_experts, logits, -jnp.inf)
    lane_f = lane.astype(F32)
    m1 = jnp.max(logits, axis=-1, keepdims=True)
    i1 = jnp.min(jnp.where(logits == m1, lane_f, float(LANES)), axis=-1, keepdims=True)
    rest = jnp.where(lane_f == i1, -jnp.inf, logits)
    m2 = jnp.max(rest, axis=-1, keepdims=True)
    i2 = jnp.min(jnp.where(rest == m2, lane_f, float(LANES)), axis=-1, keepdims=True)
    e2 = jnp.exp(m2 - m1)
    w1 = 1.0 / (1.0 + e2)
    w2 = e2 / (1.0 + e2)
    route_ref[...] = jnp.where(
        lane == 0, i1, jnp.where(lane == 1, i2,
                                 jnp.where(lane == 2, w1, jnp.where(lane == 3, w2, 0.0))))


def _pool_tail(u, wgrp, scale, wout, x, gain, rw, rb, *, seq, n_experts, tm):
    t, d = x.shape
    ng, gd, _ = wgrp.shape
    halo_blocks = tm // POOL_HALO
    last_halo = t // POOL_HALO - 1
    return pl.pallas_call(
        functools.partial(_pool_tail_kernel, seq=seq, n_experts=n_experts),
        out_shape=(jax.ShapeDtypeStruct((t, d), F32),
                   jax.ShapeDtypeStruct((t, d // 2), U32),
                   jax.ShapeDtypeStruct((t, LANES), F32)),
        grid=(t // tm,),
        in_specs=[
            pl.BlockSpec((tm, d), lambda i: (i, 0)),
            pl.BlockSpec((POOL_HALO, d), lambda i: (jnp.maximum(i * halo_blocks - 1, 0), 0)),
            pl.BlockSpec((POOL_HALO, d),
                         lambda i: (jnp.minimum((i + 1) * halo_blocks, last_halo), 0)),
            pl.BlockSpec((ng, gd, gd), lambda i: (0, 0, 0)),
            pl.BlockSpec((1, d), lambda i: (0, 0)),
            pl.BlockSpec((d, d), lambda i: (0, 0)),
            pl.BlockSpec((tm, d), lambda i: (i, 0)),
            pl.BlockSpec((1, d), lambda i: (0, 0)),
            pl.BlockSpec((d, LANES), lambda i: (0, 0)),
            pl.BlockSpec((1, LANES), lambda i: (0, 0)),
        ],
        out_specs=(pl.BlockSpec((tm, d), lambda i: (i, 0)),
                   pl.BlockSpec((tm, d // 2), lambda i: (i, 0)),
                   pl.BlockSpec((tm, LANES), lambda i: (i, 0))),
        compiler_params=_cparams(("parallel",)),
        name="pool_tail",
    )(u, u, u, wgrp, scale.reshape(1, d), wout, x, gain.reshape(1, d), rw, rb)


def _routing_plan(e_flat, *, n_experts, tm, n_tiles):
    onehot = (e_flat[:, None] == jnp.arange(n_experts, dtype=I32)[None, :]).astype(I32)
    csum = jnp.cumsum(onehot, axis=0)
    rank = jnp.sum(onehot * csum, axis=1) - 1
    counts = csum[-1]
    tiles_per = (counts + tm - 1) // tm
    padded = tiles_per * tm
    offsets = jnp.cumsum(padded) - padded
    pos = jnp.sum(onehot * offsets[None, :], axis=1) + rank
    tile_end = jnp.cumsum(tiles_per)
    n_used = tile_end[-1]
    tile_ids = jnp.arange(n_tiles, dtype=I32)
    tile_expert = jnp.sum((tile_ids[:, None] >= tile_end[None, :]).astype(I32), axis=1)
    last_expert = jnp.sum((n_used - 1 >= tile_end).astype(I32))
    tile_expert = jnp.minimum(tile_expert, last_expert).astype(I32)
    onehot_t = (tile_expert[:, None] == jnp.arange(n_experts, dtype=I32)[None, :]).astype(I32)
    first_tile = jnp.sum(onehot_t * (tile_end - tiles_per)[None, :], axis=1)
    left = jnp.sum(onehot_t * counts[None, :], axis=1) - (tile_ids - first_tile) * tm
    tile_rows = jnp.where(tile_ids < n_used, jnp.clip(left, 0, tm), 0).astype(I32)
    return (pos.astype(I32), counts.astype(I32), offsets.astype(I32), tile_expert,
            n_used.astype(I32), tile_rows)


def _dispatch_kernel(pos_ref, cnt_ref, off_ref, src_ref, dst_ref, sem, *, tm, n_experts):
    i = pl.program_id(0)
    tc = src_ref.shape[0]
    unroll = 8

    def row_copy(src_row, dst_row):
        return pltpu.make_async_copy(src_ref.at[pl.ds(src_row, 1), :],
                                     dst_ref.at[pl.ds(dst_row, 1), :], sem)

    def issue(jo, carry):
        for ji in range(unroll):
            j = jo * unroll + ji
            for k in range(TOP_K):
                row_copy(j, pos_ref[0, 0, TOP_K * j + k]).start()
        return carry

    lax.fori_loop(0, tc // unroll, issue, 0)
    for k in range(TOP_K):
        pltpu.make_async_copy(src_ref, dst_ref.at[pl.ds(0, tc), :], sem).wait()

    @pl.when(i == pl.num_programs(0) - 1)
    def _():
        for e in range(n_experts):
            cnt = cnt_ref[e]
            n_pad = (-cnt) % tm
            first = off_ref[e] + cnt

            def fill(j, carry):
                row_copy(0, first + j).start()
                return carry

            def fill_wait(j, carry):
                row_copy(0, 0).wait()
                return carry

            lax.fori_loop(0, n_pad, fill, 0)
            lax.fori_loop(0, n_pad, fill_wait, 0)

        n_used = sum((cnt_ref[e] + tm - 1) // tm for e in range(n_experts))
        fill_rows = math.gcd(tm, tc)

        def fill_block(j, carry):
            cp = pltpu.make_async_copy(
                src_ref.at[pl.ds(0, fill_rows), :],
                dst_ref.at[pl.ds(pl.multiple_of(j * fill_rows, fill_rows), fill_rows), :], sem)
            cp.start()
            cp.wait()
            return carry

        lax.fori_loop(n_used * (tm // fill_rows), dst_ref.shape[0] // fill_rows, fill_block, 0)


def _dispatch(pos, counts, offsets, packed, *, n_rows, tc, tm):
    t, width = packed.shape
    n_steps = t // tc
    return pl.pallas_call(
        functools.partial(_dispatch_kernel, tm=tm, n_experts=counts.shape[0]),
        out_shape=jax.ShapeDtypeStruct((n_rows, width), packed.dtype),
        grid=(n_steps,),
        in_specs=[
            pl.BlockSpec((1, 1, TOP_K * tc), lambda i: (i, 0, 0), memory_space=pltpu.SMEM),
            pl.BlockSpec(memory_space=pltpu.SMEM),
            pl.BlockSpec(memory_space=pltpu.SMEM),
            pl.BlockSpec((tc, width), lambda i: (i, 0)),
        ],
        out_specs=pl.BlockSpec(memory_space=pl.ANY),
        scratch_shapes=[pltpu.SemaphoreType.DMA(())],
        compiler_params=_cparams(("arbitrary",)),
        name="moe_dispatch",
    )(pos.reshape(n_steps, 1, TOP_K * tc), counts, offsets, packed)


def _moe_kernel(te_ref, nu_ref, rows_ref, xs_ref, wg_ref, wu_ref, wd_ref, o_ref, x_scr, a_scr):
    del te_ref, nu_ref
    i = pl.program_id(0)
    s = pl.program_id(1)
    tm, d = x_scr.shape
    nf = a_scr.shape[0]
    block = tm // MOE_ROW_BLOCKS
    n_blocks = (rows_ref[i] + block - 1) // block

    @pl.when((n_blocks > 0) & (s == 0))
    def _():
        lo, hi = _unpack_bf16_pairs(xs_ref[...])
        x_scr[:, :d // 2] = lo
        x_scr[:, d // 2:] = hi

    for k in range(1, MOE_ROW_BLOCKS + 1):
        m = k * block

        @pl.when((n_blocks == k) & (s < nf))
        def _():
            a_scr[s, :m] = _swiglu_act(x_scr[:m], wg_ref[...], wu_ref[...])

        @pl.when((n_blocks == k) & (s >= nf))
        def _():
            o_ref[:m] = _down_proj(a_scr, wd_ref[...], m)
            if m < tm:
                o_ref[m:] = jnp.zeros((tm - m, o_ref.shape[1]), o_ref.dtype)

    @pl.when((n_blocks == 0) & (s >= nf))
    def _():
        o_ref[...] = jnp.zeros(o_ref.shape, o_ref.dtype)


def _moe(tile_expert, n_used, tile_rows, xs, wg, wu, wd, *, tm, tf, tn, n_tiles):
    n_experts, d, dff = wg.shape
    nf = dff // tf
    nn = d // tn

    def row_map(i, s, te, nu, rows):
        return (jnp.minimum(i, nu[0] - 1), 0)

    def up_map(i, s, te, nu, rows):
        return (te[i], 0, jnp.where(i < nu[0], jnp.minimum(s, nf - 1), nf - 1))

    def down_map(i, s, te, nu, rows):
        return (te[i], 0, jnp.where(i < nu[0], jnp.maximum(s - nf, 0), nn - 1))

    grid_spec = pltpu.PrefetchScalarGridSpec(
        num_scalar_prefetch=3,
        grid=(n_tiles, nf + nn),
        in_specs=[
            pl.BlockSpec((tm, d // 2), row_map),
            pl.BlockSpec((None, d, tf), up_map),
            pl.BlockSpec((None, d, tf), up_map),
            pl.BlockSpec((None, dff, tn), down_map),
        ],
        out_specs=pl.BlockSpec((tm, tn),
                               lambda i, s, te, nu, rows: (i, jnp.maximum(s - nf, 0))),
        scratch_shapes=[pltpu.VMEM((tm, d), BF16), pltpu.VMEM((nf, tm, tf), BF16)],
    )
    return pl.pallas_call(
        _moe_kernel,
        out_shape=jax.ShapeDtypeStruct((n_tiles * tm, d), F32),
        grid_spec=grid_spec,
        compiler_params=_cparams(("arbitrary", "arbitrary")),
        name="moe_grouped_swiglu",
    )(tile_expert, n_used.reshape(1), tile_rows, xs, wg, wu, wd)


def _combine_kernel(pos_ref, pos_next_ref, ys_ref, x_ref, route_ref, o_ref, buf_a, buf_b, sems):
    i = pl.program_id(0)
    tc = x_ref.shape[0]
    slot = i % 2
    unroll = 8

    def start_gather(p_ref, sl):
        def row_copy(src_row, dst_row, buf):
            return pltpu.make_async_copy(ys_ref.at[pl.ds(src_row, 1), :],
                                         buf.at[sl, pl.ds(dst_row, 1), :], sems.at[sl])

        def issue(jo, carry):
            for ji in range(unroll):
                j = jo * unroll + ji
                row_copy(p_ref[0, 0, TOP_K * j], j, buf_a).start()
                row_copy(p_ref[0, 0, TOP_K * j + 1], j, buf_b).start()
            return carry

        lax.fori_loop(0, tc // unroll, issue, 0)

    @pl.when(i == 0)
    def _():
        start_gather(pos_ref, 0)

    @pl.when(i + 1 < pl.num_programs(0))
    def _():
        start_gather(pos_next_ref, 1 - slot)

    for buf in (buf_a, buf_b):
        pltpu.make_async_copy(ys_ref.at[pl.ds(0, tc), :], buf.at[slot], sems.at[slot]).wait()

    w1 = route_ref[:, 2:3]
    w2 = route_ref[:, 3:4]
    o_ref[...] = x_ref[...] + w1 * buf_a[slot] + w2 * buf_b[slot]


def _combine(pos, ys, x, route, *, tc):
    t, d = x.shape
    n_steps = t // tc
    pos = pos.reshape(n_steps, 1, TOP_K * tc)
    pos_spec = lambda index_map: pl.BlockSpec((1, 1, TOP_K * tc), index_map,
                                              memory_space=pltpu.SMEM)
    return pl.pallas_call(
        _combine_kernel,
        out_shape=jax.ShapeDtypeStruct((t, d), F32),
        grid=(n_steps,),
        in_specs=[
            pos_spec(lambda i: (i, 0, 0)),
            pos_spec(lambda i: (jnp.minimum(i + 1, n_steps - 1), 0, 0)),
            pl.BlockSpec(memory_space=pl.ANY),
            pl.BlockSpec((tc, d), lambda i: (i, 0)),
            pl.BlockSpec((tc, LANES), lambda i: (i, 0)),
        ],
        out_specs=pl.BlockSpec((tc, d), lambda i: (i, 0)),
        scratch_shapes=[pltpu.VMEM((2, tc, d), F32), pltpu.VMEM((2, tc, d), F32),
                        pltpu.SemaphoreType.DMA((2,))],
        compiler_params=_cparams(("arbitrary",)),
        name="moe_combine",
    )(pos, pos, ys, x, route)


def _attention_layer(x, seq, mix_gain, ffn_gain, rel_bias, w_qkv, q_gain, k_gain, sink, w_o,
                     w_gate, w_up, w_down):
    t, d = x.shape
    n_heads = sink.shape[0]
    n_kv_heads = (w_qkv.shape[1] // HEAD_DIM - n_heads) // 2
    head_gain = jnp.concatenate([
        jnp.tile(q_gain * HEAD_DIM ** -0.5, n_heads),
        jnp.tile(k_gain, n_kv_heads),
        jnp.ones((n_kv_heads * HEAD_DIM,), F32)]).reshape(1, -1)
    qkv = _norm_matmul(x, (mix_gain[:, None] * w_qkv).astype(BF16), head_gain,
                       normed_heads=n_heads + n_kv_heads, out_dtype=BF16,
                       tm=_tile(t, 512), chunk=_tile(w_qkv.shape[1], 1024))
    attn = _attention(qkv, _band_bias(rel_bias), sink, seq=seq, n_heads=n_heads,
                      n_kv_heads=n_kv_heads, tq=_tile(seq, 512))
    x, h = _proj_res_norm(attn, w_o.astype(BF16), x, ffn_gain, tm=_tile(t, 512))
    return _ffn(h, x, w_gate.astype(BF16), w_up.astype(BF16), w_down.astype(BF16),
                tm=_tile(t, 1024), tf=_tile(w_gate.shape[1], 512), tn=_tile(d, 512))


def _pool_moe_layer(x, seq, mix_gain, ffn_gain, w_in, w_group, scale, w_out, router_w, router_b,
                    w_gate, w_up, w_down):
    t, d = x.shape
    n_experts = router_w.shape[1]
    u = _norm_matmul(x, (mix_gain[:, None] * w_in).astype(BF16), jnp.ones((1, d), F32),
                     normed_heads=0, out_dtype=F32, tm=_tile(t, 512), chunk=_tile(d, 1024))
    rw_hi = router_w.astype(BF16)
    rw_lo = (router_w - rw_hi.astype(F32)).astype(BF16)
    rw = (jnp.zeros((d, LANES), BF16).at[:, :n_experts].set(rw_hi)
          .at[:, n_experts:2 * n_experts].set(rw_lo))
    rb = jnp.zeros((1, LANES), F32).at[0, :n_experts].set(router_b)
    x, packed, route = _pool_tail(u, w_group.astype(BF16), scale, w_out.astype(BF16), x, ffn_gain,
                                  rw, rb, seq=seq, n_experts=n_experts, tm=_tile(seq, 256))
    tm = MOE_ROW_TILE
    n_tiles = TOP_K * t // tm + n_experts
    e_flat = route[:, :TOP_K].astype(I32).reshape(-1)
    pos, counts, offsets, tile_expert, n_used, tile_rows = _routing_plan(
        e_flat, n_experts=n_experts, tm=tm, n_tiles=n_tiles)
    xs = _dispatch(pos, counts, offsets, packed, n_rows=n_tiles * tm, tc=_tile(t, 1024), tm=tm)
    ys = _moe(tile_expert, n_used, tile_rows, xs, w_gate, w_up, w_down, tm=tm,
              tf=_tile(w_gate.shape[2], 256), tn=_tile(d, 256), n_tiles=n_tiles)
    return _combine(pos, ys, x, route, tc=_tile(t, 256))


def kernel(x, mix_norm, ffn_norm, rel_bias, attn_w_qkv, attn_q_gain, attn_k_gain, attn_sink,
           attn_w_o, ffn_w_gate, ffn_w_up, ffn_w_down, pool_w_in, pool_w_group, pool_scale,
           pool_w_out, moe_router_w, moe_router_b, moe_w_gate, moe_w_up, moe_w_down):
    b, s, d = x.shape
    y = x.reshape(b * s, d)
    for i in range(mix_norm.shape[0]):
        j = i // 2
        if i % 2 == 0:
            y = _attention_layer(y, s, mix_norm[i], ffn_norm[i], rel_bias, attn_w_qkv[j],
                                 attn_q_gain[j], attn_k_gain[j], attn_sink[j], attn_w_o[j],
                                 ffn_w_gate[j], ffn_w_up[j], ffn_w_down[j])
        else:
            y = _pool_moe_layer(y, s, mix_norm[i], ffn_norm[i], pool_w_in[j], pool_w_group[j],
                                pool_scale[j], pool_w_out[j], moe_router_w[j], moe_router_b[j],
                                moe_w_gate[j], moe_w_up[j], moe_w_down[j])
    return y.reshape(b, s, d)
```

```python
import functools
import math

import jax
import jax.numpy as jnp
from jax import lax
from jax.experimental import pallas as pl
from jax.experimental.pallas import tpu as pltpu

F32 = jnp.float32
BF16 = jnp.bfloat16
U32 = jnp.uint32
I32 = jnp.int32

EPS = 1e-6
NEG_INF = -1e30

LANES = 128
VMEM_LIMIT_BYTES = 56 * 1024 * 1024

HEAD_DIM = 128
GROUP = 4
WINDOW = 128
BLOCK = 128
KEY_SPAN = BLOCK + 2 * WINDOW
NUM_BUCKETS = 32
MAX_DISTANCE = 128
POOL_WINDOWS = (2, 4, 8, 16)
POOL_HALO = 8
TOP_K = 2
MOE_ROW_TILE = 1024
MOE_ROW_BLOCKS = 4
MOE_GATHER_MAX_PER_STEP = 96


def _cparams(semantics):
    return pltpu.CompilerParams(dimension_semantics=semantics,
                                vmem_limit_bytes=VMEM_LIMIT_BYTES)


def _rms(x, gain):
    ms = jnp.mean(x * x, axis=-1, keepdims=True)
    return x * lax.rsqrt(ms + EPS) * gain


def _tile(n, want):
    want = min(want, n)
    while n % want:
        want //= 2
    return want


def _norm_matmul_kernel(x_ref, w_ref, hg_ref, o_ref, *, normed_heads, chunk):
    x = x_ref[...]
    inv = lax.rsqrt(jnp.mean(x * x, axis=-1, keepdims=True) + EPS)
    xb = x.astype(BF16)
    for c0 in range(0, o_ref.shape[1], chunk):
        acc = jnp.dot(xb, w_ref[:, c0:c0 + chunk], preferred_element_type=F32) * inv
        for h0 in range(c0, c0 + chunk, HEAD_DIM):
            sl = slice(h0, h0 + HEAD_DIM)
            piece = acc[:, h0 - c0:h0 - c0 + HEAD_DIM]
            if h0 // HEAD_DIM < normed_heads:
                piece = _rms(piece, hg_ref[:, sl])
            o_ref[:, sl] = piece.astype(o_ref.dtype)


def _norm_matmul(x, w, head_gain, *, normed_heads, out_dtype, tm, chunk):
    t, d = x.shape
    n = w.shape[1]
    return pl.pallas_call(
        functools.partial(_norm_matmul_kernel, normed_heads=normed_heads, chunk=chunk),
        out_shape=jax.ShapeDtypeStruct((t, n), out_dtype),
        grid=(t // tm,),
        in_specs=[
            pl.BlockSpec((tm, d), lambda i: (i, 0)),
            pl.BlockSpec((d, n), lambda i: (0, 0)),
            pl.BlockSpec((1, n), lambda i: (0, 0)),
        ],
        out_specs=pl.BlockSpec((tm, n), lambda i: (i, 0)),
        compiler_params=_cparams(("parallel",)),
        name="norm_matmul",
    )(x, w, head_gain)


def _attn_kernel(sink_ref, q_ref, kv_ref, kvp_ref, kvn_ref, bias_ref, o_ref, *,
                 n_kv_heads, steps_per_seq):
    i = pl.program_id(0) % steps_per_seq
    tq = q_ref.shape[0]
    nqb = tq // BLOCK
    kw = n_kv_heads * HEAD_DIM
    prev_penalty = jnp.where(i > 0, 0.0, NEG_INF)
    next_penalty = jnp.where(i < steps_per_seq - 1, 0.0, NEG_INF)
    col = lax.broadcasted_iota(I32, (GROUP * BLOCK, KEY_SPAN), 1)

    for h in range(n_kv_heads):
        ksl = slice(h * HEAD_DIM, (h + 1) * HEAD_DIM)
        vsl = slice(kw + h * HEAD_DIM, kw + (h + 1) * HEAD_DIM)
        bias = bias_ref[h * GROUP:(h + 1) * GROUP].reshape(GROUP * BLOCK, KEY_SPAN)
        for qb in range(nqb):
            rows = slice(qb * BLOCK, (qb + 1) * BLOCK)
            prev_rows = slice((qb - 1) * BLOCK, qb * BLOCK)
            next_rows = slice((qb + 1) * BLOCK, (qb + 2) * BLOCK)
            if qb == 0:
                k_prev, v_prev = kvp_ref[:, ksl], kvp_ref[:, vsl]
            else:
                k_prev, v_prev = kv_ref[prev_rows, ksl], kv_ref[prev_rows, vsl]
            if qb == nqb - 1:
                k_next, v_next = kvn_ref[:, ksl], kvn_ref[:, vsl]
            else:
                k_next, v_next = kv_ref[next_rows, ksl], kv_ref[next_rows, vsl]
            k3 = jnp.concatenate([k_prev, kv_ref[rows, ksl], k_next], axis=0)
            v3 = jnp.concatenate([v_prev, kv_ref[rows, vsl], v_next], axis=0)
            q4 = jnp.concatenate(
                [q_ref[rows, (h * GROUP + g) * HEAD_DIM:(h * GROUP + g + 1) * HEAD_DIM]
                 for g in range(GROUP)], axis=0)
            s = lax.dot_general(q4, k3, (((1,), (1,)), ((), ())),
                                preferred_element_type=F32) + bias
            if qb == 0:
                s = s + jnp.where(col < WINDOW, prev_penalty, 0.0)
            if qb == nqb - 1:
                s = s + jnp.where(col >= WINDOW + BLOCK, next_penalty, 0.0)
            outs = []
            for g in range(GROUP):
                sg = s[g * BLOCK:(g + 1) * BLOCK]
                sink = sink_ref[h * GROUP + g]
                m = jnp.maximum(jnp.max(sg, axis=-1, keepdims=True), sink)
                p = jnp.exp(sg - m)
                denom = jnp.sum(p, axis=-1, keepdims=True) + jnp.exp(sink - m)
                pv = jnp.dot(p.astype(BF16), v3, preferred_element_type=F32)
                outs.append(pv / denom)
            for g in range(GROUP):
                o_ref[rows, (h * GROUP + g) * HEAD_DIM:(h * GROUP + g + 1) * HEAD_DIM] = (
                    outs[g].astype(o_ref.dtype))


def _attention(qkv, bias, sink, *, seq, n_heads, n_kv_heads, tq):
    t = qkv.shape[0]
    dq = n_heads * HEAD_DIM
    dkv = 2 * n_kv_heads * HEAD_DIM
    assert dq % dkv == 0
    kv_col = dq // dkv
    steps_per_seq = seq // tq
    blocks_per_step = tq // BLOCK
    last_block = t // BLOCK - 1
    return pl.pallas_call(
        functools.partial(_attn_kernel, n_kv_heads=n_kv_heads, steps_per_seq=steps_per_seq),
        out_shape=jax.ShapeDtypeStruct((t, dq), BF16),
        grid=(t // tq,),
        in_specs=[
            pl.BlockSpec(memory_space=pltpu.SMEM),
            pl.BlockSpec((tq, dq), lambda r: (r, 0)),
            pl.BlockSpec((tq, dkv), lambda r: (r, kv_col)),
            pl.BlockSpec((BLOCK, dkv),
                         lambda r: (jnp.maximum(r * blocks_per_step - 1, 0), kv_col)),
            pl.BlockSpec((BLOCK, dkv),
                         lambda r: (jnp.minimum((r + 1) * blocks_per_step, last_block), kv_col)),
            pl.BlockSpec((n_heads, BLOCK, KEY_SPAN), lambda r: (0, 0, 0)),
        ],
        out_specs=pl.BlockSpec((tq, dq), lambda r: (r, 0)),
        compiler_params=_cparams(("parallel",)),
        name="window_attention",
    )(sink, qkv, qkv, qkv, qkv, bias)


def _t5_bucket(rel):
    half = NUM_BUCKETS // 2
    ret = jnp.where(rel > 0, half, 0)
    n = jnp.abs(rel)
    max_exact = half // 2
    nf = jnp.maximum(n, 1).astype(F32)
    large = max_exact + (jnp.log(nf / max_exact) / math.log(MAX_DISTANCE / max_exact)
                         * (half - max_exact)).astype(I32)
    large = jnp.minimum(large, half - 1)
    return ret + jnp.where(n < max_exact, n, large)


def _band_bias(rel_bias):
    qi = jnp.arange(BLOCK)[:, None]
    kj = jnp.arange(KEY_SPAN)[None, :]
    rel = kj - WINDOW - qi
    onehot = (_t5_bucket(rel)[:, :, None] == jnp.arange(NUM_BUCKETS)[None, None, :]).astype(F32)
    bias = jnp.einsum("qkb,bh->hqk", onehot, rel_bias.astype(F32),
                      precision=lax.Precision.HIGHEST)
    return jnp.where((jnp.abs(rel) <= WINDOW)[None], bias, NEG_INF)


def _proj_res_norm_kernel(a_ref, w_ref, x_ref, g_ref, xo_ref, ho_ref):
    xo = x_ref[...] + jnp.dot(a_ref[...], w_ref[...], preferred_element_type=F32)
    xo_ref[...] = xo
    ho_ref[...] = _rms(xo, g_ref[...]).astype(BF16)


def _proj_res_norm(a, w, x, gain, *, tm):
    t, d = x.shape
    k = a.shape[1]
    return pl.pallas_call(
        _proj_res_norm_kernel,
        out_shape=(jax.ShapeDtypeStruct((t, d), F32), jax.ShapeDtypeStruct((t, d), BF16)),
        grid=(t // tm,),
        in_specs=[
            pl.BlockSpec((tm, k), lambda i: (i, 0)),
            pl.BlockSpec((k, d), lambda i: (0, 0)),
            pl.BlockSpec((tm, d), lambda i: (i, 0)),
            pl.BlockSpec((1, d), lambda i: (0, 0)),
        ],
        out_specs=(pl.BlockSpec((tm, d), lambda i: (i, 0)),
                   pl.BlockSpec((tm, d), lambda i: (i, 0))),
        compiler_params=_cparams(("parallel",)),
        name="proj_res_norm",
    )(a, w, x, gain.reshape(1, d))


def _swiglu_act(h, wg, wu):
    g = jnp.dot(h, wg.astype(BF16), preferred_element_type=F32)
    u = jnp.dot(h, wu.astype(BF16), preferred_element_type=F32)
    return (g * jax.nn.sigmoid(g) * u).astype(BF16)


def _down_proj(a_scr, wd, rows):
    a = jnp.concatenate([a_scr[k, :rows] for k in range(a_scr.shape[0])], axis=1)
    return jnp.dot(a, wd.astype(BF16), preferred_element_type=F32)


def _ffn_kernel(h_ref, x_ref, wg_ref, wu_ref, wd_ref, o_ref, a_scr):
    s = pl.program_id(1)
    nf = a_scr.shape[0]

    @pl.when(s < nf)
    def _():
        a_scr[s] = _swiglu_act(h_ref[...], wg_ref[...], wu_ref[...])

    @pl.when(s >= nf)
    def _():
        o_ref[...] = x_ref[...] + _down_proj(a_scr, wd_ref[...], o_ref.shape[0])


def _ffn(h, x, wg, wu, wd, *, tm, tf, tn):
    t, d = x.shape
    dff = wg.shape[1]
    nf = dff // tf
    nn = d // tn
    up_map = lambda i, s: (0, jnp.minimum(s, nf - 1))
    out_map = lambda i, s: (i, jnp.maximum(s - nf, 0))
    return pl.pallas_call(
        _ffn_kernel,
        out_shape=jax.ShapeDtypeStruct((t, d), F32),
        grid=(t // tm, nf + nn),
        in_specs=[
            pl.BlockSpec((tm, d), lambda i, s: (i, 0)),
            pl.BlockSpec((tm, tn), out_map),
            pl.BlockSpec((d, tf), up_map),
            pl.BlockSpec((d, tf), up_map),
            pl.BlockSpec((dff, tn), lambda i, s: (0, jnp.maximum(s - nf, 0))),
        ],
        out_specs=pl.BlockSpec((tm, tn), out_map),
        scratch_shapes=[pltpu.VMEM((nf, tm, tf), BF16)],
        compiler_params=_cparams(("parallel", "arbitrary")),
        name="dense_swiglu",
    )(h, x, wg, wu, wd)


def _pack_bf16_pairs(lo, hi):
    lo_bits = lax.bitcast_convert_type(lo.astype(BF16).astype(F32), U32)
    hi_bits = lax.bitcast_convert_type(hi.astype(BF16).astype(F32), U32)
    return (lo_bits >> 16) | hi_bits


def _unpack_bf16_pairs(words):
    lo = lax.bitcast_convert_type(words << 16, F32).astype(BF16)
    hi = lax.bitcast_convert_type(words & jnp.uint32(0xFFFF0000), F32).astype(BF16)
    return lo, hi


def _pool_tail_kernel(u_ref, up_ref, un_ref, wgrp_ref, scale_ref, wout_ref, x_ref, g_ref,
                      rw_ref, rb_ref, xo_ref, hp_ref, route_ref, *, seq, n_experts):
    tm, d = x_ref.shape
    gd = d // len(POOL_WINDOWS)
    ext = tm + 2 * POOL_HALO
    start = (pl.program_id(0) * tm) % seq
    pos = start + lax.broadcasted_iota(I32, (tm, 1), 0)

    ys = []
    for gi, win in enumerate(POOL_WINDOWS):
        cs = slice(gi * gd, (gi + 1) * gd)
        u = u_ref[:, cs]
        up = jnp.where(start > 0, up_ref[:, cs], 0.0)
        un = jnp.where(start + tm < seq, un_ref[:, cs], 0.0)
        a = jnp.concatenate([up, u, un], axis=0)
        a = a + pltpu.roll(a, 1, axis=0)
        w = 2
        while w < win:
            a = pltpu.roll(a, w // 2, axis=0) + pltpu.roll(a, ext - w // 2, axis=0)
            w *= 2
        half = win // 2
        count = (jnp.minimum(pos + half, seq) - jnp.maximum(pos - half, 0)).astype(F32)
        diff = a[POOL_HALO:POOL_HALO + tm] / count - u
        y = jnp.dot(diff.astype(BF16), wgrp_ref[gi], preferred_element_type=F32)
        ys.append((y * scale_ref[:, cs]).astype(BF16))
    y = jnp.concatenate(ys, axis=1)
    xo = x_ref[...] + jnp.dot(y, wout_ref[...], preferred_element_type=F32)
    xo_ref[...] = xo

    h = _rms(xo, g_ref[...])
    hp_ref[...] = _pack_bf16_pairs(h[:, :d // 2], h[:, d // 2:])

    h_hi = h.astype(BF16)
    h_lo = (h - h_hi.astype(F32)).astype(BF16)
    prod = jnp.dot(jnp.concatenate([h_hi, h_lo], axis=0), rw_ref[...],
                   preferred_element_type=F32)
    part = prod[:tm] + prod[tm:]
    logits = part + pltpu.roll(part, LANES - n_experts, axis=1) + rb_ref[...]
    lane = lax.broadcasted_iota(I32, logits.shape, 1)
    logits = jnp.where(lane < n_experts, logits, -jnp.inf)
    lane_f = lane.astype(F32)
    m1 = jnp.max(logits, axis=-1, keepdims=True)
    i1 = jnp.min(jnp.where(logits == m1, lane_f, float(LANES)), axis=-1, keepdims=True)
    rest = jnp.where(lane_f == i1, -jnp.inf, logits)
    m2 = jnp.max(rest, axis=-1, keepdims=True)
    i2 = jnp.min(jnp.where(rest == m2, lane_f, float(LANES)), axis=-1, keepdims=True)
    e2 = jnp.exp(m2 - m1)
    w1 = 1.0 / (1.0 + e2)
    w2 = e2 / (1.0 + e2)
    route_ref[...] = jnp.where(
        lane == 0, i1, jnp.where(lane == 1, i2,
                                 jnp.where(lane == 2, w1, jnp.where(lane == 3, w2, 0.0))))


def _pool_tail(u, wgrp, scale, wout, x, gain, rw, rb, *, seq, n_experts, tm):
    t, d = x.shape
    ng, gd, _ = wgrp.shape
    halo_blocks = tm // POOL_HALO
    last_halo = t // POOL_HALO - 1
    return pl.pallas_call(
        functools.partial(_pool_tail_kernel, seq=seq, n_experts=n_experts),
        out_shape=(jax.ShapeDtypeStruct((t, d), F32),
                   jax.ShapeDtypeStruct((t, d // 2), U32),
                   jax.ShapeDtypeStruct((t, LANES), F32)),
        grid=(t // tm,),
        in_specs=[
            pl.BlockSpec((tm, d), lambda i: (i, 0)),
            pl.BlockSpec((POOL_HALO, d), lambda i: (jnp.maximum(i * halo_blocks - 1, 0), 0)),
            pl.BlockSpec((POOL_HALO, d),
                         lambda i: (jnp.minimum((i + 1) * halo_blocks, last_halo), 0)),
            pl.BlockSpec((ng, gd, gd), lambda i: (0, 0, 0)),
            pl.BlockSpec((1, d), lambda i: (0, 0)),
            pl.BlockSpec((d, d), lambda i: (0, 0)),
            pl.BlockSpec((tm, d), lambda i: (i, 0)),
            pl.BlockSpec((1, d), lambda i: (0, 0)),
            pl.BlockSpec((d, LANES), lambda i: (0, 0)),
            pl.BlockSpec((1, LANES), lambda i: (0, 0)),
        ],
        out_specs=(pl.BlockSpec((tm, d), lambda i: (i, 0)),
                   pl.BlockSpec((tm, d // 2), lambda i: (i, 0)),
                   pl.BlockSpec((tm, LANES), lambda i: (i, 0))),
        compiler_params=_cparams(("parallel",)),
        name="pool_tail",
    )(u, u, u, wgrp, scale.reshape(1, d), wout, x, gain.reshape(1, d), rw, rb)


def _routing_plan(e_flat, *, n_experts, tm, n_tiles):
    onehot = (e_flat[:, None] == jnp.arange(n_experts, dtype=I32)[None, :]).astype(I32)
    csum = jnp.cumsum(onehot, axis=0)
    rank = jnp.sum(onehot * csum, axis=1) - 1
    counts = csum[-1]
    tiles_per = (counts + tm - 1) // tm
    padded = tiles_per * tm
    offsets = jnp.cumsum(padded) - padded
    pos = jnp.sum(onehot * offsets[None, :], axis=1) + rank
    tile_end = jnp.cumsum(tiles_per)
    n_used = tile_end[-1]
    tile_ids = jnp.arange(n_tiles, dtype=I32)
    tile_expert = jnp.sum((tile_ids[:, None] >= tile_end[None, :]).astype(I32), axis=1)
    last_expert = jnp.sum((n_used - 1 >= tile_end).astype(I32))
    tile_expert = jnp.minimum(tile_expert, last_expert).astype(I32)
    onehot_t = (tile_expert[:, None] == jnp.arange(n_experts, dtype=I32)[None, :]).astype(I32)
    first_tile = jnp.sum(onehot_t * (tile_end - tiles_per)[None, :], axis=1)
    left = jnp.sum(onehot_t * counts[None, :], axis=1) - (tile_ids - first_tile) * tm
    tile_rows = jnp.where(tile_ids < n_used, jnp.clip(left, 0, tm), 0).astype(I32)
    order = jnp.argsort(e_flat, stable=True).astype(I32)
    seg_start = jnp.sum(onehot_t * (jnp.cumsum(counts) - counts)[None, :], axis=1)
    local = jnp.arange(tm, dtype=I32)[None, :]
    base = ((tile_ids - first_tile) * tm)[:, None]
    idx = jnp.clip(seg_start[:, None] + base + local, 0, e_flat.shape[0] - 1)
    src = jnp.where(local < tile_rows[:, None], order[idx] // TOP_K, 0).astype(I32)
    return pos.astype(I32), tile_expert, n_used.astype(I32), tile_rows, src


def _gather_rows_per_step(tm, nf, nn):
    for r2 in range(min(tm // nn, MOE_GATHER_MAX_PER_STEP), -1, -8):
        if (tm - nn * r2) % nf == 0:
            return (tm - nn * r2) // nf, r2
    raise ValueError("no per-step split of the row gather")


def _moe_kernel(te_ref, nu_ref, rows_ref, src0_ref, src_next_ref, packed_ref, wg_ref, wu_ref,
                wd_ref, o_ref, xg, x_scr, a_scr, sems):
    del te_ref, nu_ref
    i = pl.program_id(0)
    s = pl.program_id(1)
    tm, d = x_scr.shape
    nf = a_scr.shape[0]
    nn = d // wd_ref.shape[1]
    n_steps = nf + nn
    r1, r2 = _gather_rows_per_step(tm, nf, nn)
    block = tm // MOE_ROW_BLOCKS
    n_blocks = (rows_ref[i] + block - 1) // block
    slot = i % 2

    def row_copy(list_ref, row, sl):
        return pltpu.make_async_copy(packed_ref.at[pl.ds(list_ref[0, 0, row], 1), :],
                                     xg.at[sl, pl.ds(row, 1), :], sems.at[sl])

    def slot_wait(sl):
        pltpu.make_async_copy(packed_ref.at[pl.ds(0, tm), :], xg.at[sl], sems.at[sl]).wait()

    def gather_ahead(first, count):
        for r in range(count):
            row_copy(src_next_ref, first + r, 1 - slot).start()

    @pl.when((i == 0) & (s == 0))
    def _():
        def issue(row, carry):
            row_copy(src0_ref, row, 0).start()
            return carry

        lax.fori_loop(0, tm, issue, 0)

    @pl.when(s == 0)
    def _():
        slot_wait(slot)

    @pl.when((n_blocks > 0) & (s == 0))
    def _():
        lo, hi = _unpack_bf16_pairs(xg[slot])
        x_scr[:, :d // 2] = lo
        x_scr[:, d // 2:] = hi

    for k in range(0, MOE_ROW_BLOCKS + 1):
        m = k * block

        @pl.when((n_blocks == k) & (s < nf))
        def _():
            gather_ahead(s * r1, r1)
            if m:
                a_scr[s, :m] = _swiglu_act(x_scr[:m], wg_ref[...], wu_ref[...])

        @pl.when((n_blocks == k) & (s >= nf))
        def _():
            gather_ahead(nf * r1 + (s - nf) * r2, r2)
            if m:
                o_ref[:m] = _down_proj(a_scr, wd_ref[...], m)
            if m < tm:
                o_ref[m:] = jnp.zeros((tm - m, o_ref.shape[1]), o_ref.dtype)

    @pl.when((i == pl.num_programs(0) - 1) & (s == n_steps - 1))
    def _():
        slot_wait(1 - slot)


def _moe(tile_expert, n_used, tile_rows, src, packed, wg, wu, wd, *, tm, tf, tn, n_tiles):
    n_experts, d, dff = wg.shape
    nf = dff // tf
    nn = d // tn
    src = src.reshape(n_tiles, 1, tm)

    def up_map(i, s, te, nu, rows):
        return (te[i], 0, jnp.where(i < nu[0], jnp.minimum(s, nf - 1), nf - 1))

    def down_map(i, s, te, nu, rows):
        return (te[i], 0, jnp.where(i < nu[0], jnp.maximum(s - nf, 0), nn - 1))

    src_spec = lambda index_map: pl.BlockSpec((1, 1, tm), index_map, memory_space=pltpu.SMEM)
    grid_spec = pltpu.PrefetchScalarGridSpec(
        num_scalar_prefetch=3,
        grid=(n_tiles, nf + nn),
        in_specs=[
            src_spec(lambda i, s, te, nu, rows: (0, 0, 0)),
            src_spec(lambda i, s, te, nu, rows: (jnp.minimum(i + 1, n_tiles - 1), 0, 0)),
            pl.BlockSpec(memory_space=pl.ANY),
            pl.BlockSpec((None, d, tf), up_map),
            pl.BlockSpec((None, d, tf), up_map),
            pl.BlockSpec((None, dff, tn), down_map),
        ],
        out_specs=pl.BlockSpec((tm, tn),
                               lambda i, s, te, nu, rows: (i, jnp.maximum(s - nf, 0))),
        scratch_shapes=[pltpu.VMEM((2, tm, d // 2), U32), pltpu.VMEM((tm, d), BF16),
                        pltpu.VMEM((nf, tm, tf), BF16), pltpu.SemaphoreType.DMA((2,))],
    )
    return pl.pallas_call(
        _moe_kernel,
        out_shape=jax.ShapeDtypeStruct((n_tiles * tm, d), F32),
        grid_spec=grid_spec,
        compiler_params=_cparams(("arbitrary", "arbitrary")),
        name="moe_grouped_swiglu",
    )(tile_expert, n_used.reshape(1), tile_rows, src, src, packed, wg, wu, wd)


def _combine_kernel(pos_ref, pos_next_ref, ys_ref, x_ref, route_ref, o_ref, buf_a, buf_b, sems):
    i = pl.program_id(0)
    tc = x_ref.shape[0]
    slot = i % 2
    unroll = 8

    def start_gather(p_ref, sl):
        def row_copy(src_row, dst_row, buf):
            return pltpu.make_async_copy(ys_ref.at[pl.ds(src_row, 1), :],
                                         buf.at[sl, pl.ds(dst_row, 1), :], sems.at[sl])

        def issue(jo, carry):
            for ji in range(unroll):
                j = jo * unroll + ji
                row_copy(p_ref[0, 0, TOP_K * j], j, buf_a).start()
                row_copy(p_ref[0, 0, TOP_K * j + 1], j, buf_b).start()
            return carry

        lax.fori_loop(0, tc // unroll, issue, 0)

    @pl.when(i == 0)
    def _():
        start_gather(pos_ref, 0)

    @pl.when(i + 1 < pl.num_programs(0))
    def _():
        start_gather(pos_next_ref, 1 - slot)

    for buf in (buf_a, buf_b):
        pltpu.make_async_copy(ys_ref.at[pl.ds(0, tc), :], buf.at[slot], sems.at[slot]).wait()

    w1 = route_ref[:, 2:3]
    w2 = route_ref[:, 3:4]
    o_ref[...] = x_ref[...] + w1 * buf_a[slot] + w2 * buf_b[slot]


def _combine(pos, ys, x, route, *, tc):
    t, d = x.shape
    n_steps = t // tc
    pos = pos.reshape(n_steps, 1, TOP_K * tc)
    pos_spec = lambda index_map: pl.BlockSpec((1, 1, TOP_K * tc), index_map,
                                              memory_space=pltpu.SMEM)
    return pl.pallas_call(
        _combine_kernel,
        out_shape=jax.ShapeDtypeStruct((t, d), F32),
        grid=(n_steps,),
        in_specs=[
            pos_spec(lambda i: (i, 0, 0)),
            pos_spec(lambda i: (jnp.minimum(i + 1, n_steps - 1), 0, 0)),
            pl.BlockSpec(memory_space=pl.ANY),
            pl.BlockSpec((tc, d), lambda i: (i, 0)),
            pl.BlockSpec((tc, LANES), lambda i: (i, 0)),
        ],
        out_specs=pl.BlockSpec((tc, d), lambda i: (i, 0)),
        scratch_shapes=[pltpu.VMEM((2, tc, d), F32), pltpu.VMEM((2, tc, d), F32),
                        pltpu.SemaphoreType.DMA((2,))],
        compiler_params=_cparams(("arbitrary",)),
        name="moe_combine",
    )(pos, pos, ys, x, route)


def _attention_layer(x, seq, mix_gain, ffn_gain, rel_bias, w_qkv, q_gain, k_gain, sink, w_o,
                     w_gate, w_up, w_down):
    t, d = x.shape
    n_heads = sink.shape[0]
    n_kv_heads = (w_qkv.shape[1] // HEAD_DIM - n_heads) // 2
    head_gain = jnp.concatenate([
        jnp.tile(q_gain * HEAD_DIM ** -0.5, n_heads),
        jnp.tile(k_gain, n_kv_heads),
        jnp.ones((n_kv_heads * HEAD_DIM,), F32)]).reshape(1, -1)
    qkv = _norm_matmul(x, (mix_gain[:, None] * w_qkv).astype(BF16), head_gain,
                       normed_heads=n_heads + n_kv_heads, out_dtype=BF16,
                       tm=_tile(t, 512), chunk=_tile(w_qkv.shape[1], 1024))
    attn = _attention(qkv, _band_bias(rel_bias), sink, seq=seq, n_heads=n_heads,
                      n_kv_heads=n_kv_heads, tq=_tile(seq, 512))
    x, h = _proj_res_norm(attn, w_o.astype(BF16), x, ffn_gain, tm=_tile(t, 512))
    return _ffn(h, x, w_gate.astype(BF16), w_up.astype(BF16), w_down.astype(BF16),
                tm=_tile(t, 1024), tf=_tile(w_gate.shape[1], 512), tn=_tile(d, 512))


def _pool_moe_layer(x, seq, mix_gain, ffn_gain, w_in, w_group, scale, w_out, router_w, router_b,
                    w_gate, w_up, w_down):
    t, d = x.shape
    n_experts = router_w.shape[1]
    u = _norm_matmul(x, (mix_gain[:, None] * w_in).astype(BF16), jnp.ones((1, d), F32),
                     normed_heads=0, out_dtype=F32, tm=_tile(t, 512), chunk=_tile(d, 1024))
    rw_hi = router_w.astype(BF16)
    rw_lo = (router_w - rw_hi.astype(F32)).astype(BF16)
    rw = (jnp.zeros((d, LANES), BF16).at[:, :n_experts].set(rw_hi)
          .at[:, n_experts:2 * n_experts].set(rw_lo))
    rb = jnp.zeros((1, LANES), F32).at[0, :n_experts].set(router_b)
    x, packed, route = _pool_tail(u, w_group.astype(BF16), scale, w_out.astype(BF16), x, ffn_gain,
                                  rw, rb, seq=seq, n_experts=n_experts, tm=_tile(seq, 256))
    tm = MOE_ROW_TILE
    n_tiles = TOP_K * t // tm + n_experts
    e_flat = route[:, :TOP_K].astype(I32).reshape(-1)
    pos, tile_expert, n_used, tile_rows, src = _routing_plan(
        e_flat, n_experts=n_experts, tm=tm, n_tiles=n_tiles)
    ys = _moe(tile_expert, n_used, tile_rows, src, packed, w_gate, w_up, w_down, tm=tm,
              tf=_tile(w_gate.shape[2], 256), tn=_tile(d, 256), n_tiles=n_tiles)
    return _combine(pos, ys, x, route, tc=_tile(t, 256))


def kernel(x, mix_norm, ffn_norm, rel_bias, attn_w_qkv, attn_q_gain, attn_k_gain, attn_sink,
           attn_w_o, ffn_w_gate, ffn_w_up, ffn_w_down, pool_w_in, pool_w_group, pool_scale,
           pool_w_out, moe_router_w, moe_router_b, moe_w_gate, moe_w_up, moe_w_down):
    b, s, d = x.shape
    y = x.reshape(b * s, d)
    for i in range(mix_norm.shape[0]):
        j = i // 2
        if i % 2 == 0:
            y = _attention_layer(y, s, mix_norm[i], ffn_norm[i], rel_bias, attn_w_qkv[j],
                                 attn_q_gain[j], attn_k_gain[j], attn_sink[j], attn_w_o[j],
                                 ffn_w_gate[j], ffn_w_up[j], ffn_w_down[j])
        else:
            y = _pool_moe_layer(y, s, mix_norm[i], ffn_norm[i], pool_w_in[j], pool_w_group[j],
                                pool_scale[j], pool_w_out[j], moe_router_w[j], moe_router_b[j],
                                moe_w_gate[j], moe_w_up[j], moe_w_down[j])
    return y.reshape(b, s, d)
```

```python
import functools
import math

import jax
import jax.numpy as jnp
from jax import lax
from jax.experimental import pallas as pl
from jax.experimental.pallas import tpu as pltpu

F32 = jnp.float32
BF16 = jnp.bfloat16
U32 = jnp.uint32
I32 = jnp.int32

EPS = 1e-6
NEG_INF = -1e30

LANES = 128
VMEM_LIMIT_BYTES = 56 * 1024 * 1024

HEAD_DIM = 128
GROUP = 4
WINDOW = 128
BLOCK = 128
KEY_SPAN = BLOCK + 2 * WINDOW
NUM_BUCKETS = 32
MAX_DISTANCE = 128
POOL_WINDOWS = (2, 4, 8, 16)
POOL_HALO = 8
TOP_K = 2
MOE_ROW_TILE = 1024
MOE_ROW_BLOCKS = 4


def _cparams(semantics):
    return pltpu.CompilerParams(dimension_semantics=semantics,
                                vmem_limit_bytes=VMEM_LIMIT_BYTES)


def _rms(x, gain):
    ms = jnp.mean(x * x, axis=-1, keepdims=True)
    return x * lax.rsqrt(ms + EPS) * gain


def _tile(n, want):
    want = min(want, n)
    while n % want:
        want //= 2
    return want


def _norm_matmul_kernel(x_ref, w_ref, hg_ref, o_ref, *, normed_heads, chunk):
    x = x_ref[...]
    inv = lax.rsqrt(jnp.mean(x * x, axis=-1, keepdims=True) + EPS)
    xb = x.astype(BF16)
    for c0 in range(0, o_ref.shape[1], chunk):
        acc = jnp.dot(xb, w_ref[:, c0:c0 + chunk], preferred_element_type=F32) * inv
        for h0 in range(c0, c0 + chunk, HEAD_DIM):
            sl = slice(h0, h0 + HEAD_DIM)
            piece = acc[:, h0 - c0:h0 - c0 + HEAD_DIM]
            if h0 // HEAD_DIM < normed_heads:
                piece = _rms(piece, hg_ref[:, sl])
            o_ref[:, sl] = piece.astype(o_ref.dtype)


def _norm_matmul(x, w, head_gain, *, normed_heads, out_dtype, tm, chunk):
    t, d = x.shape
    n = w.shape[1]
    return pl.pallas_call(
        functools.partial(_norm_matmul_kernel, normed_heads=normed_heads, chunk=chunk),
        out_shape=jax.ShapeDtypeStruct((t, n), out_dtype),
        grid=(t // tm,),
        in_specs=[
            pl.BlockSpec((tm, d), lambda i: (i, 0)),
            pl.BlockSpec((d, n), lambda i: (0, 0)),
            pl.BlockSpec((1, n), lambda i: (0, 0)),
        ],
        out_specs=pl.BlockSpec((tm, n), lambda i: (i, 0)),
        compiler_params=_cparams(("parallel",)),
        name="norm_matmul",
    )(x, w, head_gain)


def _attn_kernel(sink_ref, q_ref, kv_ref, kvp_ref, kvn_ref, bias_ref, o_ref, *,
                 n_kv_heads, steps_per_seq):
    i = pl.program_id(0) % steps_per_seq
    tq = q_ref.shape[0]
    nqb = tq // BLOCK
    kw = n_kv_heads * HEAD_DIM
    prev_penalty = jnp.where(i > 0, 0.0, NEG_INF)
    next_penalty = jnp.where(i < steps_per_seq - 1, 0.0, NEG_INF)
    col = lax.broadcasted_iota(I32, (GROUP * BLOCK, KEY_SPAN), 1)

    for h in range(n_kv_heads):
        ksl = slice(h * HEAD_DIM, (h + 1) * HEAD_DIM)
        vsl = slice(kw + h * HEAD_DIM, kw + (h + 1) * HEAD_DIM)
        bias = bias_ref[h * GROUP:(h + 1) * GROUP].reshape(GROUP * BLOCK, KEY_SPAN)
        for qb in range(nqb):
            rows = slice(qb * BLOCK, (qb + 1) * BLOCK)
            prev_rows = slice((qb - 1) * BLOCK, qb * BLOCK)
            next_rows = slice((qb + 1) * BLOCK, (qb + 2) * BLOCK)
            if qb == 0:
                k_prev, v_prev = kvp_ref[:, ksl], kvp_ref[:, vsl]
            else:
                k_prev, v_prev = kv_ref[prev_rows, ksl], kv_ref[prev_rows, vsl]
            if qb == nqb - 1:
                k_next, v_next = kvn_ref[:, ksl], kvn_ref[:, vsl]
            else:
                k_next, v_next = kv_ref[next_rows, ksl], kv_ref[next_rows, vsl]
            k3 = jnp.concatenate([k_prev, kv_ref[rows, ksl], k_next], axis=0)
            v3 = jnp.concatenate([v_prev, kv_ref[rows, vsl], v_next], axis=0)
            q4 = jnp.concatenate(
                [q_ref[rows, (h * GROUP + g) * HEAD_DIM:(h * GROUP + g + 1) * HEAD_DIM]
                 for g in range(GROUP)], axis=0)
            s = lax.dot_general(q4, k3, (((1,), (1,)), ((), ())),
                                preferred_element_type=F32) + bias
            if qb == 0:
                s = s + jnp.where(col < WINDOW, prev_penalty, 0.0)
            if qb == nqb - 1:
                s = s + jnp.where(col >= WINDOW + BLOCK, next_penalty, 0.0)
            outs = []
            for g in range(GROUP):
                sg = s[g * BLOCK:(g + 1) * BLOCK]
                sink = sink_ref[h * GROUP + g]
                m = jnp.maximum(jnp.max(sg, axis=-1, keepdims=True), sink)
                p = jnp.exp(sg - m)
                denom = jnp.sum(p, axis=-1, keepdims=True) + jnp.exp(sink - m)
                pv = jnp.dot(p.astype(BF16), v3, preferred_element_type=F32)
                outs.append(pv / denom)
            for g in range(GROUP):
                o_ref[rows, (h * GROUP + g) * HEAD_DIM:(h * GROUP + g + 1) * HEAD_DIM] = (
                    outs[g].astype(o_ref.dtype))


def _attention(qkv, bias, sink, *, seq, n_heads, n_kv_heads, tq):
    t = qkv.shape[0]
    dq = n_heads * HEAD_DIM
    dkv = 2 * n_kv_heads * HEAD_DIM
    assert dq % dkv == 0
    kv_col = dq // dkv
    steps_per_seq = seq // tq
    blocks_per_step = tq // BLOCK
    last_block = t // BLOCK - 1
    return pl.pallas_call(
        functools.partial(_attn_kernel, n_kv_heads=n_kv_heads, steps_per_seq=steps_per_seq),
        out_shape=jax.ShapeDtypeStruct((t, dq), BF16),
        grid=(t // tq,),
        in_specs=[
            pl.BlockSpec(memory_space=pltpu.SMEM),
            pl.BlockSpec((tq, dq), lambda r: (r, 0)),
            pl.BlockSpec((tq, dkv), lambda r: (r, kv_col)),
            pl.BlockSpec((BLOCK, dkv),
                         lambda r: (jnp.maximum(r * blocks_per_step - 1, 0), kv_col)),
            pl.BlockSpec((BLOCK, dkv),
                         lambda r: (jnp.minimum((r + 1) * blocks_per_step, last_block), kv_col)),
            pl.BlockSpec((n_heads, BLOCK, KEY_SPAN), lambda r: (0, 0, 0)),
        ],
        out_specs=pl.BlockSpec((tq, dq), lambda r: (r, 0)),
        compiler_params=_cparams(("parallel",)),
        name="window_attention",
    )(sink, qkv, qkv, qkv, qkv, bias)


def _t5_bucket(rel):
    half = NUM_BUCKETS // 2
    ret = jnp.where(rel > 0, half, 0)
    n = jnp.abs(rel)
    max_exact = half // 2
    nf = jnp.maximum(n, 1).astype(F32)
    large = max_exact + (jnp.log(nf / max_exact) / math.log(MAX_DISTANCE / max_exact)
                         * (half - max_exact)).astype(I32)
    large = jnp.minimum(large, half - 1)
    return ret + jnp.where(n < max_exact, n, large)


def _band_bias(rel_bias):
    qi = jnp.arange(BLOCK)[:, None]
    kj = jnp.arange(KEY_SPAN)[None, :]
    rel = kj - WINDOW - qi
    onehot = (_t5_bucket(rel)[:, :, None] == jnp.arange(NUM_BUCKETS)[None, None, :]).astype(F32)
    bias = jnp.einsum("qkb,bh->hqk", onehot, rel_bias.astype(F32),
                      precision=lax.Precision.HIGHEST)
    return jnp.where((jnp.abs(rel) <= WINDOW)[None], bias, NEG_INF)


def _proj_res_norm_kernel(a_ref, w_ref, x_ref, g_ref, xo_ref, ho_ref):
    xo = x_ref[...] + jnp.dot(a_ref[...], w_ref[...], preferred_element_type=F32)
    xo_ref[...] = xo
    ho_ref[...] = _rms(xo, g_ref[...]).astype(BF16)


def _proj_res_norm(a, w, x, gain, *, tm):
    t, d = x.shape
    k = a.shape[1]
    return pl.pallas_call(
        _proj_res_norm_kernel,
        out_shape=(jax.ShapeDtypeStruct((t, d), F32), jax.ShapeDtypeStruct((t, d), BF16)),
        grid=(t // tm,),
        in_specs=[
            pl.BlockSpec((tm, k), lambda i: (i, 0)),
            pl.BlockSpec((k, d), lambda i: (0, 0)),
            pl.BlockSpec((tm, d), lambda i: (i, 0)),
            pl.BlockSpec((1, d), lambda i: (0, 0)),
        ],
        out_specs=(pl.BlockSpec((tm, d), lambda i: (i, 0)),
                   pl.BlockSpec((tm, d), lambda i: (i, 0))),
        compiler_params=_cparams(("parallel",)),
        name="proj_res_norm",
    )(a, w, x, gain.reshape(1, d))


def _swiglu_act(h, wg, wu):
    g = jnp.dot(h, wg.astype(BF16), preferred_element_type=F32)
    u = jnp.dot(h, wu.astype(BF16), preferred_element_type=F32)
    return (g * jax.nn.sigmoid(g) * u).astype(BF16)


def _down_proj(a_scr, wd, rows):
    a = jnp.concatenate([a_scr[k, :rows] for k in range(a_scr.shape[0])], axis=1)
    return jnp.dot(a, wd.astype(BF16), preferred_element_type=F32)


def _ffn_kernel(h_ref, x_ref, wg_ref, wu_ref, wd_ref, o_ref, a_scr):
    s = pl.program_id(1)
    nf = a_scr.shape[0]

    @pl.when(s < nf)
    def _():
        a_scr[s] = _swiglu_act(h_ref[...], wg_ref[...], wu_ref[...])

    @pl.when(s >= nf)
    def _():
        o_ref[...] = x_ref[...] + _down_proj(a_scr, wd_ref[...], o_ref.shape[0])


def _ffn(h, x, wg, wu, wd, *, tm, tf, tn):
    t, d = x.shape
    dff = wg.shape[1]
    nf = dff // tf
    nn = d // tn
    up_map = lambda i, s: (0, jnp.minimum(s, nf - 1))
    out_map = lambda i, s: (i, jnp.maximum(s - nf, 0))
    return pl.pallas_call(
        _ffn_kernel,
        out_shape=jax.ShapeDtypeStruct((t, d), F32),
        grid=(t // tm, nf + nn),
        in_specs=[
            pl.BlockSpec((tm, d), lambda i, s: (i, 0)),
            pl.BlockSpec((tm, tn), out_map),
            pl.BlockSpec((d, tf), up_map),
            pl.BlockSpec((d, tf), up_map),
            pl.BlockSpec((dff, tn), lambda i, s: (0, jnp.maximum(s - nf, 0))),
        ],
        out_specs=pl.BlockSpec((tm, tn), out_map),
        scratch_shapes=[pltpu.VMEM((nf, tm, tf), BF16)],
        compiler_params=_cparams(("parallel", "arbitrary")),
        name="dense_swiglu",
    )(h, x, wg, wu, wd)


def _pack_bf16_pairs(lo, hi):
    lo_bits = lax.bitcast_convert_type(lo.astype(BF16).astype(F32), U32)
    hi_bits = lax.bitcast_convert_type(hi.astype(BF16).astype(F32), U32)
    return (lo_bits >> 16) | hi_bits


def _unpack_bf16_pairs(words):
    lo = lax.bitcast_convert_type(words << 16, F32).astype(BF16)
    hi = lax.bitcast_convert_type(words & jnp.uint32(0xFFFF0000), F32).astype(BF16)
    return lo, hi


def _pool_tail_kernel(u_ref, up_ref, un_ref, wgrp_ref, scale_ref, wout_ref, x_ref, g_ref,
                      rw_ref, rb_ref, xo_ref, hp_ref, route_ref, *, seq, n_experts):
    tm, d = x_ref.shape
    gd = d // len(POOL_WINDOWS)
    ext = tm + 2 * POOL_HALO
    start = (pl.program_id(0) * tm) % seq
    pos = start + lax.broadcasted_iota(I32, (tm, 1), 0)

    ys = []
    for gi, win in enumerate(POOL_WINDOWS):
        cs = slice(gi * gd, (gi + 1) * gd)
        u = u_ref[:, cs]
        up = jnp.where(start > 0, up_ref[:, cs], 0.0)
        un = jnp.where(start + tm < seq, un_ref[:, cs], 0.0)
        a = jnp.concatenate([up, u, un], axis=0)
        a = a + pltpu.roll(a, 1, axis=0)
        w = 2
        while w < win:
            a = pltpu.roll(a, w // 2, axis=0) + pltpu.roll(a, ext - w // 2, axis=0)
            w *= 2
        half = win // 2
        count = (jnp.minimum(pos + half, seq) - jnp.maximum(pos - half, 0)).astype(F32)
        diff = a[POOL_HALO:POOL_HALO + tm] / count - u
        y = jnp.dot(diff.astype(BF16), wgrp_ref[gi], preferred_element_type=F32)
        ys.append((y * scale_ref[:, cs]).astype(BF16))
    y = jnp.concatenate(ys, axis=1)
    xo = x_ref[...] + jnp.dot(y, wout_ref[...], preferred_element_type=F32)
    xo_ref[...] = xo

    h = _rms(xo, g_ref[...])
    hp_ref[...] = _pack_bf16_pairs(h[:, :d // 2], h[:, d // 2:])

    h_hi = h.astype(BF16)
    h_lo = (h - h_hi.astype(F32)).astype(BF16)
    prod = jnp.dot(jnp.concatenate([h_hi, h_lo], axis=0), rw_ref[...],
                   preferred_element_type=F32)
    part = prod[:tm] + prod[tm:]
    logits = part + pltpu.roll(part, LANES - n_experts, axis=1) + rb_ref[...]
    lane = lax.broadcasted_iota(I32, logits.shape, 1)
    logits = jnp.where(lane < n_experts, logits, -jnp.inf)
    lane_f = lane.astype(F32)
    m1 = jnp.max(logits, axis=-1, keepdims=True)
    i1 = jnp.min(jnp.where(logits == m1, lane_f, float(LANES)), axis=-1, keepdims=True)
    rest = jnp.where(lane_f == i1, -jnp.inf, logits)
    m2 = jnp.max(rest, axis=-1, keepdims=True)
    i2 = jnp.min(jnp.where(rest == m2, lane_f, float(LANES)), axis=-1, keepdims=True)
    e2 = jnp.exp(m2 - m1)
    w1 = 1.0 / (1.0 + e2)
    w2 = e2 / (1.0 + e2)
    route_ref[...] = jnp.where(
        lane == 0, i1, jnp.where(lane == 1, i2,
                                 jnp.where(lane == 2, w1, jnp.where(lane == 3, w2, 0.0))))


def _pool_tail(u, wgrp, scale, wout, x, gain, rw, rb, *, seq, n_experts, tm):
    t, d = x.shape
    ng, gd, _ = wgrp.shape
    halo_blocks = tm // POOL_HALO
    last_halo = t // POOL_HALO - 1
    return pl.pallas_call(
        functools.partial(_pool_tail_kernel, seq=seq, n_experts=n_experts),
        out_shape=(jax.ShapeDtypeStruct((t, d), F32),
                   jax.ShapeDtypeStruct((t, d // 2), U32),
                   jax.ShapeDtypeStruct((t, LANES), F32)),
        grid=(t // tm,),
        in_specs=[
            pl.BlockSpec((tm, d), lambda i: (i, 0)),
            pl.BlockSpec((POOL_HALO, d), lambda i: (jnp.maximum(i * halo_blocks - 1, 0), 0)),
            pl.BlockSpec((POOL_HALO, d),
                         lambda i: (jnp.minimum((i + 1) * halo_blocks, last_halo), 0)),
            pl.BlockSpec((ng, gd, gd), lambda i: (0, 0, 0)),
            pl.BlockSpec((1, d), lambda i: (0, 0)),
            pl.BlockSpec((d, d), lambda i: (0, 0)),
            pl.BlockSpec((tm, d), lambda i: (i, 0)),
            pl.BlockSpec((1, d), lambda i: (0, 0)),
            pl.BlockSpec((d, LANES), lambda i: (0, 0)),
            pl.BlockSpec((1, LANES), lambda i: (0, 0)),
        ],
        out_specs=(pl.BlockSpec((tm, d), lambda i: (i, 0)),
                   pl.BlockSpec((tm, d // 2), lambda i: (i, 0)),
                   pl.BlockSpec((tm, LANES), lambda i: (i, 0))),
        compiler_params=_cparams(("parallel",)),
        name="pool_tail",
    )(u, u, u, wgrp, scale.reshape(1, d), wout, x, gain.reshape(1, d), rw, rb)


def _routing_plan(e_flat, *, n_experts, tm, n_tiles):
    block = tm // MOE_ROW_BLOCKS
    experts = jnp.arange(n_experts, dtype=I32)
    onehot = (e_flat[:, None] == experts[None, :]).astype(I32)
    csum = jnp.cumsum(onehot, axis=0)
    rank = jnp.sum(onehot * csum, axis=1) - 1
    counts = csum[-1]
    tiles_per = (counts + tm - 1) // tm
    blocks_per = (counts + block - 1) // block
    safe_tiles = jnp.maximum(tiles_per, 1)
    cap_lo = (blocks_per // safe_tiles) * block
    cap_hi = cap_lo + block
    rem = blocks_per % safe_tiles
    tile_end = jnp.cumsum(tiles_per)
    first_tile = tile_end - tiles_per
    n_used = tile_end[-1]

    def split(k, hi, lo, n_hi):
        in_hi = k < n_hi * hi
        k_lo = k - n_hi * hi
        tile = jnp.where(in_hi, k // hi, n_hi + k_lo // jnp.maximum(lo, 1))
        row = jnp.where(in_hi, k % hi, k_lo % jnp.maximum(lo, 1))
        return tile, row

    per_row = lambda v: jnp.sum(onehot * v[None, :], axis=1)
    tile_in, row_in = split(rank, per_row(cap_hi), per_row(cap_lo), per_row(rem))
    pos = (per_row(first_tile) + tile_in) * tm + row_in

    tile_ids = jnp.arange(n_tiles, dtype=I32)
    tile_expert = jnp.sum((tile_ids[:, None] >= tile_end[None, :]).astype(I32), axis=1)
    last_expert = jnp.sum((n_used - 1 >= tile_end).astype(I32))
    tile_expert = jnp.minimum(tile_expert, last_expert).astype(I32)
    onehot_t = (tile_expert[:, None] == experts[None, :]).astype(I32)
    per_tile = lambda v: jnp.sum(onehot_t * v[None, :], axis=1)
    j = tile_ids - per_tile(first_tile)
    hi, lo, n_hi = per_tile(cap_hi), per_tile(cap_lo), per_tile(rem)
    start = jnp.where(j < n_hi, j * hi, n_hi * hi + (j - n_hi) * lo)
    cap = jnp.where(j < n_hi, hi, lo)
    tile_rows = jnp.where(tile_ids < n_used, jnp.clip(per_tile(counts) - start, 0, cap), 0)
    return pos.astype(I32), tile_expert, n_used.astype(I32), tile_rows.astype(I32)


def _dispatch_kernel(pos_ref, rows_ref, src_ref, dst_ref, sem, *, tm):
    i = pl.program_id(0)
    tc = src_ref.shape[0]
    unroll = 8

    def row_copy(src_row, dst_row):
        return pltpu.make_async_copy(src_ref.at[pl.ds(src_row, 1), :],
                                     dst_ref.at[pl.ds(dst_row, 1), :], sem)

    def issue(jo, carry):
        for ji in range(unroll):
            j = jo * unroll + ji
            for k in range(TOP_K):
                row_copy(j, pos_ref[0, 0, TOP_K * j + k]).start()
        return carry

    lax.fori_loop(0, tc // unroll, issue, 0)
    for k in range(TOP_K):
        pltpu.make_async_copy(src_ref, dst_ref.at[pl.ds(0, tc), :], sem).wait()

    @pl.when(i == pl.num_programs(0) - 1)
    def _():
        n_tiles = dst_ref.shape[0] // tm

        def fill_tile(tile, n_used):
            rows = rows_ref[tile]
            first = tile * tm + rows
            n_pad = jnp.where(rows > 0, tm - rows, 0)

            def fill(j, carry):
                row_copy(0, first + j).start()
                return carry

            def fill_wait(j, carry):
                row_copy(0, 0).wait()
                return carry

            lax.fori_loop(0, n_pad, fill, 0)
            lax.fori_loop(0, n_pad, fill_wait, 0)
            return n_used + (rows > 0).astype(I32)

        n_used = lax.fori_loop(0, n_tiles, fill_tile, jnp.int32(0))
        fill_rows = math.gcd(tm, tc)

        def fill_block(j, carry):
            cp = pltpu.make_async_copy(
                src_ref.at[pl.ds(0, fill_rows), :],
                dst_ref.at[pl.ds(pl.multiple_of(j * fill_rows, fill_rows), fill_rows), :], sem)
            cp.start()
            cp.wait()
            return carry

        lax.fori_loop(n_used * (tm // fill_rows), dst_ref.shape[0] // fill_rows, fill_block, 0)


def _dispatch(pos, tile_rows, packed, *, tc, tm):
    t, width = packed.shape
    n_steps = t // tc
    return pl.pallas_call(
        functools.partial(_dispatch_kernel, tm=tm),
        out_shape=jax.ShapeDtypeStruct((tile_rows.shape[0] * tm, width), packed.dtype),
        grid=(n_steps,),
        in_specs=[
            pl.BlockSpec((1, 1, TOP_K * tc), lambda i: (i, 0, 0), memory_space=pltpu.SMEM),
            pl.BlockSpec(memory_space=pltpu.SMEM),
            pl.BlockSpec((tc, width), lambda i: (i, 0)),
        ],
        out_specs=pl.BlockSpec(memory_space=pl.ANY),
        scratch_shapes=[pltpu.SemaphoreType.DMA(())],
        compiler_params=_cparams(("arbitrary",)),
        name="moe_dispatch",
    )(pos.reshape(n_steps, 1, TOP_K * tc), tile_rows, packed)


def _moe_kernel(te_ref, nu_ref, rows_ref, xs_ref, wg_ref, wu_ref, wd_ref, o_ref, x_scr, a_scr):
    del te_ref, nu_ref
    i = pl.program_id(0)
    s = pl.program_id(1)
    tm, d = x_scr.shape
    nf = a_scr.shape[0]
    block = tm // MOE_ROW_BLOCKS
    n_blocks = (rows_ref[i] + block - 1) // block

    @pl.when((n_blocks > 0) & (s == 0))
    def _():
        lo, hi = _unpack_bf16_pairs(xs_ref[...])
        x_scr[:, :d // 2] = lo
        x_scr[:, d // 2:] = hi

    for k in range(1, MOE_ROW_BLOCKS + 1):
        m = k * block

        @pl.when((n_blocks == k) & (s < nf))
        def _():
            a_scr[s, :m] = _swiglu_act(x_scr[:m], wg_ref[...], wu_ref[...])

        @pl.when((n_blocks == k) & (s >= nf))
        def _():
            o_ref[:m] = _down_proj(a_scr, wd_ref[...], m)
            if m < tm:
                o_ref[m:] = jnp.zeros((tm - m, o_ref.shape[1]), o_ref.dtype)

    @pl.when((n_blocks == 0) & (s >= nf))
    def _():
        o_ref[...] = jnp.zeros(o_ref.shape, o_ref.dtype)


def _moe(tile_expert, n_used, tile_rows, xs, wg, wu, wd, *, tm, tf, tn, n_tiles):
    n_experts, d, dff = wg.shape
    nf = dff // tf
    nn = d // tn

    def row_map(i, s, te, nu, rows):
        return (jnp.minimum(i, nu[0] - 1), 0)

    def up_map(i, s, te, nu, rows):
        return (te[i], 0, jnp.where(i < nu[0], jnp.minimum(s, nf - 1), nf - 1))

    def down_map(i, s, te, nu, rows):
        return (te[i], 0, jnp.where(i < nu[0], jnp.maximum(s - nf, 0), nn - 1))

    grid_spec = pltpu.PrefetchScalarGridSpec(
        num_scalar_prefetch=3,
        grid=(n_tiles, nf + nn),
        in_specs=[
            pl.BlockSpec((tm, d // 2), row_map),
            pl.BlockSpec((None, d, tf), up_map),
            pl.BlockSpec((None, d, tf), up_map),
            pl.BlockSpec((None, dff, tn), down_map),
        ],
        out_specs=pl.BlockSpec((tm, tn),
                               lambda i, s, te, nu, rows: (i, jnp.maximum(s - nf, 0))),
        scratch_shapes=[pltpu.VMEM((tm, d), BF16), pltpu.VMEM((nf, tm, tf), BF16)],
    )
    return pl.pallas_call(
        _moe_kernel,
        out_shape=jax.ShapeDtypeStruct((n_tiles * tm, d), F32),
        grid_spec=grid_spec,
        compiler_params=_cparams(("arbitrary", "arbitrary")),
        name="moe_grouped_swiglu",
    )(tile_expert, n_used.reshape(1), tile_rows, xs, wg, wu, wd)


def _combine_kernel(pos_ref, pos_next_ref, ys_ref, x_ref, route_ref, o_ref, buf_a, buf_b, sems):
    i = pl.program_id(0)
    tc = x_ref.shape[0]
    slot = i % 2
    unroll = 8

    def start_gather(p_ref, sl):
        def row_copy(src_row, dst_row, buf):
            return pltpu.make_async_copy(ys_ref.at[pl.ds(src_row, 1), :],
                                         buf.at[sl, pl.ds(dst_row, 1), :], sems.at[sl])

        def issue(jo, carry):
            for ji in range(unroll):
                j = jo * unroll + ji
                row_copy(p_ref[0, 0, TOP_K * j], j, buf_a).start()
                row_copy(p_ref[0, 0, TOP_K * j + 1], j, buf_b).start()
            return carry

        lax.fori_loop(0, tc // unroll, issue, 0)

    @pl.when(i == 0)
    def _():
        start_gather(pos_ref, 0)

    @pl.when(i + 1 < pl.num_programs(0))
    def _():
        start_gather(pos_next_ref, 1 - slot)

    for buf in (buf_a, buf_b):
        pltpu.make_async_copy(ys_ref.at[pl.ds(0, tc), :], buf.at[slot], sems.at[slot]).wait()

    w1 = route_ref[:, 2:3]
    w2 = route_ref[:, 3:4]
    o_ref[...] = x_ref[...] + w1 * buf_a[slot] + w2 * buf_b[slot]


def _combine(pos, ys, x, route, *, tc):
    t, d = x.shape
    n_steps = t // tc
    pos = pos.reshape(n_steps, 1, TOP_K * tc)
    pos_spec = lambda index_map: pl.BlockSpec((1, 1, TOP_K * tc), index_map,
                                              memory_space=pltpu.SMEM)
    return pl.pallas_call(
        _combine_kernel,
        out_shape=jax.ShapeDtypeStruct((t, d), F32),
        grid=(n_steps,),
        in_specs=[
            pos_spec(lambda i: (i, 0, 0)),
            pos_spec(lambda i: (jnp.minimum(i + 1, n_steps - 1), 0, 0)),
            pl.BlockSpec(memory_space=pl.ANY),
            pl.BlockSpec((tc, d), lambda i: (i, 0)),
            pl.BlockSpec((tc, LANES), lambda i: (i, 0)),
        ],
        out_specs=pl.BlockSpec((tc, d), lambda i: (i, 0)),
        scratch_shapes=[pltpu.VMEM((2, tc, d), F32), pltpu.VMEM((2, tc, d), F32),
                        pltpu.SemaphoreType.DMA((2,))],
        compiler_params=_cparams(("arbitrary",)),
        name="moe_combine",
    )(pos, pos, ys, x, route)


def _attention_layer(x, seq, mix_gain, ffn_gain, rel_bias, w_qkv, q_gain, k_gain, sink, w_o,
                     w_gate, w_up, w_down):
    t, d = x.shape
    n_heads = sink.shape[0]
    n_kv_heads = (w_qkv.shape[1] // HEAD_DIM - n_heads) // 2
    head_gain = jnp.concatenate([
        jnp.tile(q_gain * HEAD_DIM ** -0.5, n_heads),
        jnp.tile(k_gain, n_kv_heads),
        jnp.ones((n_kv_heads * HEAD_DIM,), F32)]).reshape(1, -1)
    qkv = _norm_matmul(x, (mix_gain[:, None] * w_qkv).astype(BF16), head_gain,
                       normed_heads=n_heads + n_kv_heads, out_dtype=BF16,
                       tm=_tile(t, 512), chunk=_tile(w_qkv.shape[1], 1024))
    attn = _attention(qkv, _band_bias(rel_bias), sink, seq=seq, n_heads=n_heads,
                      n_kv_heads=n_kv_heads, tq=_tile(seq, 512))
    x, h = _proj_res_norm(attn, w_o.astype(BF16), x, ffn_gain, tm=_tile(t, 512))
    return _ffn(h, x, w_gate.astype(BF16), w_up.astype(BF16), w_down.astype(BF16),
                tm=_tile(t, 1024), tf=_tile(w_gate.shape[1], 512), tn=_tile(d, 512))


def _pool_moe_layer(x, seq, mix_gain, ffn_gain, w_in, w_group, scale, w_out, router_w, router_b,
                    w_gate, w_up, w_down):
    t, d = x.shape
    n_experts = router_w.shape[1]
    u = _norm_matmul(x, (mix_gain[:, None] * w_in).astype(BF16), jnp.ones((1, d), F32),
                     normed_heads=0, out_dtype=F32, tm=_tile(t, 512), chunk=_tile(d, 1024))
    rw_hi = router_w.astype(BF16)
    rw_lo = (router_w - rw_hi.astype(F32)).astype(BF16)
    rw = (jnp.zeros((d, LANES), BF16).at[:, :n_experts].set(rw_hi)
          .at[:, n_experts:2 * n_experts].set(rw_lo))
    rb = jnp.zeros((1, LANES), F32).at[0, :n_experts].set(router_b)
    x, packed, route = _pool_tail(u, w_group.astype(BF16), scale, w_out.astype(BF16), x, ffn_gain,
                                  rw, rb, seq=seq, n_experts=n_experts, tm=_tile(seq, 512))
    tm = MOE_ROW_TILE
    n_tiles = TOP_K * t // tm + n_experts
    e_flat = route[:, :TOP_K].astype(I32).reshape(-1)
    pos, tile_expert, n_used, tile_rows = _routing_plan(
        e_flat, n_experts=n_experts, tm=tm, n_tiles=n_tiles)
    xs = _dispatch(pos, tile_rows, packed, tc=_tile(t, 1024), tm=tm)
    ys = _moe(tile_expert, n_used, tile_rows, xs, w_gate, w_up, w_down, tm=tm,
              tf=_tile(w_gate.shape[2], 256), tn=_tile(d, 256), n_tiles=n_tiles)
    return _combine(pos, ys, x, route, tc=_tile(t, 256))


def kernel(x, mix_norm, ffn_norm, rel_bias, attn_w_qkv, attn_q_gain, attn_k_gain, attn_sink,
           attn_w_o, ffn_w_gate, ffn_w_up, ffn_w_down, pool_w_in, pool_w_group, pool_scale,
           pool_w_out, moe_router_w, moe_router_b, moe_w_gate, moe_w_up, moe_w_down):
    b, s, d = x.shape
    y = x.reshape(b * s, d)
    for i in range(mix_norm.shape[0]):
        j = i // 2
        if i % 2 == 0:
            y = _attention_layer(y, s, mix_norm[i], ffn_norm[i], rel_bias, attn_w_qkv[j],
                                 attn_q_gain[j], attn_k_gain[j], attn_sink[j], attn_w_o[j],
                                 ffn_w_gate[j], ffn_w_up[j], ffn_w_down[j])
        else:
            y = _pool_moe_layer(y, s, mix_norm[i], ffn_norm[i], pool_w_in[j], pool_w_group[j],
                                pool_scale[j], pool_w_out[j], moe_router_w[j], moe_router_b[j],
                                moe_w_gate[j], moe_w_up[j], moe_w_down[j])
    return y.reshape(b, s, d)
```

```python
import functools
import math

import jax
import jax.numpy as jnp
from jax import lax
from jax.experimental import pallas as pl
from jax.experimental.pallas import tpu as pltpu

F32 = jnp.float32
BF16 = jnp.bfloat16
U32 = jnp.uint32
I32 = jnp.int32

EPS = 1e-6
NEG_INF = -1e30
LOG2_E = 1.4426950408889634

LANES = 128
VMEM_LIMIT_BYTES = 56 * 1024 * 1024

HEAD_DIM = 128
GROUP = 4
WINDOW = 128
BLOCK = 128
KEY_SPAN = BLOCK + 2 * WINDOW
NUM_BUCKETS = 32
MAX_DISTANCE = 128
POOL_WINDOWS = (2, 4, 8, 16)
POOL_HALO = 8
TOP_K = 2
MOE_ROW_TILE = 1024
MOE_ROW_BLOCKS = 4


def _cparams(semantics):
    return pltpu.CompilerParams(dimension_semantics=semantics,
                                vmem_limit_bytes=VMEM_LIMIT_BYTES)


def _rms(x, gain):
    ms = jnp.mean(x * x, axis=-1, keepdims=True)
    return x * lax.rsqrt(ms + EPS) * gain


def _tile(n, want):
    want = min(want, n)
    while n % want:
        want //= 2
    return want


def _norm_matmul_kernel(x_ref, w_ref, hg_ref, o_ref, *, normed_heads, chunk):
    x = x_ref[...]
    inv = lax.rsqrt(jnp.mean(x * x, axis=-1, keepdims=True) + EPS)
    xb = x.astype(BF16)
    for c0 in range(0, o_ref.shape[1], chunk):
        acc = jnp.dot(xb, w_ref[:, c0:c0 + chunk], preferred_element_type=F32) * inv
        for h0 in range(c0, c0 + chunk, HEAD_DIM):
            sl = slice(h0, h0 + HEAD_DIM)
            piece = acc[:, h0 - c0:h0 - c0 + HEAD_DIM]
            if h0 // HEAD_DIM < normed_heads:
                piece = _rms(piece, hg_ref[:, sl])
            o_ref[:, sl] = piece.astype(o_ref.dtype)


def _norm_matmul(x, w, head_gain, *, normed_heads, out_dtype, tm, chunk):
    t, d = x.shape
    n = w.shape[1]
    return pl.pallas_call(
        functools.partial(_norm_matmul_kernel, normed_heads=normed_heads, chunk=chunk),
        out_shape=jax.ShapeDtypeStruct((t, n), out_dtype),
        grid=(t // tm,),
        in_specs=[
            pl.BlockSpec((tm, d), lambda i: (i, 0)),
            pl.BlockSpec((d, n), lambda i: (0, 0)),
            pl.BlockSpec((1, n), lambda i: (0, 0)),
        ],
        out_specs=pl.BlockSpec((tm, n), lambda i: (i, 0)),
        compiler_params=_cparams(("parallel",)),
        name="norm_matmul",
    )(x, w, head_gain)


def _attn_kernel(sink_ref, q_ref, kv_ref, kvp_ref, kvn_ref, bias_ref, o_ref, *,
                 n_kv_heads, steps_per_seq):
    i = pl.program_id(0) % steps_per_seq
    tq = q_ref.shape[0]
    nqb = tq // BLOCK
    kw = n_kv_heads * HEAD_DIM
    prev_penalty = jnp.where(i > 0, 0.0, NEG_INF)
    next_penalty = jnp.where(i < steps_per_seq - 1, 0.0, NEG_INF)
    col = lax.broadcasted_iota(I32, (GROUP * BLOCK, KEY_SPAN), 1)

    for h in range(n_kv_heads):
        ksl = slice(h * HEAD_DIM, (h + 1) * HEAD_DIM)
        vsl = slice(kw + h * HEAD_DIM, kw + (h + 1) * HEAD_DIM)
        bias = bias_ref[h * GROUP:(h + 1) * GROUP].reshape(GROUP * BLOCK, KEY_SPAN)
        for qb in range(nqb):
            rows = slice(qb * BLOCK, (qb + 1) * BLOCK)
            prev_rows = slice((qb - 1) * BLOCK, qb * BLOCK)
            next_rows = slice((qb + 1) * BLOCK, (qb + 2) * BLOCK)
            if qb == 0:
                k_prev, v_prev = kvp_ref[:, ksl], kvp_ref[:, vsl]
            else:
                k_prev, v_prev = kv_ref[prev_rows, ksl], kv_ref[prev_rows, vsl]
            if qb == nqb - 1:
                k_next, v_next = kvn_ref[:, ksl], kvn_ref[:, vsl]
            else:
                k_next, v_next = kv_ref[next_rows, ksl], kv_ref[next_rows, vsl]
            k3 = jnp.concatenate([k_prev, kv_ref[rows, ksl], k_next], axis=0)
            v3 = jnp.concatenate([v_prev, kv_ref[rows, vsl], v_next], axis=0)
            q4 = jnp.concatenate(
                [q_ref[rows, (h * GROUP + g) * HEAD_DIM:(h * GROUP + g + 1) * HEAD_DIM]
                 for g in range(GROUP)], axis=0)
            s = lax.dot_general(q4, k3, (((1,), (1,)), ((), ())),
                                preferred_element_type=F32) + bias
            if qb == 0:
                s = s + jnp.where(col < WINDOW, prev_penalty, 0.0)
            if qb == nqb - 1:
                s = s + jnp.where(col >= WINDOW + BLOCK, next_penalty, 0.0)
            outs = []
            for g in range(GROUP):
                sg = s[g * BLOCK:(g + 1) * BLOCK]
                sink = sink_ref[h * GROUP + g]
                m = jnp.maximum(jnp.max(sg, axis=-1, keepdims=True), sink)
                p = jnp.exp2(sg - m)
                denom = jnp.sum(p, axis=-1, keepdims=True) + jnp.exp2(sink - m)
                pv = jnp.dot(p.astype(BF16), v3, preferred_element_type=F32)
                outs.append(pv / denom)
            for g in range(GROUP):
                o_ref[rows, (h * GROUP + g) * HEAD_DIM:(h * GROUP + g + 1) * HEAD_DIM] = (
                    outs[g].astype(o_ref.dtype))


def _attention(qkv, bias, sink, *, seq, n_heads, n_kv_heads, tq):
    t = qkv.shape[0]
    dq = n_heads * HEAD_DIM
    dkv = 2 * n_kv_heads * HEAD_DIM
    assert dq % dkv == 0
    kv_col = dq // dkv
    steps_per_seq = seq // tq
    blocks_per_step = tq // BLOCK
    last_block = t // BLOCK - 1
    return pl.pallas_call(
        functools.partial(_attn_kernel, n_kv_heads=n_kv_heads, steps_per_seq=steps_per_seq),
        out_shape=jax.ShapeDtypeStruct((t, dq), BF16),
        grid=(t // tq,),
        in_specs=[
            pl.BlockSpec(memory_space=pltpu.SMEM),
            pl.BlockSpec((tq, dq), lambda r: (r, 0)),
            pl.BlockSpec((tq, dkv), lambda r: (r, kv_col)),
            pl.BlockSpec((BLOCK, dkv),
                         lambda r: (jnp.maximum(r * blocks_per_step - 1, 0), kv_col)),
            pl.BlockSpec((BLOCK, dkv),
                         lambda r: (jnp.minimum((r + 1) * blocks_per_step, last_block), kv_col)),
            pl.BlockSpec((n_heads, BLOCK, KEY_SPAN), lambda r: (0, 0, 0)),
        ],
        out_specs=pl.BlockSpec((tq, dq), lambda r: (r, 0)),
        compiler_params=_cparams(("parallel",)),
        name="window_attention",
    )(sink, qkv, qkv, qkv, qkv, bias)


def _t5_bucket(rel):
    half = NUM_BUCKETS // 2
    ret = jnp.where(rel > 0, half, 0)
    n = jnp.abs(rel)
    max_exact = half // 2
    nf = jnp.maximum(n, 1).astype(F32)
    large = max_exact + (jnp.log(nf / max_exact) / math.log(MAX_DISTANCE / max_exact)
                         * (half - max_exact)).astype(I32)
    large = jnp.minimum(large, half - 1)
    return ret + jnp.where(n < max_exact, n, large)


def _band_bias(rel_bias):
    qi = jnp.arange(BLOCK)[:, None]
    kj = jnp.arange(KEY_SPAN)[None, :]
    rel = kj - WINDOW - qi
    onehot = (_t5_bucket(rel)[:, :, None] == jnp.arange(NUM_BUCKETS)[None, None, :]).astype(F32)
    bias = jnp.einsum("qkb,bh->hqk", onehot, rel_bias.astype(F32),
                      precision=lax.Precision.HIGHEST)
    return jnp.where((jnp.abs(rel) <= WINDOW)[None], bias, NEG_INF)


def _proj_res_norm_kernel(a_ref, w_ref, x_ref, g_ref, xo_ref, ho_ref):
    xo = x_ref[...] + jnp.dot(a_ref[...], w_ref[...], preferred_element_type=F32)
    xo_ref[...] = xo
    ho_ref[...] = _rms(xo, g_ref[...]).astype(BF16)


def _proj_res_norm(a, w, x, gain, *, tm):
    t, d = x.shape
    k = a.shape[1]
    return pl.pallas_call(
        _proj_res_norm_kernel,
        out_shape=(jax.ShapeDtypeStruct((t, d), F32), jax.ShapeDtypeStruct((t, d), BF16)),
        grid=(t // tm,),
        in_specs=[
            pl.BlockSpec((tm, k), lambda i: (i, 0)),
            pl.BlockSpec((k, d), lambda i: (0, 0)),
            pl.BlockSpec((tm, d), lambda i: (i, 0)),
            pl.BlockSpec((1, d), lambda i: (0, 0)),
        ],
        out_specs=(pl.BlockSpec((tm, d), lambda i: (i, 0)),
                   pl.BlockSpec((tm, d), lambda i: (i, 0))),
        compiler_params=_cparams(("parallel",)),
        name="proj_res_norm",
    )(a, w, x, gain.reshape(1, d))


def _swiglu_act(h, wg, wu):
    g = jnp.dot(h, wg.astype(BF16), preferred_element_type=F32)
    u = jnp.dot(h, wu.astype(BF16), preferred_element_type=F32)
    return (g * jax.nn.sigmoid(g) * u).astype(BF16)


def _down_proj(a_scr, wd, rows):
    a = jnp.concatenate([a_scr[k, :rows] for k in range(a_scr.shape[0])], axis=1)
    return jnp.dot(a, wd.astype(BF16), preferred_element_type=F32)


def _ffn_kernel(h_ref, x_ref, wg_ref, wu_ref, wd_ref, o_ref, a_scr):
    s = pl.program_id(1)
    nf = a_scr.shape[0]

    @pl.when(s < nf)
    def _():
        a_scr[s] = _swiglu_act(h_ref[...], wg_ref[...], wu_ref[...])

    @pl.when(s >= nf)
    def _():
        o_ref[...] = x_ref[...] + _down_proj(a_scr, wd_ref[...], o_ref.shape[0])


def _ffn(h, x, wg, wu, wd, *, tm, tf, tn):
    t, d = x.shape
    dff = wg.shape[1]
    nf = dff // tf
    nn = d // tn
    up_map = lambda i, s: (0, jnp.minimum(s, nf - 1))
    out_map = lambda i, s: (i, jnp.maximum(s - nf, 0))
    return pl.pallas_call(
        _ffn_kernel,
        out_shape=jax.ShapeDtypeStruct((t, d), F32),
        grid=(t // tm, nf + nn),
        in_specs=[
            pl.BlockSpec((tm, d), lambda i, s: (i, 0)),
            pl.BlockSpec((tm, tn), out_map),
            pl.BlockSpec((d, tf), up_map),
            pl.BlockSpec((d, tf), up_map),
            pl.BlockSpec((dff, tn), lambda i, s: (0, jnp.maximum(s - nf, 0))),
        ],
        out_specs=pl.BlockSpec((tm, tn), out_map),
        scratch_shapes=[pltpu.VMEM((nf, tm, tf), BF16)],
        compiler_params=_cparams(("parallel", "arbitrary")),
        name="dense_swiglu",
    )(h, x, wg, wu, wd)


def _pack_bf16_pairs(lo, hi):
    lo_bits = lax.bitcast_convert_type(lo.astype(BF16).astype(F32), U32)
    hi_bits = lax.bitcast_convert_type(hi.astype(BF16).astype(F32), U32)
    return (lo_bits >> 16) | hi_bits


def _unpack_bf16_pairs(words):
    lo = lax.bitcast_convert_type(words << 16, F32).astype(BF16)
    hi = lax.bitcast_convert_type(words & jnp.uint32(0xFFFF0000), F32).astype(BF16)
    return lo, hi


def _pool_tail_kernel(u_ref, up_ref, un_ref, wgrp_ref, scale_ref, wout_ref, x_ref, g_ref,
                      rw_ref, rb_ref, xo_ref, hp_ref, route_ref, *, seq, n_experts):
    tm, d = x_ref.shape
    gd = d // len(POOL_WINDOWS)
    ext = tm + 2 * POOL_HALO
    start = (pl.program_id(0) * tm) % seq
    pos = start + lax.broadcasted_iota(I32, (tm, 1), 0)

    ys = []
    for gi, win in enumerate(POOL_WINDOWS):
        cs = slice(gi * gd, (gi + 1) * gd)
        u = u_ref[:, cs]
        up = jnp.where(start > 0, up_ref[:, cs], 0.0)
        un = jnp.where(start + tm < seq, un_ref[:, cs], 0.0)
        a = jnp.concatenate([up, u, un], axis=0)
        a = a + pltpu.roll(a, 1, axis=0)
        w = 2
        while w < win:
            a = pltpu.roll(a, w // 2, axis=0) + pltpu.roll(a, ext - w // 2, axis=0)
            w *= 2
        half = win // 2
        count = (jnp.minimum(pos + half, seq) - jnp.maximum(pos - half, 0)).astype(F32)
        diff = a[POOL_HALO:POOL_HALO + tm] / count - u
        y = jnp.dot(diff.astype(BF16), wgrp_ref[gi], preferred_element_type=F32)
        ys.append((y * scale_ref[:, cs]).astype(BF16))
    y = jnp.concatenate(ys, axis=1)
    xo = x_ref[...] + jnp.dot(y, wout_ref[...], preferred_element_type=F32)
    xo_ref[...] = xo

    h = _rms(xo, g_ref[...])
    hp_ref[...] = _pack_bf16_pairs(h[:, :d // 2], h[:, d // 2:])

    h_hi = h.astype(BF16)
    h_lo = (h - h_hi.astype(F32)).astype(BF16)
    prod = jnp.dot(jnp.concatenate([h_hi, h_lo], axis=0), rw_ref[...],
                   preferred_element_type=F32)
    part = prod[:tm] + prod[tm:]
    logits = part + pltpu.roll(part, LANES - n_experts, axis=1) + rb_ref[...]
    lane = lax.broadcasted_iota(I32, logits.shape, 1)
    logits = jnp.where(lane < n_experts, logits, -jnp.inf)
    lane_f = lane.astype(F32)
    m1 = jnp.max(logits, axis=-1, keepdims=True)
    i1 = jnp.min(jnp.where(logits == m1, lane_f, float(LANES)), axis=-1, keepdims=True)
    rest = jnp.where(lane_f == i1, -jnp.inf, logits)
    m2 = jnp.max(rest, axis=-1, keepdims=True)
    i2 = jnp.min(jnp.where(rest == m2, lane_f, float(LANES)), axis=-1, keepdims=True)
    e2 = jnp.exp(m2 - m1)
    w1 = 1.0 / (1.0 + e2)
    w2 = e2 / (1.0 + e2)
    route_ref[...] = jnp.where(
        lane == 0, i1, jnp.where(lane == 1, i2,
                                 jnp.where(lane == 2, w1, jnp.where(lane == 3, w2, 0.0))))


def _pool_tail(u, wgrp, scale, wout, x, gain, rw, rb, *, seq, n_experts, tm):
    t, d = x.shape
    ng, gd, _ = wgrp.shape
    halo_blocks = tm // POOL_HALO
    last_halo = t // POOL_HALO - 1
    return pl.pallas_call(
        functools.partial(_pool_tail_kernel, seq=seq, n_experts=n_experts),
        out_shape=(jax.ShapeDtypeStruct((t, d), F32),
                   jax.ShapeDtypeStruct((t, d // 2), U32),
                   jax.ShapeDtypeStruct((t, LANES), F32)),
        grid=(t // tm,),
        in_specs=[
            pl.BlockSpec((tm, d), lambda i: (i, 0)),
            pl.BlockSpec((POOL_HALO, d), lambda i: (jnp.maximum(i * halo_blocks - 1, 0), 0)),
            pl.BlockSpec((POOL_HALO, d),
                         lambda i: (jnp.minimum((i + 1) * halo_blocks, last_halo), 0)),
            pl.BlockSpec((ng, gd, gd), lambda i: (0, 0, 0)),
            pl.BlockSpec((1, d), lambda i: (0, 0)),
            pl.BlockSpec((d, d), lambda i: (0, 0)),
            pl.BlockSpec((tm, d), lambda i: (i, 0)),
            pl.BlockSpec((1, d), lambda i: (0, 0)),
            pl.BlockSpec((d, LANES), lambda i: (0, 0)),
            pl.BlockSpec((1, LANES), lambda i: (0, 0)),
        ],
        out_specs=(pl.BlockSpec((tm, d), lambda i: (i, 0)),
                   pl.BlockSpec((tm, d // 2), lambda i: (i, 0)),
                   pl.BlockSpec((tm, LANES), lambda i: (i, 0))),
        compiler_params=_cparams(("parallel",)),
        name="pool_tail",
    )(u, u, u, wgrp, scale.reshape(1, d), wout, x, gain.reshape(1, d), rw, rb)


def _routing_plan(e_flat, *, n_experts, tm, n_tiles):
    block = tm // MOE_ROW_BLOCKS
    experts = jnp.arange(n_experts, dtype=I32)
    onehot = (e_flat[:, None] == experts[None, :]).astype(I32)
    csum = jnp.cumsum(onehot, axis=0)
    rank = jnp.sum(onehot * csum, axis=1) - 1
    counts = csum[-1]
    tiles_per = (counts + tm - 1) // tm
    blocks_per = (counts + block - 1) // block
    safe_tiles = jnp.maximum(tiles_per, 1)
    cap_lo = (blocks_per // safe_tiles) * block
    cap_hi = cap_lo + block
    rem = blocks_per % safe_tiles
    tile_end = jnp.cumsum(tiles_per)
    first_tile = tile_end - tiles_per
    n_used = tile_end[-1]

    tile_ids = jnp.arange(n_tiles, dtype=I32)
    used = tile_ids < n_used
    tile_expert = jnp.sum((tile_ids[:, None] >= tile_end[None, :]).astype(I32), axis=1)
    last_expert = jnp.sum((n_used - 1 >= tile_end).astype(I32))
    tile_expert = jnp.minimum(tile_expert, last_expert).astype(I32)
    onehot_t = (tile_expert[:, None] == experts[None, :]).astype(I32)
    per_tile = lambda v: jnp.sum(onehot_t * v[None, :], axis=1)
    j = tile_ids - per_tile(first_tile)
    hi, lo, n_hi = per_tile(cap_hi), per_tile(cap_lo), per_tile(rem)
    start = jnp.where(j < n_hi, j * hi, n_hi * hi + (j - n_hi) * lo)
    cap = jnp.where(j < n_hi, hi, lo)
    tile_rows = jnp.where(used, jnp.clip(per_tile(counts) - start, 0, cap), 0)
    reaches = ((e_flat[:, None] == tile_expert[None, :]) & used[None, :]
               & (rank[:, None] >= start[None, :]))
    tile_of = jnp.max(jnp.where(reaches, tile_ids[None, :], 0), axis=1)
    start_of = jnp.max(jnp.where(reaches, start[None, :], 0), axis=1)
    pos = tile_of * tm + rank - start_of
    return pos.astype(I32), tile_expert, n_used.astype(I32), tile_rows.astype(I32)


def _dispatch_kernel(pos_ref, rows_ref, src_ref, dst_ref, sem, *, tm):
    i = pl.program_id(0)
    tc = src_ref.shape[0]
    unroll = 8

    def row_copy(src_row, dst_row):
        return pltpu.make_async_copy(src_ref.at[pl.ds(src_row, 1), :],
                                     dst_ref.at[pl.ds(dst_row, 1), :], sem)

    def issue(jo, carry):
        for ji in range(unroll):
            j = jo * unroll + ji
            for k in range(TOP_K):
                row_copy(j, pos_ref[0, 0, TOP_K * j + k]).start()
        return carry

    lax.fori_loop(0, tc // unroll, issue, 0)
    for k in range(TOP_K):
        pltpu.make_async_copy(src_ref, dst_ref.at[pl.ds(0, tc), :], sem).wait()

    @pl.when(i == pl.num_programs(0) - 1)
    def _():
        n_tiles = dst_ref.shape[0] // tm

        def fill_tile(tile, n_used):
            rows = rows_ref[tile]
            first = tile * tm + rows
            n_pad = jnp.where(rows > 0, tm - rows, 0)

            def fill(j, carry):
                row_copy(0, first + j).start()
                return carry

            def fill_wait(j, carry):
                row_copy(0, 0).wait()
                return carry

            lax.fori_loop(0, n_pad, fill, 0)
            lax.fori_loop(0, n_pad, fill_wait, 0)
            return n_used + (rows > 0).astype(I32)

        n_used = lax.fori_loop(0, n_tiles, fill_tile, jnp.int32(0))
        fill_rows = math.gcd(tm, tc)

        def fill_block(j, carry):
            cp = pltpu.make_async_copy(
                src_ref.at[pl.ds(0, fill_rows), :],
                dst_ref.at[pl.ds(pl.multiple_of(j * fill_rows, fill_rows), fill_rows), :], sem)
            cp.start()
            cp.wait()
            return carry

        lax.fori_loop(n_used * (tm // fill_rows), dst_ref.shape[0] // fill_rows, fill_block, 0)


def _dispatch(pos, tile_rows, packed, *, tc, tm):
    t, width = packed.shape
    n_steps = t // tc
    return pl.pallas_call(
        functools.partial(_dispatch_kernel, tm=tm),
        out_shape=jax.ShapeDtypeStruct((tile_rows.shape[0] * tm, width), packed.dtype),
        grid=(n_steps,),
        in_specs=[
            pl.BlockSpec((1, 1, TOP_K * tc), lambda i: (i, 0, 0), memory_space=pltpu.SMEM),
            pl.BlockSpec(memory_space=pltpu.SMEM),
            pl.BlockSpec((tc, width), lambda i: (i, 0)),
        ],
        out_specs=pl.BlockSpec(memory_space=pl.ANY),
        scratch_shapes=[pltpu.SemaphoreType.DMA(())],
        compiler_params=_cparams(("arbitrary",)),
        name="moe_dispatch",
    )(pos.reshape(n_steps, 1, TOP_K * tc), tile_rows, packed)


def _moe_kernel(te_ref, nu_ref, rows_ref, xs_ref, wg_ref, wu_ref, wd_ref, o_ref, x_scr, a_scr):
    del te_ref, nu_ref
    i = pl.program_id(0)
    s = pl.program_id(1)
    tm, d = x_scr.shape
    nf = a_scr.shape[0]
    block = tm // MOE_ROW_BLOCKS
    n_blocks = (rows_ref[i] + block - 1) // block

    @pl.when((n_blocks > 0) & (s == 0))
    def _():
        lo, hi = _unpack_bf16_pairs(xs_ref[...])
        x_scr[:, :d // 2] = lo
        x_scr[:, d // 2:] = hi

    for k in range(1, MOE_ROW_BLOCKS + 1):
        m = k * block

        @pl.when((n_blocks == k) & (s < nf))
        def _():
            a_scr[s, :m] = _swiglu_act(x_scr[:m], wg_ref[...], wu_ref[...])

        @pl.when((n_blocks == k) & (s >= nf))
        def _():
            o_ref[:m] = _down_proj(a_scr, wd_ref[...], m)
            if m < tm:
                o_ref[m:] = jnp.zeros((tm - m, o_ref.shape[1]), o_ref.dtype)

    @pl.when((n_blocks == 0) & (s >= nf))
    def _():
        o_ref[...] = jnp.zeros(o_ref.shape, o_ref.dtype)


def _moe(tile_expert, n_used, tile_rows, xs, wg, wu, wd, *, tm, tf, tn, n_tiles):
    n_experts, d, dff = wg.shape
    nf = dff // tf
    nn = d // tn

    def row_map(i, s, te, nu, rows):
        return (jnp.minimum(i, nu[0] - 1), 0)

    def up_map(i, s, te, nu, rows):
        return (te[i], 0, jnp.where(i < nu[0], jnp.minimum(s, nf - 1), nf - 1))

    def down_map(i, s, te, nu, rows):
        return (te[i], 0, jnp.where(i < nu[0], jnp.maximum(s - nf, 0), nn - 1))

    grid_spec = pltpu.PrefetchScalarGridSpec(
        num_scalar_prefetch=3,
        grid=(n_tiles, nf + nn),
        in_specs=[
            pl.BlockSpec((tm, d // 2), row_map),
            pl.BlockSpec((None, d, tf), up_map),
            pl.BlockSpec((None, d, tf), up_map),
            pl.BlockSpec((None, dff, tn), down_map),
        ],
        out_specs=pl.BlockSpec((tm, tn),
                               lambda i, s, te, nu, rows: (i, jnp.maximum(s - nf, 0))),
        scratch_shapes=[pltpu.VMEM((tm, d), BF16), pltpu.VMEM((nf, tm, tf), BF16)],
    )
    return pl.pallas_call(
        _moe_kernel,
        out_shape=jax.ShapeDtypeStruct((n_tiles * tm, d), F32),
        grid_spec=grid_spec,
        compiler_params=_cparams(("arbitrary", "arbitrary")),
        name="moe_grouped_swiglu",
    )(tile_expert, n_used.reshape(1), tile_rows, xs, wg, wu, wd)


def _combine_kernel(pos_ref, pos_next_ref, ys_ref, x_ref, route_ref, o_ref, buf_a, buf_b, sems):
    i = pl.program_id(0)
    tc = x_ref.shape[0]
    slot = i % 2
    unroll = 8

    def start_gather(p_ref, sl):
        def row_copy(src_row, dst_row, buf):
            return pltpu.make_async_copy(ys_ref.at[pl.ds(src_row, 1), :],
                                         buf.at[sl, pl.ds(dst_row, 1), :], sems.at[sl])

        def issue(jo, carry):
            for ji in range(unroll):
                j = jo * unroll + ji
                row_copy(p_ref[0, 0, TOP_K * j], j, buf_a).start()
                row_copy(p_ref[0, 0, TOP_K * j + 1], j, buf_b).start()
            return carry

        lax.fori_loop(0, tc // unroll, issue, 0)

    @pl.when(i == 0)
    def _():
        start_gather(pos_ref, 0)

    @pl.when(i + 1 < pl.num_programs(0))
    def _():
        start_gather(pos_next_ref, 1 - slot)

    for buf in (buf_a, buf_b):
        pltpu.make_async_copy(ys_ref.at[pl.ds(0, tc), :], buf.at[slot], sems.at[slot]).wait()

    w1 = route_ref[:, 2:3]
    w2 = route_ref[:, 3:4]
    o_ref[...] = x_ref[...] + w1 * buf_a[slot] + w2 * buf_b[slot]


def _combine(pos, ys, x, route, *, tc):
    t, d = x.shape
    n_steps = t // tc
    pos = pos.reshape(n_steps, 1, TOP_K * tc)
    pos_spec = lambda index_map: pl.BlockSpec((1, 1, TOP_K * tc), index_map,
                                              memory_space=pltpu.SMEM)
    return pl.pallas_call(
        _combine_kernel,
        out_shape=jax.ShapeDtypeStruct((t, d), F32),
        grid=(n_steps,),
        in_specs=[
            pos_spec(lambda i: (i, 0, 0)),
            pos_spec(lambda i: (jnp.minimum(i + 1, n_steps - 1), 0, 0)),
            pl.BlockSpec(memory_space=pl.ANY),
            pl.BlockSpec((tc, d), lambda i: (i, 0)),
            pl.BlockSpec((tc, LANES), lambda i: (i, 0)),
        ],
        out_specs=pl.BlockSpec((tc, d), lambda i: (i, 0)),
        scratch_shapes=[pltpu.VMEM((2, tc, d), F32), pltpu.VMEM((2, tc, d), F32),
                        pltpu.SemaphoreType.DMA((2,))],
        compiler_params=_cparams(("arbitrary",)),
        name="moe_combine",
    )(pos, pos, ys, x, route)


def _attention_layer(x, seq, mix_gain, ffn_gain, rel_bias, w_qkv, q_gain, k_gain, sink, w_o,
                     w_gate, w_up, w_down):
    t, d = x.shape
    n_heads = sink.shape[0]
    n_kv_heads = (w_qkv.shape[1] // HEAD_DIM - n_heads) // 2
    head_gain = jnp.concatenate([
        jnp.tile(q_gain * (HEAD_DIM ** -0.5 * LOG2_E), n_heads),
        jnp.tile(k_gain, n_kv_heads),
        jnp.ones((n_kv_heads * HEAD_DIM,), F32)]).reshape(1, -1)
    qkv = _norm_matmul(x, (mix_gain[:, None] * w_qkv).astype(BF16), head_gain,
                       normed_heads=n_heads + n_kv_heads, out_dtype=BF16,
                       tm=_tile(t, 512), chunk=_tile(w_qkv.shape[1], 1024))
    attn = _attention(qkv, _band_bias(rel_bias * LOG2_E), sink * LOG2_E, seq=seq, n_heads=n_heads,
                      n_kv_heads=n_kv_heads, tq=_tile(seq, 512))
    x, h = _proj_res_norm(attn, w_o.astype(BF16), x, ffn_gain, tm=_tile(t, 512))
    return _ffn(h, x, w_gate.astype(BF16), w_up.astype(BF16), w_down.astype(BF16),
                tm=_tile(t, 1024), tf=_tile(w_gate.shape[1], 512), tn=_tile(d, 512))


def _pool_moe_layer(x, seq, mix_gain, ffn_gain, w_in, w_group, scale, w_out, router_w, router_b,
                    w_gate, w_up, w_down):
    t, d = x.shape
    n_experts = router_w.shape[1]
    u = _norm_matmul(x, (mix_gain[:, None] * w_in).astype(BF16), jnp.ones((1, d), F32),
                     normed_heads=0, out_dtype=F32, tm=_tile(t, 512), chunk=_tile(d, 1024))
    rw_hi = router_w.astype(BF16)
    rw_lo = (router_w - rw_hi.astype(F32)).astype(BF16)
    rw = (jnp.zeros((d, LANES), BF16).at[:, :n_experts].set(rw_hi)
          .at[:, n_experts:2 * n_experts].set(rw_lo))
    rb = jnp.zeros((1, LANES), F32).at[0, :n_experts].set(router_b)
    x, packed, route = _pool_tail(u, w_group.astype(BF16), scale, w_out.astype(BF16), x, ffn_gain,
                                  rw, rb, seq=seq, n_experts=n_experts, tm=_tile(seq, 512))
    tm = MOE_ROW_TILE
    n_tiles = TOP_K * t // tm + n_experts
    e_flat = route[:, :TOP_K].astype(I32).reshape(-1)
    pos, tile_expert, n_used, tile_rows = _routing_plan(
        e_flat, n_experts=n_experts, tm=tm, n_tiles=n_tiles)
    xs = _dispatch(pos, tile_rows, packed, tc=_tile(t, 1024), tm=tm)
    ys = _moe(tile_expert, n_used, tile_rows, xs, w_gate, w_up, w_down, tm=tm,
              tf=_tile(w_gate.shape[2], 256), tn=_tile(d, 256), n_tiles=n_tiles)
    return _combine(pos, ys, x, route, tc=_tile(t, 256))


def kernel(x, mix_norm, ffn_norm, rel_bias, attn_w_qkv, attn_q_gain, attn_k_gain, attn_sink,
           attn_w_o, ffn_w_gate, ffn_w_up, ffn_w_down, pool_w_in, pool_w_group, pool_scale,
           pool_w_out, moe_router_w, moe_router_b, moe_w_gate, moe_w_up, moe_w_down):
    b, s, d = x.shape
    y = x.reshape(b * s, d)
    for i in range(mix_norm.shape[0]):
        j = i // 2
        if i % 2 == 0:
            y = _attention_layer(y, s, mix_norm[i], ffn_norm[i], rel_bias, attn_w_qkv[j],
                                 attn_q_gain[j], attn_k_gain[j], attn_sink[j], attn_w_o[j],
                                 ffn_w_gate[j], ffn_w_up[j], ffn_w_down[j])
        else:
            y = _pool_moe_layer(y, s, mix_norm[i], ffn_norm[i], pool_w_in[j], pool_w_group[j],
                                pool_scale[j], pool_w_out[j], moe_router_w[j], moe_router_b[j],
                                moe_w_gate[j], moe_w_up[j], moe_w_down[j])
    return y.reshape(b, s, d)
```

```python
import functools
import math

import jax
import jax.numpy as jnp
from jax import lax
from jax.experimental import pallas as pl
from jax.experimental.pallas import tpu as pltpu

F32 = jnp.float32
BF16 = jnp.bfloat16
U32 = jnp.uint32
I32 = jnp.int32

EPS = 1e-6
NEG_INF = -1e30
LOG2_E = 1.4426950408889634

LANES = 128
VMEM_LIMIT_BYTES = 56 * 1024 * 1024

HEAD_DIM = 128
GROUP = 4
WINDOW = 128
BLOCK = 128
KEY_SPAN = BLOCK + 2 * WINDOW
NUM_BUCKETS = 32
MAX_DISTANCE = 128
POOL_WINDOWS = (2, 4, 8, 16)
POOL_HALO = 8
TOP_K = 2
MOE_ROW_TILE = 1024
MOE_ROW_BLOCKS = 8


def _cparams(semantics):
    return pltpu.CompilerParams(dimension_semantics=semantics,
                                vmem_limit_bytes=VMEM_LIMIT_BYTES)


def _rms(x, gain):
    ms = jnp.mean(x * x, axis=-1, keepdims=True)
    return x * lax.rsqrt(ms + EPS) * gain


def _tile(n, want):
    want = min(want, n)
    while n % want:
        want //= 2
    return want


def _norm_matmul_kernel(x_ref, w_ref, hg_ref, o_ref, *, normed_heads, chunk):
    x = x_ref[...]
    inv = lax.rsqrt(jnp.mean(x * x, axis=-1, keepdims=True) + EPS)
    xb = x.astype(BF16)
    for c0 in range(0, o_ref.shape[1], chunk):
        acc = jnp.dot(xb, w_ref[:, c0:c0 + chunk], preferred_element_type=F32) * inv
        for h0 in range(c0, c0 + chunk, HEAD_DIM):
            sl = slice(h0, h0 + HEAD_DIM)
            piece = acc[:, h0 - c0:h0 - c0 + HEAD_DIM]
            if h0 // HEAD_DIM < normed_heads:
                piece = _rms(piece, hg_ref[:, sl])
            o_ref[:, sl] = piece.astype(o_ref.dtype)


def _norm_matmul(x, w, head_gain, *, normed_heads, out_dtype, tm, chunk):
    t, d = x.shape
    n = w.shape[1]
    return pl.pallas_call(
        functools.partial(_norm_matmul_kernel, normed_heads=normed_heads, chunk=chunk),
        out_shape=jax.ShapeDtypeStruct((t, n), out_dtype),
        grid=(t // tm,),
        in_specs=[
            pl.BlockSpec((tm, d), lambda i: (i, 0)),
            pl.BlockSpec((d, n), lambda i: (0, 0)),
            pl.BlockSpec((1, n), lambda i: (0, 0)),
        ],
        out_specs=pl.BlockSpec((tm, n), lambda i: (i, 0)),
        compiler_params=_cparams(("parallel",)),
        name="norm_matmul",
    )(x, w, head_gain)


def _attn_kernel(sink_ref, q_ref, kv_ref, kvp_ref, kvn_ref, bias_ref, o_ref, *,
                 n_kv_heads, steps_per_seq):
    i = pl.program_id(0) % steps_per_seq
    tq = q_ref.shape[0]
    nqb = tq // BLOCK
    kw = n_kv_heads * HEAD_DIM
    prev_penalty = jnp.where(i > 0, 0.0, NEG_INF)
    next_penalty = jnp.where(i < steps_per_seq - 1, 0.0, NEG_INF)
    col = lax.broadcasted_iota(I32, (GROUP * BLOCK, KEY_SPAN), 1)

    for h in range(n_kv_heads):
        ksl = slice(h * HEAD_DIM, (h + 1) * HEAD_DIM)
        vsl = slice(kw + h * HEAD_DIM, kw + (h + 1) * HEAD_DIM)
        bias = bias_ref[h * GROUP:(h + 1) * GROUP].reshape(GROUP * BLOCK, KEY_SPAN)
        for qb in range(nqb):
            rows = slice(qb * BLOCK, (qb + 1) * BLOCK)
            prev_rows = slice((qb - 1) * BLOCK, qb * BLOCK)
            next_rows = slice((qb + 1) * BLOCK, (qb + 2) * BLOCK)
            if qb == 0:
                k_prev, v_prev = kvp_ref[:, ksl], kvp_ref[:, vsl]
            else:
                k_prev, v_prev = kv_ref[prev_rows, ksl], kv_ref[prev_rows, vsl]
            if qb == nqb - 1:
                k_next, v_next = kvn_ref[:, ksl], kvn_ref[:, vsl]
            else:
                k_next, v_next = kv_ref[next_rows, ksl], kv_ref[next_rows, vsl]
            k3 = jnp.concatenate([k_prev, kv_ref[rows, ksl], k_next], axis=0)
            v3 = jnp.concatenate([v_prev, kv_ref[rows, vsl], v_next], axis=0)
            q4 = jnp.concatenate(
                [q_ref[rows, (h * GROUP + g) * HEAD_DIM:(h * GROUP + g + 1) * HEAD_DIM]
                 for g in range(GROUP)], axis=0)
            s = lax.dot_general(q4, k3, (((1,), (1,)), ((), ())),
                                preferred_element_type=F32) + bias
            if qb == 0:
                s = s + jnp.where(col < WINDOW, prev_penalty, 0.0)
            if qb == nqb - 1:
                s = s + jnp.where(col >= WINDOW + BLOCK, next_penalty, 0.0)
            outs = []
            for g in range(GROUP):
                sg = s[g * BLOCK:(g + 1) * BLOCK]
                sink = sink_ref[h * GROUP + g]
                m = jnp.maximum(jnp.max(sg, axis=-1, keepdims=True), sink)
                p = jnp.exp2(sg - m)
                denom = jnp.sum(p, axis=-1, keepdims=True) + jnp.exp2(sink - m)
                pv = jnp.dot(p.astype(BF16), v3, preferred_element_type=F32)
                outs.append(pv / denom)
            for g in range(GROUP):
                o_ref[rows, (h * GROUP + g) * HEAD_DIM:(h * GROUP + g + 1) * HEAD_DIM] = (
                    outs[g].astype(o_ref.dtype))


def _attention(qkv, bias, sink, *, seq, n_heads, n_kv_heads, tq):
    t = qkv.shape[0]
    dq = n_heads * HEAD_DIM
    dkv = 2 * n_kv_heads * HEAD_DIM
    assert dq % dkv == 0
    kv_col = dq // dkv
    steps_per_seq = seq // tq
    blocks_per_step = tq // BLOCK
    last_block = t // BLOCK - 1
    return pl.pallas_call(
        functools.partial(_attn_kernel, n_kv_heads=n_kv_heads, steps_per_seq=steps_per_seq),
        out_shape=jax.ShapeDtypeStruct((t, dq), BF16),
        grid=(t // tq,),
        in_specs=[
            pl.BlockSpec(memory_space=pltpu.SMEM),
            pl.BlockSpec((tq, dq), lambda r: (r, 0)),
            pl.BlockSpec((tq, dkv), lambda r: (r, kv_col)),
            pl.BlockSpec((BLOCK, dkv),
                         lambda r: (jnp.maximum(r * blocks_per_step - 1, 0), kv_col)),
            pl.BlockSpec((BLOCK, dkv),
                         lambda r: (jnp.minimum((r + 1) * blocks_per_step, last_block), kv_col)),
            pl.BlockSpec((n_heads, BLOCK, KEY_SPAN), lambda r: (0, 0, 0)),
        ],
        out_specs=pl.BlockSpec((tq, dq), lambda r: (r, 0)),
        compiler_params=_cparams(("parallel",)),
        name="window_attention",
    )(sink, qkv, qkv, qkv, qkv, bias)


def _t5_bucket(rel):
    half = NUM_BUCKETS // 2
    ret = jnp.where(rel > 0, half, 0)
    n = jnp.abs(rel)
    max_exact = half // 2
    nf = jnp.maximum(n, 1).astype(F32)
    large = max_exact + (jnp.log(nf / max_exact) / math.log(MAX_DISTANCE / max_exact)
                         * (half - max_exact)).astype(I32)
    large = jnp.minimum(large, half - 1)
    return ret + jnp.where(n < max_exact, n, large)


def _band_bias(rel_bias):
    qi = jnp.arange(BLOCK)[:, None]
    kj = jnp.arange(KEY_SPAN)[None, :]
    rel = kj - WINDOW - qi
    onehot = (_t5_bucket(rel)[:, :, None] == jnp.arange(NUM_BUCKETS)[None, None, :]).astype(F32)
    bias = jnp.einsum("qkb,bh->hqk", onehot, rel_bias.astype(F32),
                      precision=lax.Precision.HIGHEST)
    return jnp.where((jnp.abs(rel) <= WINDOW)[None], bias, NEG_INF)


def _proj_res_norm_kernel(a_ref, w_ref, x_ref, g_ref, xo_ref, ho_ref):
    xo = x_ref[...] + jnp.dot(a_ref[...], w_ref[...], preferred_element_type=F32)
    xo_ref[...] = xo
    ho_ref[...] = _rms(xo, g_ref[...]).astype(BF16)


def _proj_res_norm(a, w, x, gain, *, tm):
    t, d = x.shape
    k = a.shape[1]
    return pl.pallas_call(
        _proj_res_norm_kernel,
        out_shape=(jax.ShapeDtypeStruct((t, d), F32), jax.ShapeDtypeStruct((t, d), BF16)),
        grid=(t // tm,),
        in_specs=[
            pl.BlockSpec((tm, k), lambda i: (i, 0)),
            pl.BlockSpec((k, d), lambda i: (0, 0)),
            pl.BlockSpec((tm, d), lambda i: (i, 0)),
            pl.BlockSpec((1, d), lambda i: (0, 0)),
        ],
        out_specs=(pl.BlockSpec((tm, d), lambda i: (i, 0)),
                   pl.BlockSpec((tm, d), lambda i: (i, 0))),
        compiler_params=_cparams(("parallel",)),
        name="proj_res_norm",
    )(a, w, x, gain.reshape(1, d))


def _swiglu_act(h, wg, wu):
    g = jnp.dot(h, wg.astype(BF16), preferred_element_type=F32)
    u = jnp.dot(h, wu.astype(BF16), preferred_element_type=F32)
    return (g * jax.nn.sigmoid(g) * u).astype(BF16)


def _down_proj(a_scr, wd, rows):
    a = jnp.concatenate([a_scr[k, :rows] for k in range(a_scr.shape[0])], axis=1)
    return jnp.dot(a, wd.astype(BF16), preferred_element_type=F32)


def _ffn_kernel(h_ref, x_ref, wg_ref, wu_ref, wd_ref, o_ref, a_scr):
    s = pl.program_id(1)
    nf = a_scr.shape[0]

    @pl.when(s < nf)
    def _():
        a_scr[s] = _swiglu_act(h_ref[...], wg_ref[...], wu_ref[...])

    @pl.when(s >= nf)
    def _():
        o_ref[...] = x_ref[...] + _down_proj(a_scr, wd_ref[...], o_ref.shape[0])


def _ffn(h, x, wg, wu, wd, *, tm, tf, tn):
    t, d = x.shape
    dff = wg.shape[1]
    nf = dff // tf
    nn = d // tn
    up_map = lambda i, s: (0, jnp.minimum(s, nf - 1))
    out_map = lambda i, s: (i, jnp.maximum(s - nf, 0))
    return pl.pallas_call(
        _ffn_kernel,
        out_shape=jax.ShapeDtypeStruct((t, d), F32),
        grid=(t // tm, nf + nn),
        in_specs=[
            pl.BlockSpec((tm, d), lambda i, s: (i, 0)),
            pl.BlockSpec((tm, tn), out_map),
            pl.BlockSpec((d, tf), up_map),
            pl.BlockSpec((d, tf), up_map),
            pl.BlockSpec((dff, tn), lambda i, s: (0, jnp.maximum(s - nf, 0))),
        ],
        out_specs=pl.BlockSpec((tm, tn), out_map),
        scratch_shapes=[pltpu.VMEM((nf, tm, tf), BF16)],
        compiler_params=_cparams(("parallel", "arbitrary")),
        name="dense_swiglu",
    )(h, x, wg, wu, wd)


def _pack_bf16_pairs(lo, hi):
    lo_bits = lax.bitcast_convert_type(lo.astype(BF16).astype(F32), U32)
    hi_bits = lax.bitcast_convert_type(hi.astype(BF16).astype(F32), U32)
    return (lo_bits >> 16) | hi_bits


def _unpack_bf16_pairs(words):
    lo = lax.bitcast_convert_type(words << 16, F32).astype(BF16)
    hi = lax.bitcast_convert_type(words & jnp.uint32(0xFFFF0000), F32).astype(BF16)
    return lo, hi


def _pool_tail_kernel(u_ref, up_ref, un_ref, wgrp_ref, scale_ref, wout_ref, x_ref, g_ref,
                      rw_ref, rb_ref, xo_ref, hp_ref, route_ref, *, seq, n_experts):
    tm, d = x_ref.shape
    gd = d // len(POOL_WINDOWS)
    ext = tm + 2 * POOL_HALO
    start = (pl.program_id(0) * tm) % seq
    pos = start + lax.broadcasted_iota(I32, (tm, 1), 0)

    ys = []
    for gi, win in enumerate(POOL_WINDOWS):
        cs = slice(gi * gd, (gi + 1) * gd)
        u = u_ref[:, cs]
        up = jnp.where(start > 0, up_ref[:, cs], 0.0)
        un = jnp.where(start + tm < seq, un_ref[:, cs], 0.0)
        a = jnp.concatenate([up, u, un], axis=0)
        a = a + pltpu.roll(a, 1, axis=0)
        w = 2
        while w < win:
            a = pltpu.roll(a, w // 2, axis=0) + pltpu.roll(a, ext - w // 2, axis=0)
            w *= 2
        half = win // 2
        count = (jnp.minimum(pos + half, seq) - jnp.maximum(pos - half, 0)).astype(F32)
        diff = a[POOL_HALO:POOL_HALO + tm] / count - u
        y = jnp.dot(diff.astype(BF16), wgrp_ref[gi], preferred_element_type=F32)
        ys.append((y * scale_ref[:, cs]).astype(BF16))
    y = jnp.concatenate(ys, axis=1)
    xo = x_ref[...] + jnp.dot(y, wout_ref[...], preferred_element_type=F32)
    xo_ref[...] = xo

    h = _rms(xo, g_ref[...])
    hp_ref[...] = _pack_bf16_pairs(h[:, :d // 2], h[:, d // 2:])

    h_hi = h.astype(BF16)
    h_lo = (h - h_hi.astype(F32)).astype(BF16)
    prod = jnp.dot(jnp.concatenate([h_hi, h_lo], axis=0), rw_ref[...],
                   preferred_element_type=F32)
    part = prod[:tm] + prod[tm:]
    logits = part + pltpu.roll(part, LANES - n_experts, axis=1) + rb_ref[...]
    lane = lax.broadcasted_iota(I32, logits.shape, 1)
    logits = jnp.where(lane < n_experts, logits, -jnp.inf)
    lane_f = lane.astype(F32)
    m1 = jnp.max(logits, axis=-1, keepdims=True)
    i1 = jnp.min(jnp.where(logits == m1, lane_f, float(LANES)), axis=-1, keepdims=True)
    rest = jnp.where(lane_f == i1, -jnp.inf, logits)
    m2 = jnp.max(rest, axis=-1, keepdims=True)
    i2 = jnp.min(jnp.where(rest == m2, lane_f, float(LANES)), axis=-1, keepdims=True)
    e2 = jnp.exp(m2 - m1)
    w1 = 1.0 / (1.0 + e2)
    w2 = e2 / (1.0 + e2)
    route_ref[...] = jnp.where(
        lane == 0, i1, jnp.where(lane == 1, i2,
                                 jnp.where(lane == 2, w1, jnp.where(lane == 3, w2, 0.0))))


def _pool_tail(u, wgrp, scale, wout, x, gain, rw, rb, *, seq, n_experts, tm):
    t, d = x.shape
    ng, gd, _ = wgrp.shape
    halo_blocks = tm // POOL_HALO
    last_halo = t // POOL_HALO - 1
    return pl.pallas_call(
        functools.partial(_pool_tail_kernel, seq=seq, n_experts=n_experts),
        out_shape=(jax.ShapeDtypeStruct((t, d), F32),
                   jax.ShapeDtypeStruct((t, d // 2), U32),
                   jax.ShapeDtypeStruct((t, LANES), F32)),
        grid=(t // tm,),
        in_specs=[
            pl.BlockSpec((tm, d), lambda i: (i, 0)),
            pl.BlockSpec((POOL_HALO, d), lambda i: (jnp.maximum(i * halo_blocks - 1, 0), 0)),
            pl.BlockSpec((POOL_HALO, d),
                         lambda i: (jnp.minimum((i + 1) * halo_blocks, last_halo), 0)),
            pl.BlockSpec((ng, gd, gd), lambda i: (0, 0, 0)),
            pl.BlockSpec((1, d), lambda i: (0, 0)),
            pl.BlockSpec((d, d), lambda i: (0, 0)),
            pl.BlockSpec((tm, d), lambda i: (i, 0)),
            pl.BlockSpec((1, d), lambda i: (0, 0)),
            pl.BlockSpec((d, LANES), lambda i: (0, 0)),
            pl.BlockSpec((1, LANES), lambda i: (0, 0)),
        ],
        out_specs=(pl.BlockSpec((tm, d), lambda i: (i, 0)),
                   pl.BlockSpec((tm, d // 2), lambda i: (i, 0)),
                   pl.BlockSpec((tm, LANES), lambda i: (i, 0))),
        compiler_params=_cparams(("parallel",)),
        name="pool_tail",
    )(u, u, u, wgrp, scale.reshape(1, d), wout, x, gain.reshape(1, d), rw, rb)


def _routing_plan(e_flat, *, n_experts, tm, n_tiles):
    block = tm // MOE_ROW_BLOCKS
    experts = jnp.arange(n_experts, dtype=I32)
    onehot = (e_flat[:, None] == experts[None, :]).astype(I32)
    csum = jnp.cumsum(onehot, axis=0)
    rank = jnp.sum(onehot * csum, axis=1) - 1
    counts = csum[-1]
    tiles_per = (counts + tm - 1) // tm
    blocks_per = (counts + block - 1) // block
    safe_tiles = jnp.maximum(tiles_per, 1)
    cap_lo = (blocks_per // safe_tiles) * block
    cap_hi = cap_lo + block
    rem = blocks_per % safe_tiles
    tile_end = jnp.cumsum(tiles_per)
    first_tile = tile_end - tiles_per
    n_used = tile_end[-1]

    tile_ids = jnp.arange(n_tiles, dtype=I32)
    used = tile_ids < n_used
    tile_expert = jnp.sum((tile_ids[:, None] >= tile_end[None, :]).astype(I32), axis=1)
    last_expert = jnp.sum((n_used - 1 >= tile_end).astype(I32))
    tile_expert = jnp.minimum(tile_expert, last_expert).astype(I32)
    onehot_t = (tile_expert[:, None] == experts[None, :]).astype(I32)
    per_tile = lambda v: jnp.sum(onehot_t * v[None, :], axis=1)
    j = tile_ids - per_tile(first_tile)
    hi, lo, n_hi = per_tile(cap_hi), per_tile(cap_lo), per_tile(rem)
    start = jnp.where(j < n_hi, j * hi, n_hi * hi + (j - n_hi) * lo)
    cap = jnp.where(j < n_hi, hi, lo)
    tile_rows = jnp.where(used, jnp.clip(per_tile(counts) - start, 0, cap), 0)
    reaches = ((e_flat[:, None] == tile_expert[None, :]) & used[None, :]
               & (rank[:, None] >= start[None, :]))
    tile_of = jnp.max(jnp.where(reaches, tile_ids[None, :], 0), axis=1)
    start_of = jnp.max(jnp.where(reaches, start[None, :], 0), axis=1)
    pos = tile_of * tm + rank - start_of
    return pos.astype(I32), tile_expert, n_used.astype(I32), tile_rows.astype(I32)


def _dispatch_kernel(pos_ref, rows_ref, src_ref, dst_ref, sem, *, tm):
    i = pl.program_id(0)
    tc = src_ref.shape[0]
    unroll = 8

    def row_copy(src_row, dst_row):
        return pltpu.make_async_copy(src_ref.at[pl.ds(src_row, 1), :],
                                     dst_ref.at[pl.ds(dst_row, 1), :], sem)

    def issue(jo, carry):
        for ji in range(unroll):
            j = jo * unroll + ji
            for k in range(TOP_K):
                row_copy(j, pos_ref[0, 0, TOP_K * j + k]).start(priority=k % 2)
        return carry

    lax.fori_loop(0, tc // unroll, issue, 0)
    for k in range(TOP_K):
        pltpu.make_async_copy(src_ref, dst_ref.at[pl.ds(0, tc), :], sem).wait()

    @pl.when(i == pl.num_programs(0) - 1)
    def _():
        n_tiles = dst_ref.shape[0] // tm

        def fill_tile(tile, n_used):
            rows = rows_ref[tile]
            first = tile * tm + rows
            n_pad = jnp.where(rows > 0, tm - rows, 0)

            def fill(j, carry):
                row_copy(0, first + j).start()
                return carry

            def fill_wait(j, carry):
                row_copy(0, 0).wait()
                return carry

            lax.fori_loop(0, n_pad, fill, 0)
            lax.fori_loop(0, n_pad, fill_wait, 0)
            return n_used + (rows > 0).astype(I32)

        n_used = lax.fori_loop(0, n_tiles, fill_tile, jnp.int32(0))
        fill_rows = math.gcd(tm, tc)

        def fill_block(j, carry):
            cp = pltpu.make_async_copy(
                src_ref.at[pl.ds(0, fill_rows), :],
                dst_ref.at[pl.ds(pl.multiple_of(j * fill_rows, fill_rows), fill_rows), :], sem)
            cp.start()
            cp.wait()
            return carry

        lax.fori_loop(n_used * (tm // fill_rows), dst_ref.shape[0] // fill_rows, fill_block, 0)


def _dispatch(pos, tile_rows, packed, *, tc, tm):
    t, width = packed.shape
    n_steps = t // tc
    return pl.pallas_call(
        functools.partial(_dispatch_kernel, tm=tm),
        out_shape=jax.ShapeDtypeStruct((tile_rows.shape[0] * tm, width), packed.dtype),
        grid=(n_steps,),
        in_specs=[
            pl.BlockSpec((1, 1, TOP_K * tc), lambda i: (i, 0, 0), memory_space=pltpu.SMEM),
            pl.BlockSpec(memory_space=pltpu.SMEM),
            pl.BlockSpec((tc, width), lambda i: (i, 0)),
        ],
        out_specs=pl.BlockSpec(memory_space=pl.ANY),
        scratch_shapes=[pltpu.SemaphoreType.DMA(())],
        compiler_params=_cparams(("arbitrary",)),
        name="moe_dispatch",
    )(pos.reshape(n_steps, 1, TOP_K * tc), tile_rows, packed)


def _moe_kernel(te_ref, nu_ref, rows_ref, xs_ref, wg_ref, wu_ref, wd_ref, o_ref, x_scr, a_scr):
    del te_ref, nu_ref
    i = pl.program_id(0)
    s = pl.program_id(1)
    tm, d = x_scr.shape
    nf = a_scr.shape[0]
    block = tm // MOE_ROW_BLOCKS
    n_blocks = (rows_ref[i] + block - 1) // block

    @pl.when((n_blocks > 0) & (s == 0))
    def _():
        lo, hi = _unpack_bf16_pairs(xs_ref[...])
        x_scr[:, :d // 2] = lo
        x_scr[:, d // 2:] = hi

    for k in range(1, MOE_ROW_BLOCKS + 1):
        m = k * block

        @pl.when((n_blocks == k) & (s < nf))
        def _():
            a_scr[s, :m] = _swiglu_act(x_scr[:m], wg_ref[...], wu_ref[...])

        @pl.when((n_blocks == k) & (s >= nf))
        def _():
            o_ref[:m] = _down_proj(a_scr, wd_ref[...], m)
            if m < tm:
                o_ref[m:] = jnp.zeros((tm - m, o_ref.shape[1]), o_ref.dtype)

    @pl.when((n_blocks == 0) & (s >= nf))
    def _():
        o_ref[...] = jnp.zeros(o_ref.shape, o_ref.dtype)


def _moe(tile_expert, n_used, tile_rows, xs, wg, wu, wd, *, tm, tf, tn, n_tiles):
    n_experts, d, dff = wg.shape
    nf = dff // tf
    nn = d // tn

    def row_map(i, s, te, nu, rows):
        return (jnp.minimum(i, nu[0] - 1), 0)

    def up_map(i, s, te, nu, rows):
        return (te[i], 0, jnp.where(i < nu[0], jnp.minimum(s, nf - 1), nf - 1))

    def down_map(i, s, te, nu, rows):
        return (te[i], 0, jnp.where(i < nu[0], jnp.maximum(s - nf, 0), nn - 1))

    grid_spec = pltpu.PrefetchScalarGridSpec(
        num_scalar_prefetch=3,
        grid=(n_tiles, nf + nn),
        in_specs=[
            pl.BlockSpec((tm, d // 2), row_map),
            pl.BlockSpec((None, d, tf), up_map),
            pl.BlockSpec((None, d, tf), up_map),
            pl.BlockSpec((None, dff, tn), down_map),
        ],
        out_specs=pl.BlockSpec((tm, tn),
                               lambda i, s, te, nu, rows: (i, jnp.maximum(s - nf, 0))),
        scratch_shapes=[pltpu.VMEM((tm, d), BF16), pltpu.VMEM((nf, tm, tf), BF16)],
    )
    return pl.pallas_call(
        _moe_kernel,
        out_shape=jax.ShapeDtypeStruct((n_tiles * tm, d), F32),
        grid_spec=grid_spec,
        compiler_params=_cparams(("arbitrary", "arbitrary")),
        name="moe_grouped_swiglu",
    )(tile_expert, n_used.reshape(1), tile_rows, xs, wg, wu, wd)


def _combine_kernel(pos_ref, pos_next_ref, ys_ref, x_ref, route_ref, o_ref, buf_a, buf_b, sems):
    i = pl.program_id(0)
    tc = x_ref.shape[0]
    slot = i % 2
    unroll = 8

    def start_gather(p_ref, sl):
        def row_copy(src_row, dst_row, buf):
            return pltpu.make_async_copy(ys_ref.at[pl.ds(src_row, 1), :],
                                         buf.at[sl, pl.ds(dst_row, 1), :], sems.at[sl])

        def issue(jo, carry):
            for ji in range(unroll):
                j = jo * unroll + ji
                row_copy(p_ref[0, 0, TOP_K * j], j, buf_a).start(priority=0)
                row_copy(p_ref[0, 0, TOP_K * j + 1], j, buf_b).start(priority=1)
            return carry

        lax.fori_loop(0, tc // unroll, issue, 0)

    @pl.when(i == 0)
    def _():
        start_gather(pos_ref, 0)

    @pl.when(i + 1 < pl.num_programs(0))
    def _():
        start_gather(pos_next_ref, 1 - slot)

    for buf in (buf_a, buf_b):
        pltpu.make_async_copy(ys_ref.at[pl.ds(0, tc), :], buf.at[slot], sems.at[slot]).wait()

    w1 = route_ref[:, 2:3]
    w2 = route_ref[:, 3:4]
    o_ref[...] = x_ref[...] + w1 * buf_a[slot] + w2 * buf_b[slot]


def _combine(pos, ys, x, route, *, tc):
    t, d = x.shape
    n_steps = t // tc
    pos = pos.reshape(n_steps, 1, TOP_K * tc)
    pos_spec = lambda index_map: pl.BlockSpec((1, 1, TOP_K * tc), index_map,
                                              memory_space=pltpu.SMEM)
    return pl.pallas_call(
        _combine_kernel,
        out_shape=jax.ShapeDtypeStruct((t, d), F32),
        grid=(n_steps,),
        in_specs=[
            pos_spec(lambda i: (i, 0, 0)),
            pos_spec(lambda i: (jnp.minimum(i + 1, n_steps - 1), 0, 0)),
            pl.BlockSpec(memory_space=pl.ANY),
            pl.BlockSpec((tc, d), lambda i: (i, 0)),
            pl.BlockSpec((tc, LANES), lambda i: (i, 0)),
        ],
        out_specs=pl.BlockSpec((tc, d), lambda i: (i, 0)),
        scratch_shapes=[pltpu.VMEM((2, tc, d), F32), pltpu.VMEM((2, tc, d), F32),
                        pltpu.SemaphoreType.DMA((2,))],
        compiler_params=_cparams(("arbitrary",)),
        name="moe_combine",
    )(pos, pos, ys, x, route)


def _attention_layer(x, seq, mix_gain, ffn_gain, rel_bias, w_qkv, q_gain, k_gain, sink, w_o,
                     w_gate, w_up, w_down):
    t, d = x.shape
    n_heads = sink.shape[0]
    n_kv_heads = (w_qkv.shape[1] // HEAD_DIM - n_heads) // 2
    head_gain = jnp.concatenate([
        jnp.tile(q_gain * (HEAD_DIM ** -0.5 * LOG2_E), n_heads),
        jnp.tile(k_gain, n_kv_heads),
        jnp.ones((n_kv_heads * HEAD_DIM,), F32)]).reshape(1, -1)
    qkv = _norm_matmul(x, (mix_gain[:, None] * w_qkv).astype(BF16), head_gain,
                       normed_heads=n_heads + n_kv_heads, out_dtype=BF16,
                       tm=_tile(t, 512), chunk=_tile(w_qkv.shape[1], 1024))
    attn = _attention(qkv, _band_bias(rel_bias * LOG2_E), sink * LOG2_E, seq=seq, n_heads=n_heads,
                      n_kv_heads=n_kv_heads, tq=_tile(seq, 512))
    x, h = _proj_res_norm(attn, w_o.astype(BF16), x, ffn_gain, tm=_tile(t, 512))
    return _ffn(h, x, w_gate.astype(BF16), w_up.astype(BF16), w_down.astype(BF16),
                tm=_tile(t, 1024), tf=_tile(w_gate.shape[1], 512), tn=_tile(d, 512))


def _pool_moe_layer(x, seq, mix_gain, ffn_gain, w_in, w_group, scale, w_out, router_w, router_b,
                    w_gate, w_up, w_down):
    t, d = x.shape
    n_experts = router_w.shape[1]
    u = _norm_matmul(x, (mix_gain[:, None] * w_in).astype(BF16), jnp.ones((1, d), F32),
                     normed_heads=0, out_dtype=F32, tm=_tile(t, 512), chunk=_tile(d, 1024))
    rw_hi = router_w.astype(BF16)
    rw_lo = (router_w - rw_hi.astype(F32)).astype(BF16)
    rw = (jnp.zeros((d, LANES), BF16).at[:, :n_experts].set(rw_hi)
          .at[:, n_experts:2 * n_experts].set(rw_lo))
    rb = jnp.zeros((1, LANES), F32).at[0, :n_experts].set(router_b)
    x, packed, route = _pool_tail(u, w_group.astype(BF16), scale, w_out.astype(BF16), x, ffn_gain,
                                  rw, rb, seq=seq, n_experts=n_experts, tm=_tile(seq, 512))
    tm = MOE_ROW_TILE
    n_tiles = TOP_K * t // tm + n_experts
    e_flat = route[:, :TOP_K].astype(I32).reshape(-1)
    pos, tile_expert, n_used, tile_rows = _routing_plan(
        e_flat, n_experts=n_experts, tm=tm, n_tiles=n_tiles)
    xs = _dispatch(pos, tile_rows, packed, tc=_tile(t, 1024), tm=tm)
    ys = _moe(tile_expert, n_used, tile_rows, xs, w_gate, w_up, w_down, tm=tm,
              tf=_tile(w_gate.shape[2], 256), tn=_tile(d, 256), n_tiles=n_tiles)
    return _combine(pos, ys, x, route, tc=_tile(t, 256))


def kernel(x, mix_norm, ffn_norm, rel_bias, attn_w_qkv, attn_q_gain, attn_k_gain, attn_sink,
           attn_w_o, ffn_w_gate, ffn_w_up, ffn_w_down, pool_w_in, pool_w_group, pool_scale,
           pool_w_out, moe_router_w, moe_router_b, moe_w_gate, moe_w_up, moe_w_down):
    b, s, d = x.shape
    y = x.reshape(b * s, d)
    for i in range(mix_norm.shape[0]):
        j = i // 2
        if i % 2 == 0:
            y = _attention_layer(y, s, mix_norm[i], ffn_norm[i], rel_bias, attn_w_qkv[j],
                                 attn_q_gain[j], attn_k_gain[j], attn_sink[j], attn_w_o[j],
                                 ffn_w_gate[j], ffn_w_up[j], ffn_w_down[j])
        else:
            y = _pool_moe_layer(y, s, mix_norm[i], ffn_norm[i], pool_w_in[j], pool_w_group[j],
                                pool_scale[j], pool_w_out[j], moe_router_w[j], moe_router_b[j],
                                moe_w_gate[j], moe_w_up[j], moe_w_down[j])
    return y.reshape(b, s, d)
```
